```python
import jax
import jax.numpy as jnp
from jax import lax
import numpy as np

D_MODEL = 1024
BATCH = 8
SEQ = 2048
DEPTH = 1

D_MIX = D_MODEL
D_RWKV = D_MIX // 2
D_HGRN = D_MIX - D_RWKV
RWKV_HEAD = 64
RWKV_HEADS = D_RWKV // RWKV_HEAD
RANK_W = 64
RANK_A = 64
RANK_G = 128
HGRN_HEAD = 128
HGRN_HEADS = D_HGRN // HGRN_HEAD
CHUNK = 64
CONV_W = 4
N_GROUPS = 4
EXPERTS_PER_GROUP = 8
N_EXPERTS = N_GROUPS * EXPERTS_PER_GROUP
TOP_K = 2
D_EXPERT = 512
RMS_EPS = 1e-6
GN_EPS = 64e-5
L2_EPS = 1e-12
D_RWKV_IN = 3 * D_RWKV + RANK_W + RANK_A + RANK_G
D_HGRN_IN = 4 * D_HGRN
D_IN = D_RWKV_IN + D_HGRN_IN

kernel_name = 'hybrid_rwkv7_hgrn2_hmoe'


def rms_norm(x, gain):
    xf = x.astype(jnp.float32)
    y = xf * lax.rsqrt(jnp.mean(xf * xf, axis=-1, keepdims=True) + RMS_EPS)
    return (y * gain.astype(jnp.float32)).astype(x.dtype)


def token_shift(x):
    return jnp.pad(x, ((0, 0), (1, 0), (0, 0)))[:, :-1]


def causal_dwconv(x, w):
    return lax.conv_general_dilated(
        x, w[:, None, :].astype(x.dtype), window_strides=(1,),
        padding=[(CONV_W - 1, 0)], dimension_numbers=('NWC', 'WIO', 'NWC'),
        feature_group_count=x.shape[-1])


def rwkv7_mixer(p, mu, w0, w_up, a0, a_up, g_up, k_k, k_a, r_k, ln_w, ln_b):
    B, S, _ = p.shape
    f32 = jnp.float32
    p = p.astype(f32)
    p = p + mu * (token_shift(p) - p)
    splits = [D_RWKV, 2 * D_RWKV, 3 * D_RWKV, 3 * D_RWKV + RANK_W, 3 * D_RWKV + RANK_W + RANK_A]
    r, k, v, dw, da, dg = jnp.split(p, splits, axis=-1)
    w = -jax.nn.softplus(-(w0 + jnp.tanh(dw) @ w_up)) - 0.5
    decay = jnp.exp(-jnp.exp(w))
    a = jax.nn.sigmoid(a0 + da @ a_up)
    g = jax.nn.sigmoid(dg) @ g_up
    hs = lambda t: t.reshape(B, S, RWKV_HEADS, RWKV_HEAD)
    kk = hs(k * k_k)
    kk = kk / jnp.maximum(jnp.linalg.norm(kk, axis=-1, keepdims=True), L2_EPS)
    k = k * (1.0 + (a - 1.0) * k_a)
    r, k, v, decay, a = hs(r), hs(k), hs(v), hs(decay), hs(a)

    def step(state, inp):
        r_t, w_t, k_t, v_t, kk_t, a_t = inp
        sa = jnp.einsum('bhvk,bhk->bhv', state, -kk_t)
        state = (state * w_t[:, :, None, :]
                 + sa[..., None] * (kk_t * a_t)[:, :, None, :]
                 + v_t[..., None] * k_t[:, :, None, :])
        return state, jnp.einsum('bhvk,bhk->bhv', state, r_t)

    xs = tuple(jnp.moveaxis(t, 1, 0) for t in (r, decay, k, v, kk, a))
    s0 = jnp.zeros((B, RWKV_HEADS, RWKV_HEAD, RWKV_HEAD), f32)
    _, y = lax.scan(step, s0, xs)
    y = jnp.moveaxis(y, 0, 1)
    mean = jnp.mean(y, axis=-1, keepdims=True)
    var = jnp.mean(jnp.square(y - mean), axis=-1, keepdims=True)
    y = ((y - mean) * lax.rsqrt(var + GN_EPS) * ln_w.reshape(RWKV_HEADS, RWKV_HEAD)
         + ln_b.reshape(RWKV_HEADS, RWKV_HEAD))
    y = y + jnp.sum(r * k * r_k, axis=-1, keepdims=True) * v
    return y.reshape(B, S, D_RWKV) * g


def hgrn2_mixer(p, conv_w, lb, norm_g):
    B, S, _ = p.shape
    f32 = jnp.float32
    qfi = causal_dwconv(p[..., :3 * D_HGRN], conv_w).astype(f32)
    q, f, i = jnp.split(qfi, 3, axis=-1)
    g = p[..., 3 * D_HGRN:].astype(f32)
    q = jax.nn.silu(q)
    log_f = jnp.logaddexp(jnp.log(lb), jnp.log1p(-lb) + jax.nn.log_sigmoid(f))
    k = (1.0 - lb) * jax.nn.sigmoid(-f)
    n_chunks = S // CHUNK

    def chunks(t):
        return t.reshape(B, n_chunks, CHUNK, HGRN_HEADS, HGRN_HEAD).transpose(1, 0, 3, 2, 4)

    causal = jnp.tril(jnp.ones((CHUNK, CHUNK), dtype=bool))

    def chunk_step(state, inp):
        q_c, k_c, v_c, lf_c = inp
        b = jnp.cumsum(lf_c, axis=2)
        diff = b[:, :, :, None, :] - b[:, :, None, :, :]
        decay = jnp.exp(jnp.where(causal[:, :, None], diff, -jnp.inf))
        scores = jnp.einsum('bhtk,bhsk,bhtsk->bhts', q_c, k_c, decay)
        o = (jnp.einsum('bhts,bhsv->bhtv', scores, v_c)
             + jnp.einsum('bhtk,bhkv->bhtv', q_c * jnp.exp(b), state))
        b_last = b[:, :, -1:, :]
        state = (jnp.exp(b_last[:, :, 0, :, None]) * state
                 + jnp.einsum('bhsk,bhsv->bhkv', k_c * jnp.exp(b_last - b), v_c))
        return state, o

    s0 = jnp.zeros((B, HGRN_HEADS, HGRN_HEAD, HGRN_HEAD), f32)
    _, o = lax.scan(chunk_step, s0, (chunks(q), chunks(k), chunks(i), chunks(log_f)))
    o = o.transpose(1, 0, 3, 2, 4).reshape(B, S, HGRN_HEADS, HGRN_HEAD)
    o = o * lax.rsqrt(jnp.mean(o * o, axis=-1, keepdims=True) + RMS_EPS) * norm_g.reshape(HGRN_HEADS, HGRN_HEAD)
    return o.reshape(B, S, D_HGRN) * jax.nn.silu(g)


def hier_moe(x, wg, bg, we, be, w_gate, w_up, w_down):
    B, S, D = x.shape
    f32 = jnp.float32
    t = x.reshape(B * S, D)
    gp = jax.nn.softmax((t @ wg + bg).astype(f32), axis=-1)
    g_w, g_idx = lax.top_k(gp, 1)
    el = (t @ we + be).astype(f32).reshape(-1, N_GROUPS, EXPERTS_PER_GROUP)
    el = jnp.take_along_axis(el, g_idx[:, :, None], axis=1)[:, 0]
    e_w, e_idx = lax.top_k(jax.nn.softmax(el, axis=-1), TOP_K)
    e_w = e_w / jnp.sum(e_w, axis=-1, keepdims=True)
    ids = g_idx * EXPERTS_PER_GROUP + e_idx
    gates = jnp.sum(jax.nn.one_hot(ids, N_EXPERTS, dtype=f32) * (g_w * e_w)[..., None], axis=1)
    y = jnp.zeros((B * S, D), f32)
    for e in range(N_EXPERTS):
        h = jax.nn.silu(t @ w_gate[e]) * (t @ w_up[e])
        y = y + gates[:, e:e + 1] * (h @ w_down[e])
    return y.reshape(B, S, D).astype(x.dtype)


def setup_inputs(seed: int = 0) -> dict:
    key = jax.random.key(seed)
    keys = jax.random.split(key, 32)
    counter = [0]

    def nxt():
        k = keys[counter[0]]
        counter[0] += 1
        return k

    def nrm(shape, scale):
        return scale * jax.random.normal(nxt(), shape, jnp.float32)

    L = DEPTH
    return {
        'x': nrm((BATCH, SEQ, D_MODEL), 1.0),
        'norm1_g': 1.0 + nrm((L, D_MODEL), 0.02),
        'w_in': nrm((L, D_MODEL, D_IN), D_MODEL ** -0.5),
        'rwkv_mu': jax.random.uniform(nxt(), (L, D_RWKV_IN), jnp.float32),
        'rwkv_w0': jax.random.uniform(nxt(), (L, D_RWKV), jnp.float32, -6.5, -1.0),
        'rwkv_w_up': nrm((L, RANK_W, D_RWKV), 0.1 * RANK_W ** -0.5),
        'rwkv_a0': nrm((L, D_RWKV), 0.1),
        'rwkv_a_up': nrm((L, RANK_A, D_RWKV), 0.1 * RANK_A ** -0.5),
        'rwkv_g_up': nrm((L, RANK_G, D_RWKV), RANK_G ** -0.5),
        'rwkv_k_k': 0.85 + nrm((L, D_RWKV), 0.02),
        'rwkv_k_a': 1.0 + nrm((L, D_RWKV), 0.02),
        'rwkv_r_k': nrm((L, RWKV_HEADS, RWKV_HEAD), 0.1),
        'rwkv_ln_w': 1.0 + nrm((L, D_RWKV), 0.02),
        'rwkv_ln_b': nrm((L, D_RWKV), 0.02),
        'hgrn_conv_w': nrm((L, CONV_W, 3 * D_HGRN), CONV_W ** -0.5),
        'hgrn_lb_logits': nrm((L + 1, D_HGRN), 0.1),
        'hgrn_norm_g': 1.0 + nrm((L, D_HGRN), 0.02),
        'w_out': nrm((L, D_MIX, D_MODEL), D_MIX ** -0.5),
        'norm2_g': 1.0 + nrm((L, D_MODEL), 0.02),
        'router_g_w': nrm((L, D_MODEL, N_GROUPS), D_MODEL ** -0.5),
        'router_g_b': nrm((L, N_GROUPS), 0.01),
        'router_e_w': nrm((L, D_MODEL, N_EXPERTS), D_MODEL ** -0.5),
        'router_e_b': nrm((L, N_EXPERTS), 0.01),
        'exp_w_gate': nrm((L, N_EXPERTS, D_MODEL, D_EXPERT), D_MODEL ** -0.5),
        'exp_w_up': nrm((L, N_EXPERTS, D_MODEL, D_EXPERT), D_MODEL ** -0.5),
        'exp_w_down': nrm((L, N_EXPERTS, D_EXPERT, D_MODEL), D_EXPERT ** -0.5),
        'final_norm_g': 1.0 + nrm((D_MODEL,), 0.02),
    }


def reference(x, norm1_g, w_in, rwkv_mu, rwkv_w0, rwkv_w_up, rwkv_a0, rwkv_a_up, rwkv_g_up,
              rwkv_k_k, rwkv_k_a, rwkv_r_k, rwkv_ln_w, rwkv_ln_b, hgrn_conv_w, hgrn_lb_logits,
              hgrn_norm_g, w_out, norm2_g, router_g_w, router_g_b, router_e_w, router_e_b,
              exp_w_gate, exp_w_up, exp_w_down, final_norm_g):
    lb_all = jnp.cumsum(jax.nn.softmax(hgrn_lb_logits.astype(jnp.float32), axis=0), axis=0)
    h = x
    for l in range(DEPTH):
        n = rms_norm(h, norm1_g[l])
        p = n @ w_in[l]
        ya = rwkv7_mixer(p[..., :D_RWKV_IN], rwkv_mu[l], rwkv_w0[l], rwkv_w_up[l], rwkv_a0[l],
                         rwkv_a_up[l], rwkv_g_up[l], rwkv_k_k[l], rwkv_k_a[l], rwkv_r_k[l],
                         rwkv_ln_w[l], rwkv_ln_b[l])
        yb = hgrn2_mixer(p[..., D_RWKV_IN:], hgrn_conv_w[l], lb_all[l], hgrn_norm_g[l])
        mix = jnp.concatenate([ya, yb], axis=-1).astype(p.dtype)
        h = h + mix @ w_out[l]
        h = h + hier_moe(rms_norm(h, norm2_g[l]), router_g_w[l], router_g_b[l], router_e_w[l],
                         router_e_b[l], exp_w_gate[l], exp_w_up[l], exp_w_down[l])
    return rms_norm(h, final_norm_g)
```

```python
import functools

import numpy as np
import jax
import jax.numpy as jnp
from jax import lax
from jax.experimental import pallas as pl
from jax.experimental.pallas import tpu as pltpu

F32 = jnp.float32
BF16 = jnp.bfloat16

D_MODEL = 1024
D_RWKV = 512
D_HGRN = 512
RWKV_HEAD = 64
RANK_W = 64
RANK_A = 64
RANK_G = 128
HGRN_HEAD = 128
HGRN_HEADS = D_HGRN // HGRN_HEAD
CONV_W = 4
N_GROUPS = 4
EXPERTS_PER_GROUP = 8
N_EXPERTS = N_GROUPS * EXPERTS_PER_GROUP
D_EXPERT = 512
RMS_EPS = 1e-6
GN_EPS = 64e-5
L2_EPS = 1e-12
D_RWKV_IN = 3 * D_RWKV + RANK_W + RANK_A + RANK_G
D_HGRN_IN = 4 * D_HGRN

V7X_LANES = 128
V7X_SUBLANES = 8
V7X_MXU_DIM = 256
V7X_VMEM_LIMIT_BYTES = 56 * 1024 * 1024

CHUNK = 64
GROUP_LANES = V7X_MXU_DIM
HEADS_PER_GROUP = GROUP_LANES // RWKV_HEAD
N_RWKV_GROUPS = D_RWKV // GROUP_LANES
N_LEVELS = 6
TM_PROJ = 512
TM_MOE = 1024


def _dot(a, b):
    return jnp.dot(a, b, preferred_element_type=F32)


def _dot_nt(a, b):
    return lax.dot_general(a, b, (((1,), (1,)), ((), ())), preferred_element_type=F32)


def _dot_tn(a, b):
    return lax.dot_general(a, b, (((0,), (0,)), ((), ())), preferred_element_type=F32)


def _split_bf16(x):
    hi = x.astype(BF16)
    lo = (x - hi.astype(F32)).astype(BF16)
    return hi, lo


def _dot01_left(m01, x):
    hi, lo = _split_bf16(x)
    return _dot(m01, hi) + _dot(m01, lo)


def _dot01_right(x, m01):
    hi, lo = _split_bf16(x)
    return _dot(hi, m01) + _dot(lo, m01)


def _log1p_exp_neg_abs(x):
    return jnp.log1p(jnp.exp(-jnp.abs(x)))


def _log_sigmoid(x):
    return jnp.minimum(x, 0.0) - _log1p_exp_neg_abs(x)


def _sigmoid(x):
    return 1.0 / (1.0 + jnp.exp(-x))


def _rms(x, gain):
    return x * lax.rsqrt(jnp.mean(x * x, axis=-1, keepdims=True) + RMS_EPS) * gain


def _const_spec(shape):
    nd = len(shape)
    return pl.BlockSpec(shape, lambda *_: (0,) * nd)


def _inproj_kernel(x_ref, g_ref, w_ref, pa_ref, pb_ref):
    n = _rms(x_ref[...], g_ref[...]).astype(BF16)
    pa_ref[...] = _dot(n, w_ref[:, :D_RWKV_IN])
    pb_ref[...] = _dot(n, w_ref[:, D_RWKV_IN:])


def _inproj(x2d, gain, w_bf16):
    t = x2d.shape[0]
    return pl.pallas_call(
        _inproj_kernel,
        grid=(t // TM_PROJ,),
        in_specs=[
            pl.BlockSpec((TM_PROJ, D_MODEL), lambda i: (i, 0)),
            _const_spec((1, D_MODEL)),
            _const_spec((D_MODEL, D_RWKV_IN + D_HGRN_IN)),
        ],
        out_specs=[
            pl.BlockSpec((TM_PROJ, D_RWKV_IN), lambda i: (i, 0)),
            pl.BlockSpec((TM_PROJ, D_HGRN_IN), lambda i: (i, 0)),
        ],
        out_shape=[
            jax.ShapeDtypeStruct((t, D_RWKV_IN), F32),
            jax.ShapeDtypeStruct((t, D_HGRN_IN), F32),
        ],
        compiler_params=pltpu.CompilerParams(
            dimension_semantics=("parallel",), vmem_limit_bytes=V7X_VMEM_LIMIT_BYTES),
        name="inproj",
    )(x2d, gain, w_bf16)


_M_EYE, _M_BD, _M_STRICT, _M_INCL, _M_OFF0 = 0, 1, 2, 3, 4


def _rwkv_masks():
    i = np.arange(GROUP_LANES)[:, None]
    j = np.arange(GROUP_LANES)[None, :]
    bd = (i // CHUNK) == (j // CHUNK)
    masks = [i == j, bd, (j % CHUNK) < (i % CHUNK), (j % CHUNK) <= (i % CHUNK)]
    for lvl in range(N_LEVELS):
        s = 1 << lvl
        masks.append(((i // (2 * s)) == (j // (2 * s))) & ((i % (2 * s)) >= s) & ((j % (2 * s)) < s))
    return np.stack(masks).astype(np.float32)


def _tile_rows(x, n):
    return jnp.concatenate([x] * n, axis=0)


def _rwkv_kernel(p_ref, mu_ref, w0_ref, a0_ref, wwa_ref, gup_ref, kk_ref, ka_ref, rk_ref, lnw_ref,
                 lnb_ref, tri_ref, ones_ref, m_ref, o_ref, prev_ref, st_ref):
    c = pl.program_id(1)

    @pl.when(c == 0)
    def _():
        prev_ref[...] = jnp.zeros_like(prev_ref)
        st_ref[...] = jnp.zeros_like(st_ref)

    p = p_ref[...]
    row = lax.broadcasted_iota(jnp.int32, p.shape, 0)
    shifted = jnp.where(row == 0, prev_ref[V7X_SUBLANES - 1:V7X_SUBLANES, :], pltpu.roll(p, 1, 0))
    prev_ref[...] = p[CHUNK - V7X_SUBLANES:, :]
    p = p + mu_ref[...] * (shifted - p)

    r = p[:, 0:D_RWKV]
    k = p[:, D_RWKV:2 * D_RWKV]
    v = p[:, 2 * D_RWKV:3 * D_RWKV]
    x_wa = p[:, 3 * D_RWKV:3 * D_RWKV + RANK_W + RANK_A]
    dg = p[:, 3 * D_RWKV + RANK_W + RANK_A:]

    lane = lax.broadcasted_iota(jnp.int32, x_wa.shape, 1)
    t_wa = jnp.where(lane < RANK_W, jnp.tanh(x_wa), x_wa).astype(BF16)
    wa = _dot(t_wa, wwa_ref[...])
    w_lin = w0_ref[...] + wa[:, :D_RWKV]
    w = _log_sigmoid(w_lin) - 0.5
    log_decay = -jnp.exp(w)
    a = _sigmoid(a0_ref[...] + wa[:, D_RWKV:])
    g = _dot(_sigmoid(dg).astype(BF16), gup_ref[...])

    ones_bd = ones_ref[...]
    kk = k * kk_ref[...]
    kk = kk / jnp.maximum(jnp.sqrt(_dot01_right(kk * kk, ones_bd)), L2_EPS)
    k = k * (1.0 + (a - 1.0) * ka_ref[...])
    kb = kk * a

    cum = _dot01_left(tri_ref[...], log_decay)
    cum_last = cum[CHUNK - 1:CHUNK, :]
    e_in = jnp.exp(cum)
    e_ex = jnp.exp(cum - log_decay)
    e_inv = jnp.exp(-cum)
    e_rem = jnp.exp(cum_last - cum)
    w_chunk = jnp.exp(cum_last)

    r_hat = r * e_in
    kk_hat = kk * e_ex
    k_inv = k * e_inv
    b_inv = kb * e_inv
    k_rem = k * e_rem
    b_rem = kb * e_rem

    eye = m_ref[_M_EYE]
    bd = m_ref[_M_BD]
    strict = m_ref[_M_STRICT]
    incl = m_ref[_M_INCL]
    strict_lc = strict[:CHUNK]
    incl_lc = incl[:CHUNK]

    def block_diag(x):
        return (_tile_rows(x, HEADS_PER_GROUP) * bd).astype(BF16)

    ys = []
    for gi in range(N_RWKV_GROUPS):
        sl = slice(gi * GROUP_LANES, (gi + 1) * GROUP_LANES)
        s_bd = st_ref[gi]
        s_bf = s_bd.astype(BF16)
        lhs = jnp.concatenate([kk_hat[:, sl], r_hat[:, sl]], axis=0).astype(BF16)
        a_k = _dot_nt(lhs, block_diag(k_inv[:, sl]))
        a_b = _dot_nt(lhs, block_diag(b_inv[:, sl]))
        a_kk = (a_k[:CHUNK] * strict_lc).astype(BF16)
        a_rk = (a_k[CHUNK:] * incl_lc).astype(BF16)
        a_kb = a_b[:CHUNK] * strict_lc
        a_rb = (a_b[CHUNK:] * incl_lc).astype(BF16)

        a_kb_bd = _tile_rows(a_kb, HEADS_PER_GROUP) * bd
        inv = eye - a_kb_bd * m_ref[_M_OFF0]
        for lvl in range(1, N_LEVELS):
            off = (a_kb_bd * m_ref[_M_OFF0 + lvl]).astype(BF16)
            inv_bf = inv.astype(BF16)
            inv = inv - _dot(_dot(inv_bf, off).astype(BF16), inv_bf)
        inv_lc = inv[0:CHUNK]
        for h in range(1, HEADS_PER_GROUP):
            inv_lc = inv_lc + inv[h * CHUNK:(h + 1) * CHUNK]

        v_bd = block_diag(v[:, sl])
        from_state = _dot_nt(lhs, s_bf)
        rhs = from_state[:CHUNK] + _dot(a_kk, v_bd)
        u = _dot(inv_lc.astype(BF16), block_diag(rhs))
        y = from_state[CHUNK:] + _dot(a_rk, v_bd) - _dot(a_rb, block_diag(u))
        ys.append(y)

        vu = jnp.concatenate([v[:, sl], -u], axis=0).astype(BF16)
        kb_rem = jnp.concatenate([k_rem[:, sl], b_rem[:, sl]], axis=0).astype(BF16)
        st_ref[gi] = s_bd * w_chunk[:, sl] + _dot_tn(vu, kb_rem) * bd

    y = jnp.concatenate(ys, axis=1)
    inv_n = 1.0 / RWKV_HEAD
    mean = _dot01_right(y, ones_bd) * inv_n
    yc = y - mean
    var = _dot01_right(yc * yc, ones_bd) * inv_n
    yn = yc * lax.rsqrt(var + GN_EPS) * lnw_ref[...] + lnb_ref[...]
    bonus = _dot01_right(r * k * rk_ref[...], ones_bd)
    o_ref[...] = (yn + bonus * v) * g


def _rwkv(pa, mu, w0, a0, wwa, gup, k_k, k_a, r_k, ln_w, ln_b):
    b, s, _ = pa.shape
    tri = jnp.asarray(np.tril(np.ones((CHUNK, CHUNK), np.float32)), BF16)
    hh = np.arange(D_RWKV) // RWKV_HEAD
    ones_bd = jnp.asarray((hh[:, None] == hh[None, :]).astype(np.float32), BF16)
    masks = jnp.asarray(_rwkv_masks())
    vec = _const_spec((1, D_RWKV))
    return pl.pallas_call(
        _rwkv_kernel,
        grid=(b, s // CHUNK),
        in_specs=[
            pl.BlockSpec((None, CHUNK, D_RWKV_IN), lambda i, j: (i, j, 0)),
            _const_spec((1, D_RWKV_IN)),
            vec, vec,
            _const_spec((RANK_W + RANK_A, 2 * D_RWKV)),
            _const_spec((RANK_G, D_RWKV)),
            vec, vec, vec, vec, vec,
            _const_spec((CHUNK, CHUNK)),
            _const_spec((D_RWKV, D_RWKV)),
            _const_spec(masks.shape),
        ],
        out_specs=pl.BlockSpec((None, CHUNK, D_RWKV), lambda i, j: (i, j, 0)),
        out_shape=jax.ShapeDtypeStruct((b, s, D_RWKV), F32),
        scratch_shapes=[
            pltpu.VMEM((V7X_SUBLANES, D_RWKV_IN), F32),
            pltpu.VMEM((N_RWKV_GROUPS, GROUP_LANES, GROUP_LANES), F32),
        ],
        compiler_params=pltpu.CompilerParams(
            dimension_semantics=("parallel", "arbitrary"), vmem_limit_bytes=V7X_VMEM_LIMIT_BYTES),
        name="rwkv7",
    )(pa, mu, w0, a0, wwa, gup, k_k, k_a, r_k, ln_w, ln_b, tri, ones_bd, masks)


def _hgrn_sum_matrix():
    t = np.arange(CHUNK)[:, None]
    j = np.arange(CHUNK)[None, :]
    mats = [(j <= t), (j > t)]
    for lvl in range(N_LEVELS):
        s = 1 << lvl
        mid = (t // (2 * s)) * (2 * s) + s - 1
        mats.append(((j > mid) & (j <= t)).astype(np.float32) - ((j > t) & (j <= mid)).astype(np.float32))
    return np.concatenate([np.asarray(m, np.float32) for m in mats], axis=0)


def _hgrn_level_masks():
    t = np.arange(CHUNK)[:, None]
    s_ = np.arange(CHUNK)[None, :]
    masks = [t == s_]
    for lvl in range(N_LEVELS):
        s = 1 << lvl
        masks.append(((t // (2 * s)) == (s_ // (2 * s))) & ((t % (2 * s)) >= s) & ((s_ % (2 * s)) < s))
    return np.stack(masks).astype(np.float32)


def _hgrn_kernel(p_ref, cw_ref, lbl_ref, ng_ref, sm_ref, lm_ref, o_ref, prev_ref, st_ref):
    c = pl.program_id(1)

    @pl.when(c == 0)
    def _():
        prev_ref[...] = jnp.zeros_like(prev_ref)
        st_ref[...] = jnp.zeros_like(st_ref)

    x = p_ref[:, :3 * D_HGRN]
    gate = p_ref[:, 3 * D_HGRN:]
    xe = jnp.concatenate([prev_ref[...], x], axis=0)
    prev_ref[...] = x[CHUNK - V7X_SUBLANES:, :]
    conv = x * cw_ref[CONV_W - 1:CONV_W, :]
    for s in range(1, CONV_W):
        conv = conv + pltpu.roll(xe, s, 0)[V7X_SUBLANES:, :] * cw_ref[CONV_W - 1 - s:CONV_W - s, :]

    qc = conv[:, :D_HGRN]
    f = conv[:, D_HGRN:2 * D_HGRN]
    val = conv[:, 2 * D_HGRN:]
    q = qc * _sigmoid(qc)

    l0 = lbl_ref[0:1, :]
    l1 = lbl_ref[1:2, :]
    lmax = jnp.maximum(l0, l1)
    e0 = jnp.exp(l0 - lmax)
    lb = e0 / (e0 + jnp.exp(l1 - lmax))
    ta = jnp.log(lb)
    tb = jnp.log1p(-lb) + _log_sigmoid(f)
    log_f = jnp.maximum(ta, tb) + _log1p_exp_neg_abs(ta - tb)
    kg = (1.0 - lb) * _sigmoid(-f)

    sums = _dot01_left(sm_ref[...], log_f)
    b_in = sums[0:CHUNK]
    b_rem = sums[CHUNK:2 * CHUNK]
    q_inter = (q * jnp.exp(b_in)).astype(BF16)
    k_inter = (kg * jnp.exp(b_rem)).astype(BF16)
    w_chunk = jnp.exp(b_in[CHUNK - 1:CHUNK, :])

    q_lv = [q.astype(BF16)]
    k_lv = [kg.astype(BF16)]
    for lvl in range(N_LEVELS):
        e = jnp.exp(-jnp.abs(sums[(2 + lvl) * CHUNK:(3 + lvl) * CHUNK]))
        q_lv.append((q * e).astype(BF16))
        k_lv.append((kg * e).astype(BF16))

    val_bf = val.astype(BF16)
    outs = []
    for h in range(HGRN_HEADS):
        sl = slice(h * HGRN_HEAD, (h + 1) * HGRN_HEAD)
        scores = _dot_nt(q_lv[0][:, sl], k_lv[0][:, sl]) * lm_ref[0]
        for lvl in range(N_LEVELS):
            scores = scores + _dot_nt(q_lv[lvl + 1][:, sl], k_lv[lvl + 1][:, sl]) * lm_ref[lvl + 1]
        st = st_ref[h]
        o = _dot(scores.astype(BF16), val_bf[:, sl]) + _dot_nt(q_inter[:, sl], st.astype(BF16))
        st_ref[h] = st * w_chunk[:, sl] + _dot_tn(val_bf[:, sl], k_inter[:, sl])
        outs.append(o * lax.rsqrt(jnp.mean(o * o, axis=-1, keepdims=True) + RMS_EPS))
    o = jnp.concatenate(outs, axis=1)
    o_ref[...] = o * ng_ref[...] * (gate * _sigmoid(gate))


def _hgrn(pb, conv_w, lb_logits, norm_g):
    b, s, _ = pb.shape
    sum_m = jnp.asarray(_hgrn_sum_matrix(), BF16)
    lvl_m = jnp.asarray(_hgrn_level_masks())
    return pl.pallas_call(
        _hgrn_kernel,
        grid=(b, s // CHUNK),
        in_specs=[
            pl.BlockSpec((None, CHUNK, D_HGRN_IN), lambda i, j: (i, j, 0)),
            _const_spec((CONV_W, 3 * D_HGRN)),
            _const_spec(lb_logits.shape),
            _const_spec((1, D_HGRN)),
            _const_spec(sum_m.shape),
            _const_spec(lvl_m.shape),
        ],
        out_specs=pl.BlockSpec((None, CHUNK, D_HGRN), lambda i, j: (i, j, 0)),
        out_shape=jax.ShapeDtypeStruct((b, s, D_HGRN), F32),
        scratch_shapes=[
            pltpu.VMEM((V7X_SUBLANES, 3 * D_HGRN), F32),
            pltpu.VMEM((HGRN_HEADS, HGRN_HEAD, HGRN_HEAD), F32),
        ],
        compiler_params=pltpu.CompilerParams(
            dimension_semantics=("parallel", "arbitrary"), vmem_limit_bytes=V7X_VMEM_LIMIT_BYTES),
        name="hgrn2",
    )(pb, conv_w, lb_logits, norm_g, sum_m, lvl_m)


ROUTER_LANES = V7X_LANES


def _outproj_kernel(x_ref, ya_ref, yb_ref, wo_ref, g2_ref, wr_ref, br_ref, h_ref, n_ref, gates_ref):
    h = (x_ref[...]
         + _dot(ya_ref[...].astype(BF16), wo_ref[:D_RWKV, :])
         + _dot(yb_ref[...].astype(BF16), wo_ref[D_RWKV:, :]))
    h_ref[...] = h
    n = _rms(h, g2_ref[...])
    n_ref[...] = n.astype(BF16)

    logits = jnp.dot(n, wr_ref[...], preferred_element_type=F32,
                     precision=lax.Precision.HIGHEST) + br_ref[...]
    lane = lax.broadcasted_iota(jnp.int32, logits.shape, 1)
    neg = -jnp.inf
    gl = jnp.where((lane >= N_EXPERTS) & (lane < N_EXPERTS + N_GROUPS), logits, neg)
    gmax = jnp.max(gl, axis=-1, keepdims=True)
    gidx = jnp.min(jnp.where(gl == gmax, lane - N_EXPERTS, ROUTER_LANES), axis=-1, keepdims=True)
    g_w = 1.0 / jnp.sum(jnp.exp(gl - gmax), axis=-1, keepdims=True)
    el = jnp.where((lane < N_EXPERTS) & ((lane // EXPERTS_PER_GROUP) == gidx), logits, neg)
    m1 = jnp.max(el, axis=-1, keepdims=True)
    i1 = jnp.min(jnp.where(el == m1, lane, ROUTER_LANES), axis=-1, keepdims=True)
    el2 = jnp.where(lane == i1, neg, el)
    m2 = jnp.max(el2, axis=-1, keepdims=True)
    i2 = jnp.min(jnp.where(el2 == m2, lane, ROUTER_LANES), axis=-1, keepdims=True)
    t = jnp.exp(m2 - m1)
    w1 = 1.0 / (1.0 + t)
    gates_ref[...] = jnp.where(lane == i1, g_w * w1, jnp.where(lane == i2, g_w * (t * w1), 0.0))


def _outproj(x2d, ya, yb, wo_bf16, g2, wr, br):
    t = x2d.shape[0]
    row = lambda w: pl.BlockSpec((TM_PROJ, w), lambda i: (i, 0))
    return pl.pallas_call(
        _outproj_kernel,
        grid=(t // TM_PROJ,),
        in_specs=[
            row(D_MODEL), row(D_RWKV), row(D_HGRN),
            _const_spec((D_RWKV + D_HGRN, D_MODEL)),
            _const_spec((1, D_MODEL)),
            _const_spec((D_MODEL, ROUTER_LANES)),
            _const_spec((1, ROUTER_LANES)),
        ],
        out_specs=[row(D_MODEL), row(D_MODEL), row(ROUTER_LANES)],
        out_shape=[
            jax.ShapeDtypeStruct((t, D_MODEL), F32),
            jax.ShapeDtypeStruct((t, D_MODEL), BF16),
            jax.ShapeDtypeStruct((t, ROUTER_LANES), F32),
        ],
        compiler_params=pltpu.CompilerParams(
            dimension_semantics=("parallel",), vmem_limit_bytes=V7X_VMEM_LIMIT_BYTES),
        name="outproj_router",
    )(x2d, ya, yb, wo_bf16, g2, wr, br)


def _moe_kernel(n_ref, gates_ref, h_ref, wg_ref, wu_ref, wd_ref, gf_ref, o_ref, acc_ref):
    e = pl.program_id(1)

    @pl.when(e == 0)
    def _():
        acc_ref[...] = jnp.zeros_like(acc_ref)

    x = n_ref[...]
    gates = gates_ref[...]
    lane = lax.broadcasted_iota(jnp.int32, gates.shape, 1)
    gcol = jnp.sum(jnp.where(lane == e, gates, 0.0), axis=-1, keepdims=True)
    hg = _dot(x, wg_ref[...])
    hu = _dot(x, wu_ref[...])
    hid = (hg * _sigmoid(hg) * hu * gcol).astype(BF16)
    acc_ref[...] += _dot(hid, wd_ref[...])

    @pl.when(e == pl.num_programs(1) - 1)
    def _():
        o_ref[...] = _rms(h_ref[...] + acc_ref[...], gf_ref[...])


def _moe(n2, gates, h, wg, wu, wd, gf):
    t = n2.shape[0]
    return pl.pallas_call(
        _moe_kernel,
        grid=(t // TM_MOE, N_EXPERTS),
        in_specs=[
            pl.BlockSpec((TM_MOE, D_MODEL), lambda i, e: (i, 0)),
            pl.BlockSpec((TM_MOE, ROUTER_LANES), lambda i, e: (i, 0)),
            pl.BlockSpec((TM_MOE, D_MODEL), lambda i, e: (i, 0)),
            pl.BlockSpec((None, D_MODEL, D_EXPERT), lambda i, e: (e, 0, 0)),
            pl.BlockSpec((None, D_MODEL, D_EXPERT), lambda i, e: (e, 0, 0)),
            pl.BlockSpec((None, D_EXPERT, D_MODEL), lambda i, e: (e, 0, 0)),
            _const_spec((1, D_MODEL)),
        ],
        out_specs=pl.BlockSpec((TM_MOE, D_MODEL), lambda i, e: (i, 0)),
        out_shape=jax.ShapeDtypeStruct((t, D_MODEL), F32),
        scratch_shapes=[pltpu.VMEM((TM_MOE, D_MODEL), F32)],
        compiler_params=pltpu.CompilerParams(
            dimension_semantics=("parallel", "arbitrary"), vmem_limit_bytes=V7X_VMEM_LIMIT_BYTES),
        name="moe_dense",
    )(n2, gates, h, wg, wu, wd, gf)


def kernel(x, norm1_g, w_in, rwkv_mu, rwkv_w0, rwkv_w_up, rwkv_a0, rwkv_a_up, rwkv_g_up, rwkv_k_k,
           rwkv_k_a, rwkv_r_k, rwkv_ln_w, rwkv_ln_b, hgrn_conv_w, hgrn_lb_logits, hgrn_norm_g, w_out,
           norm2_g, router_g_w, router_g_b, router_e_w, router_e_b, exp_w_gate, exp_w_up, exp_w_down,
           final_norm_g):
    b, s, d = x.shape
    t = b * s
    l = 0
    x2d = x.reshape(t, d)
    row = lambda a: a.reshape(1, -1)

    pa, pb = _inproj(x2d, row(norm1_g[l]), w_in[l].astype(BF16))

    wwa = jnp.zeros((RANK_W + RANK_A, 2 * D_RWKV), F32)
    wwa = wwa.at[:RANK_W, :D_RWKV].set(rwkv_w_up[l]).at[RANK_W:, D_RWKV:].set(rwkv_a_up[l]).astype(BF16)
    ya = _rwkv(pa.reshape(b, s, D_RWKV_IN), row(rwkv_mu[l]), row(rwkv_w0[l]), row(rwkv_a0[l]), wwa,
               rwkv_g_up[l].astype(BF16), row(rwkv_k_k[l]), row(rwkv_k_a[l]), row(rwkv_r_k[l]),
               row(rwkv_ln_w[l]), row(rwkv_ln_b[l]))
    yb = _hgrn(pb.reshape(b, s, D_HGRN_IN), hgrn_conv_w[l], hgrn_lb_logits, row(hgrn_norm_g[l]))

    wr = jnp.zeros((D_MODEL, ROUTER_LANES), F32)
    wr = wr.at[:, :N_EXPERTS].set(router_e_w[l]).at[:, N_EXPERTS:N_EXPERTS + N_GROUPS].set(router_g_w[l])
    br = jnp.zeros((1, ROUTER_LANES), F32)
    br = br.at[0, :N_EXPERTS].set(router_e_b[l]).at[0, N_EXPERTS:N_EXPERTS + N_GROUPS].set(router_g_b[l])
    h, n2, gates = _outproj(x2d, ya.reshape(t, D_RWKV), yb.reshape(t, D_HGRN), w_out[l].astype(BF16),
                            row(norm2_g[l]), wr, br)

    out = _moe(n2, gates, h, exp_w_gate[l].astype(BF16), exp_w_up[l].astype(BF16),
               exp_w_down[l].astype(BF16), row(final_norm_g))
    return out.reshape(b, s, d)
```

```python
import functools

import numpy as np
import jax
import jax.numpy as jnp
from jax import lax
from jax.experimental import pallas as pl
from jax.experimental.pallas import tpu as pltpu

F32 = jnp.float32
BF16 = jnp.bfloat16

D_MODEL = 1024
D_RWKV = 512
D_HGRN = 512
RWKV_HEAD = 64
RANK_W = 64
RANK_A = 64
RANK_G = 128
HGRN_HEAD = 128
HGRN_HEADS = D_HGRN // HGRN_HEAD
CONV_W = 4
N_GROUPS = 4
EXPERTS_PER_GROUP = 8
N_EXPERTS = N_GROUPS * EXPERTS_PER_GROUP
D_EXPERT = 512
RMS_EPS = 1e-6
GN_EPS = 64e-5
L2_EPS = 1e-12
D_RWKV_IN = 3 * D_RWKV + RANK_W + RANK_A + RANK_G
D_HGRN_IN = 4 * D_HGRN

V7X_LANES = 128
V7X_SUBLANES = 8
V7X_MXU_DIM = 256
V7X_VMEM_LIMIT_BYTES = 56 * 1024 * 1024

CHUNK = 64
GROUP_LANES = V7X_MXU_DIM
HEADS_PER_GROUP = GROUP_LANES // RWKV_HEAD
N_RWKV_GROUPS = D_RWKV // GROUP_LANES
N_LEVELS = 6
TM_PROJ = 512
TM_DISPATCH = 512
TB_MOE = 256


def _dot(a, b):
    return jnp.dot(a, b, preferred_element_type=F32)


def _dot_nt(a, b):
    return lax.dot_general(a, b, (((1,), (1,)), ((), ())), preferred_element_type=F32)


def _dot_tn(a, b):
    return lax.dot_general(a, b, (((0,), (0,)), ((), ())), preferred_element_type=F32)


def _split_bf16(x):
    hi = x.astype(BF16)
    lo = (x - hi.astype(F32)).astype(BF16)
    return hi, lo


def _dot01_left(m01, x):
    hi, lo = _split_bf16(x)
    return _dot(m01, hi) + _dot(m01, lo)


def _dot01_right(x, m01):
    hi, lo = _split_bf16(x)
    return _dot(hi, m01) + _dot(lo, m01)


def _log1p_exp_neg_abs(x):
    return jnp.log1p(jnp.exp(-jnp.abs(x)))


def _log_sigmoid(x):
    return jnp.minimum(x, 0.0) - _log1p_exp_neg_abs(x)


def _sigmoid(x):
    return 1.0 / (1.0 + jnp.exp(-x))


def _rms(x, gain):
    return x * lax.rsqrt(jnp.mean(x * x, axis=-1, keepdims=True) + RMS_EPS) * gain


def _const_spec(shape):
    nd = len(shape)
    return pl.BlockSpec(shape, lambda *_: (0,) * nd)


def _inproj_kernel(x_ref, g_ref, w_ref, pa_ref, pb_ref):
    n = _rms(x_ref[...], g_ref[...]).astype(BF16)
    pa_ref[...] = _dot(n, w_ref[:, :D_RWKV_IN])
    pb_ref[...] = _dot(n, w_ref[:, D_RWKV_IN:])


def _inproj(x2d, gain, w_bf16):
    t = x2d.shape[0]
    return pl.pallas_call(
        _inproj_kernel,
        grid=(t // TM_PROJ,),
        in_specs=[
            pl.BlockSpec((TM_PROJ, D_MODEL), lambda i: (i, 0)),
            _const_spec((1, D_MODEL)),
            _const_spec((D_MODEL, D_RWKV_IN + D_HGRN_IN)),
        ],
        out_specs=[
            pl.BlockSpec((TM_PROJ, D_RWKV_IN), lambda i: (i, 0)),
            pl.BlockSpec((TM_PROJ, D_HGRN_IN), lambda i: (i, 0)),
        ],
        out_shape=[
            jax.ShapeDtypeStruct((t, D_RWKV_IN), F32),
            jax.ShapeDtypeStruct((t, D_HGRN_IN), F32),
        ],
        compiler_params=pltpu.CompilerParams(
            dimension_semantics=("parallel",), vmem_limit_bytes=V7X_VMEM_LIMIT_BYTES),
        name="inproj",
    )(x2d, gain, w_bf16)


_M_EYE, _M_BD, _M_STRICT, _M_INCL, _M_OFF0 = 0, 1, 2, 3, 4


def _rwkv_masks():
    i = np.arange(GROUP_LANES)[:, None]
    j = np.arange(GROUP_LANES)[None, :]
    bd = (i // CHUNK) == (j // CHUNK)
    masks = [i == j, bd, (j % CHUNK) < (i % CHUNK), (j % CHUNK) <= (i % CHUNK)]
    for lvl in range(N_LEVELS):
        s = 1 << lvl
        masks.append(((i // (2 * s)) == (j // (2 * s))) & ((i % (2 * s)) >= s) & ((j % (2 * s)) < s))
    return np.stack(masks).astype(np.float32)


def _tile_rows(x, n):
    return jnp.concatenate([x] * n, axis=0)


def _rwkv_kernel(p_ref, mu_ref, w0_ref, a0_ref, wwa_ref, gup_ref, kk_ref, ka_ref, rk_ref, lnw_ref,
                 lnb_ref, tri_ref, ones_ref, m_ref, o_ref, prev_ref, st_ref):
    c = pl.program_id(1)

    @pl.when(c == 0)
    def _():
        prev_ref[...] = jnp.zeros_like(prev_ref)
        st_ref[...] = jnp.zeros_like(st_ref)

    p = p_ref[...]
    row = lax.broadcasted_iota(jnp.int32, p.shape, 0)
    shifted = jnp.where(row == 0, prev_ref[V7X_SUBLANES - 1:V7X_SUBLANES, :], pltpu.roll(p, 1, 0))
    prev_ref[...] = p[CHUNK - V7X_SUBLANES:, :]
    p = p + mu_ref[...] * (shifted - p)

    r = p[:, 0:D_RWKV]
    k = p[:, D_RWKV:2 * D_RWKV]
    v = p[:, 2 * D_RWKV:3 * D_RWKV]
    x_wa = p[:, 3 * D_RWKV:3 * D_RWKV + RANK_W + RANK_A]
    dg = p[:, 3 * D_RWKV + RANK_W + RANK_A:]

    lane = lax.broadcasted_iota(jnp.int32, x_wa.shape, 1)
    t_wa = jnp.where(lane < RANK_W, jnp.tanh(x_wa), x_wa).astype(BF16)
    wa = _dot(t_wa, wwa_ref[...])
    w_lin = w0_ref[...] + wa[:, :D_RWKV]
    w = _log_sigmoid(w_lin) - 0.5
    log_decay = -jnp.exp(w)
    a = _sigmoid(a0_ref[...] + wa[:, D_RWKV:])
    g = _dot(_sigmoid(dg).astype(BF16), gup_ref[...])

    ones_bd = ones_ref[...]
    kk = k * kk_ref[...]
    kk = kk / jnp.maximum(jnp.sqrt(_dot01_right(kk * kk, ones_bd)), L2_EPS)
    k = k * (1.0 + (a - 1.0) * ka_ref[...])
    kb = kk * a

    cum = _dot01_left(tri_ref[...], log_decay)
    cum_last = cum[CHUNK - 1:CHUNK, :]
    e_in = jnp.exp(cum)
    e_ex = jnp.exp(cum - log_decay)
    e_inv = jnp.exp(-cum)
    e_rem = jnp.exp(cum_last - cum)
    w_chunk = jnp.exp(cum_last)

    r_hat = r * e_in
    kk_hat = kk * e_ex
    k_inv = k * e_inv
    b_inv = kb * e_inv
    k_rem = k * e_rem
    b_rem = kb * e_rem

    eye = m_ref[_M_EYE]
    bd = m_ref[_M_BD]
    strict = m_ref[_M_STRICT]
    incl = m_ref[_M_INCL]
    strict_lc = strict[:CHUNK]
    incl_lc = incl[:CHUNK]

    def block_diag(x):
        return (_tile_rows(x, HEADS_PER_GROUP) * bd).astype(BF16)

    ys = []
    for gi in range(N_RWKV_GROUPS):
        sl = slice(gi * GROUP_LANES, (gi + 1) * GROUP_LANES)
        s_bd = st_ref[gi]
        s_bf = s_bd.astype(BF16)
        lhs = jnp.concatenate([kk_hat[:, sl], r_hat[:, sl]], axis=0).astype(BF16)
        a_k = _dot_nt(lhs, block_diag(k_inv[:, sl]))
        a_b = _dot_nt(lhs, block_diag(b_inv[:, sl]))
        a_kk = (a_k[:CHUNK] * strict_lc).astype(BF16)
        a_rk = (a_k[CHUNK:] * incl_lc).astype(BF16)
        a_kb = a_b[:CHUNK] * strict_lc
        a_rb = (a_b[CHUNK:] * incl_lc).astype(BF16)

        a_kb_bd = _tile_rows(a_kb, HEADS_PER_GROUP) * bd
        inv = eye - a_kb_bd * m_ref[_M_OFF0]
        for lvl in range(1, N_LEVELS):
            off = (a_kb_bd * m_ref[_M_OFF0 + lvl]).astype(BF16)
            inv_bf = inv.astype(BF16)
            inv = inv - _dot(_dot(inv_bf, off).astype(BF16), inv_bf)
        inv_lc = inv[0:CHUNK]
        for h in range(1, HEADS_PER_GROUP):
            inv_lc = inv_lc + inv[h * CHUNK:(h + 1) * CHUNK]

        v_bd = block_diag(v[:, sl])
        from_state = _dot_nt(lhs, s_bf)
        rhs = from_state[:CHUNK] + _dot(a_kk, v_bd)
        u = _dot(inv_lc.astype(BF16), block_diag(rhs))
        y = from_state[CHUNK:] + _dot(a_rk, v_bd) - _dot(a_rb, block_diag(u))
        ys.append(y)

        vu = jnp.concatenate([v[:, sl], -u], axis=0).astype(BF16)
        kb_rem = jnp.concatenate([k_rem[:, sl], b_rem[:, sl]], axis=0).astype(BF16)
        st_ref[gi] = s_bd * w_chunk[:, sl] + _dot_tn(vu, kb_rem) * bd

    y = jnp.concatenate(ys, axis=1)
    inv_n = 1.0 / RWKV_HEAD
    mean = _dot01_right(y, ones_bd) * inv_n
    yc = y - mean
    var = _dot01_right(yc * yc, ones_bd) * inv_n
    yn = yc * lax.rsqrt(var + GN_EPS) * lnw_ref[...] + lnb_ref[...]
    bonus = _dot01_right(r * k * rk_ref[...], ones_bd)
    o_ref[...] = (yn + bonus * v) * g


def _rwkv(pa, mu, w0, a0, wwa, gup, k_k, k_a, r_k, ln_w, ln_b):
    b, s, _ = pa.shape
    tri = jnp.asarray(np.tril(np.ones((CHUNK, CHUNK), np.float32)), BF16)
    hh = np.arange(D_RWKV) // RWKV_HEAD
    ones_bd = jnp.asarray((hh[:, None] == hh[None, :]).astype(np.float32), BF16)
    masks = jnp.asarray(_rwkv_masks())
    vec = _const_spec((1, D_RWKV))
    return pl.pallas_call(
        _rwkv_kernel,
        grid=(b, s // CHUNK),
        in_specs=[
            pl.BlockSpec((None, CHUNK, D_RWKV_IN), lambda i, j: (i, j, 0)),
            _const_spec((1, D_RWKV_IN)),
            vec, vec,
            _const_spec((RANK_W + RANK_A, 2 * D_RWKV)),
            _const_spec((RANK_G, D_RWKV)),
            vec, vec, vec, vec, vec,
            _const_spec((CHUNK, CHUNK)),
            _const_spec((D_RWKV, D_RWKV)),
            _const_spec(masks.shape),
        ],
        out_specs=pl.BlockSpec((None, CHUNK, D_RWKV), lambda i, j: (i, j, 0)),
        out_shape=jax.ShapeDtypeStruct((b, s, D_RWKV), F32),
        scratch_shapes=[
            pltpu.VMEM((V7X_SUBLANES, D_RWKV_IN), F32),
            pltpu.VMEM((N_RWKV_GROUPS, GROUP_LANES, GROUP_LANES), F32),
        ],
        compiler_params=pltpu.CompilerParams(
            dimension_semantics=("parallel", "arbitrary"), vmem_limit_bytes=V7X_VMEM_LIMIT_BYTES),
        name="rwkv7",
    )(pa, mu, w0, a0, wwa, gup, k_k, k_a, r_k, ln_w, ln_b, tri, ones_bd, masks)


def _hgrn_sum_matrix():
    t = np.arange(CHUNK)[:, None]
    j = np.arange(CHUNK)[None, :]
    mats = [(j <= t), (j > t)]
    for lvl in range(N_LEVELS):
        s = 1 << lvl
        mid = (t // (2 * s)) * (2 * s) + s - 1
        mats.append(((j > mid) & (j <= t)).astype(np.float32) - ((j > t) & (j <= mid)).astype(np.float32))
    return np.concatenate([np.asarray(m, np.float32) for m in mats], axis=0)


def _hgrn_level_masks():
    t = np.arange(CHUNK)[:, None]
    s_ = np.arange(CHUNK)[None, :]
    masks = [t == s_]
    for lvl in range(N_LEVELS):
        s = 1 << lvl
        masks.append(((t // (2 * s)) == (s_ // (2 * s))) & ((t % (2 * s)) >= s) & ((s_ % (2 * s)) < s))
    return np.stack(masks).astype(np.float32)


def _hgrn_kernel(p_ref, cw_ref, lbl_ref, ng_ref, sm_ref, lm_ref, o_ref, prev_ref, st_ref):
    c = pl.program_id(1)

    @pl.when(c == 0)
    def _():
        prev_ref[...] = jnp.zeros_like(prev_ref)
        st_ref[...] = jnp.zeros_like(st_ref)

    x = p_ref[:, :3 * D_HGRN]
    gate = p_ref[:, 3 * D_HGRN:]
    xe = jnp.concatenate([prev_ref[...], x], axis=0)
    prev_ref[...] = x[CHUNK - V7X_SUBLANES:, :]
    conv = x * cw_ref[CONV_W - 1:CONV_W, :]
    for s in range(1, CONV_W):
        conv = conv + pltpu.roll(xe, s, 0)[V7X_SUBLANES:, :] * cw_ref[CONV_W - 1 - s:CONV_W - s, :]

    qc = conv[:, :D_HGRN]
    f = conv[:, D_HGRN:2 * D_HGRN]
    val = conv[:, 2 * D_HGRN:]
    q = qc * _sigmoid(qc)

    l0 = lbl_ref[0:1, :]
    l1 = lbl_ref[1:2, :]
    lmax = jnp.maximum(l0, l1)
    e0 = jnp.exp(l0 - lmax)
    lb = e0 / (e0 + jnp.exp(l1 - lmax))
    ta = jnp.log(lb)
    tb = jnp.log1p(-lb) + _log_sigmoid(f)
    log_f = jnp.maximum(ta, tb) + _log1p_exp_neg_abs(ta - tb)
    kg = (1.0 - lb) * _sigmoid(-f)

    sums = _dot01_left(sm_ref[...], log_f)
    b_in = sums[0:CHUNK]
    b_rem = sums[CHUNK:2 * CHUNK]
    q_inter = (q * jnp.exp(b_in)).astype(BF16)
    k_inter = (kg * jnp.exp(b_rem)).astype(BF16)
    w_chunk = jnp.exp(b_in[CHUNK - 1:CHUNK, :])

    q_lv = [q.astype(BF16)]
    k_lv = [kg.astype(BF16)]
    for lvl in range(N_LEVELS):
        e = jnp.exp(-jnp.abs(sums[(2 + lvl) * CHUNK:(3 + lvl) * CHUNK]))
        q_lv.append((q * e).astype(BF16))
        k_lv.append((kg * e).astype(BF16))

    val_bf = val.astype(BF16)
    outs = []
    for h in range(HGRN_HEADS):
        sl = slice(h * HGRN_HEAD, (h + 1) * HGRN_HEAD)
        scores = _dot_nt(q_lv[0][:, sl], k_lv[0][:, sl]) * lm_ref[0]
        for lvl in range(N_LEVELS):
            scores = scores + _dot_nt(q_lv[lvl + 1][:, sl], k_lv[lvl + 1][:, sl]) * lm_ref[lvl + 1]
        st = st_ref[h]
        o = _dot(scores.astype(BF16), val_bf[:, sl]) + _dot_nt(q_inter[:, sl], st.astype(BF16))
        st_ref[h] = st * w_chunk[:, sl] + _dot_tn(val_bf[:, sl], k_inter[:, sl])
        outs.append(o * lax.rsqrt(jnp.mean(o * o, axis=-1, keepdims=True) + RMS_EPS))
    o = jnp.concatenate(outs, axis=1)
    o_ref[...] = o * ng_ref[...] * (gate * _sigmoid(gate))


def _hgrn(pb, conv_w, lb_logits, norm_g):
    b, s, _ = pb.shape
    sum_m = jnp.asarray(_hgrn_sum_matrix(), BF16)
    lvl_m = jnp.asarray(_hgrn_level_masks())
    return pl.pallas_call(
        _hgrn_kernel,
        grid=(b, s // CHUNK),
        in_specs=[
            pl.BlockSpec((None, CHUNK, D_HGRN_IN), lambda i, j: (i, j, 0)),
            _const_spec((CONV_W, 3 * D_HGRN)),
            _const_spec(lb_logits.shape),
            _const_spec((1, D_HGRN)),
            _const_spec(sum_m.shape),
            _const_spec(lvl_m.shape),
        ],
        out_specs=pl.BlockSpec((None, CHUNK, D_HGRN), lambda i, j: (i, j, 0)),
        out_shape=jax.ShapeDtypeStruct((b, s, D_HGRN), F32),
        scratch_shapes=[
            pltpu.VMEM((V7X_SUBLANES, 3 * D_HGRN), F32),
            pltpu.VMEM((HGRN_HEADS, HGRN_HEAD, HGRN_HEAD), F32),
        ],
        compiler_params=pltpu.CompilerParams(
            dimension_semantics=("parallel", "arbitrary"), vmem_limit_bytes=V7X_VMEM_LIMIT_BYTES),
        name="hgrn2",
    )(pb, conv_w, lb_logits, norm_g, sum_m, lvl_m)


ROUTER_LANES = V7X_LANES


_R_E1, _R_E2, _R_G1, _R_G2, _R_RANK1, _R_RANK2 = 0, 1, 2, 3, 4, 5


def _outproj_kernel(x_ref, ya_ref, yb_ref, wo_ref, g2_ref, wr_ref, br_ref, tri_ref,
                    h_ref, n_ref, meta_ref, counts_ref, carry_ref):
    @pl.when(pl.program_id(0) == 0)
    def _():
        carry_ref[...] = jnp.zeros_like(carry_ref)

    h = (x_ref[...]
         + _dot(ya_ref[...].astype(BF16), wo_ref[:D_RWKV, :])
         + _dot(yb_ref[...].astype(BF16), wo_ref[D_RWKV:, :]))
    h_ref[...] = h
    n = _rms(h, g2_ref[...])
    n_ref[...] = n

    logits = jnp.dot(n, wr_ref[...], preferred_element_type=F32,
                     precision=lax.Precision.HIGHEST) + br_ref[...]
    lane = lax.broadcasted_iota(jnp.int32, logits.shape, 1).astype(F32)
    neg = -jnp.inf
    gl = jnp.where((lane >= N_EXPERTS) & (lane < N_EXPERTS + N_GROUPS), logits, neg)
    gmax = jnp.max(gl, axis=-1, keepdims=True)
    gidx = jnp.min(jnp.where(gl == gmax, lane - N_EXPERTS, ROUTER_LANES), axis=-1, keepdims=True)
    g_w = 1.0 / jnp.sum(jnp.exp(gl - gmax), axis=-1, keepdims=True)
    lo = gidx * EXPERTS_PER_GROUP
    el = jnp.where((lane >= lo) & (lane < lo + EXPERTS_PER_GROUP), logits, neg)
    m1 = jnp.max(el, axis=-1, keepdims=True)
    i1 = jnp.min(jnp.where(el == m1, lane, ROUTER_LANES), axis=-1, keepdims=True)
    el2 = jnp.where(lane == i1, neg, el)
    m2 = jnp.max(el2, axis=-1, keepdims=True)
    i2 = jnp.min(jnp.where(el2 == m2, lane, ROUTER_LANES), axis=-1, keepdims=True)
    t = jnp.exp(m2 - m1)
    w1 = 1.0 / (1.0 + t)

    onehot1 = lane == i1
    onehot2 = lane == i2
    assigned = jnp.where(onehot1 | onehot2, 1.0, 0.0)
    before = _dot(tri_ref[...], assigned.astype(BF16)) + carry_ref[...]
    rank1 = jnp.sum(jnp.where(onehot1, before, 0.0), axis=-1, keepdims=True)
    rank2 = jnp.sum(jnp.where(onehot2, before, 0.0), axis=-1, keepdims=True)
    carry_ref[...] += jnp.sum(assigned, axis=0, keepdims=True)
    counts_ref[...] = jnp.broadcast_to(carry_ref[...], counts_ref.shape)

    meta = jnp.zeros_like(logits)
    for idx, val in ((_R_E1, i1), (_R_E2, i2), (_R_G1, g_w * w1), (_R_G2, g_w * (t * w1)),
                     (_R_RANK1, rank1), (_R_RANK2, rank2)):
        meta = jnp.where(lane == idx, val, meta)
    meta_ref[...] = meta


def _outproj(x2d, ya, yb, wo_bf16, g2, wr, br):
    t = x2d.shape[0]
    row = lambda w: pl.BlockSpec((TM_PROJ, w), lambda i: (i, 0))
    tri = jnp.asarray(np.tril(np.ones((TM_PROJ, TM_PROJ), np.float32), -1), BF16)
    return pl.pallas_call(
        _outproj_kernel,
        grid=(t // TM_PROJ,),
        in_specs=[
            row(D_MODEL), row(D_RWKV), row(D_HGRN),
            _const_spec((D_RWKV + D_HGRN, D_MODEL)),
            _const_spec((1, D_MODEL)),
            _const_spec((D_MODEL, ROUTER_LANES)),
            _const_spec((1, ROUTER_LANES)),
            _const_spec((TM_PROJ, TM_PROJ)),
        ],
        out_specs=[row(D_MODEL), row(D_MODEL), row(ROUTER_LANES), _const_spec((V7X_SUBLANES, ROUTER_LANES))],
        out_shape=[
            jax.ShapeDtypeStruct((t, D_MODEL), F32),
            jax.ShapeDtypeStruct((t, D_MODEL), F32),
            jax.ShapeDtypeStruct((t, ROUTER_LANES), F32),
            jax.ShapeDtypeStruct((V7X_SUBLANES, ROUTER_LANES), F32),
        ],
        scratch_shapes=[pltpu.VMEM((1, ROUTER_LANES), F32)],
        compiler_params=pltpu.CompilerParams(
            dimension_semantics=("arbitrary",), vmem_limit_bytes=V7X_VMEM_LIMIT_BYTES),
        name="outproj_router",
    )(x2d, ya, yb, wo_bf16, g2, wr, br, tri)


def _slot_kernel(meta_ref, offs_ref, pos_ref):
    meta = meta_ref[...]
    lane = lax.broadcasted_iota(jnp.int32, meta.shape, 1).astype(F32)
    offs = offs_ref[...]

    def slot(e_lane, rank_lane):
        e = meta[:, e_lane:e_lane + 1]
        return jnp.sum(jnp.where(lane == e, offs, 0.0), axis=-1, keepdims=True) + meta[:, rank_lane:rank_lane + 1]

    pos = jnp.where(lane == 0, slot(_R_E1, _R_RANK1), jnp.where(lane == 1, slot(_R_E2, _R_RANK2), 0.0))
    pos_ref[...] = pos.astype(jnp.int32)


def _slots(meta, offs):
    t = meta.shape[0]
    return pl.pallas_call(
        _slot_kernel,
        grid=(t // TM_PROJ,),
        in_specs=[pl.BlockSpec((TM_PROJ, ROUTER_LANES), lambda i: (i, 0)), _const_spec((1, ROUTER_LANES))],
        out_specs=pl.BlockSpec((TM_PROJ, ROUTER_LANES), lambda i: (i, 0)),
        out_shape=jax.ShapeDtypeStruct((t, ROUTER_LANES), jnp.int32),
        compiler_params=pltpu.CompilerParams(dimension_semantics=("parallel",)),
        name="moe_slots",
    )(meta, offs)


def _row_copy(src_ref, src_row, dst_ref, dst_row, sem):
    return pltpu.make_async_copy(src_ref.at[pl.ds(src_row, 1)], dst_ref.at[pl.ds(dst_row, 1)], sem)


def _dispatch_kernel(pos_ref, n_ref, xs_in_ref, xs_ref, sem):
    del xs_in_ref

    def body(r, carry):
        _row_copy(n_ref, r, xs_ref, pos_ref[2 * r], sem).start()
        _row_copy(n_ref, r, xs_ref, pos_ref[2 * r + 1], sem).start()
        return carry

    lax.fori_loop(0, TM_DISPATCH, body, 0, unroll=8)
    for _ in range(2):
        pltpu.make_async_copy(n_ref, xs_ref.at[pl.ds(0, TM_DISPATCH)], sem).wait()


def _dispatch(pos_flat, n2, n_slots):
    t = n2.shape[0]
    xs0 = jnp.zeros((n_slots, D_MODEL), F32)
    return pl.pallas_call(
        _dispatch_kernel,
        grid=(t // TM_DISPATCH,),
        in_specs=[
            pl.BlockSpec((2 * TM_DISPATCH,), lambda i: (i,), memory_space=pltpu.SMEM),
            pl.BlockSpec((TM_DISPATCH, D_MODEL), lambda i: (i, 0)),
            pl.BlockSpec(memory_space=pl.ANY),
        ],
        out_specs=pl.BlockSpec(memory_space=pl.ANY),
        out_shape=jax.ShapeDtypeStruct((n_slots, D_MODEL), F32),
        scratch_shapes=[pltpu.SemaphoreType.DMA(())],
        input_output_aliases={2: 0},
        compiler_params=pltpu.CompilerParams(
            dimension_semantics=("arbitrary",), has_side_effects=True),
        name="moe_dispatch",
    )(pos_flat, n2, xs0)


def _gmm_kernel(be_ref, nu_ref, xs_ref, wg_ref, wu_ref, wd_ref, y_ref, wgu_s, wd_s):
    b = pl.program_id(0)

    @pl.when(b < nu_ref[0])
    def _():
        @pl.when((b == 0) | (be_ref[b] != be_ref[jnp.maximum(b - 1, 0)]))
        def _():
            wgu_s[:, :D_EXPERT] = wg_ref[...].astype(BF16)
            wgu_s[:, D_EXPERT:] = wu_ref[...].astype(BF16)
            wd_s[...] = wd_ref[...].astype(BF16)

        gu = _dot(xs_ref[...].astype(BF16), wgu_s[...])
        hg = gu[:, :D_EXPERT]
        hid = (hg * _sigmoid(hg) * gu[:, D_EXPERT:]).astype(BF16)
        y_ref[...] = _dot(hid, wd_s[...])

    @pl.when(b >= nu_ref[0])
    def _():
        y_ref[...] = jnp.zeros_like(y_ref)


def _gmm(block_expert, n_used, xs, wg, wu, wd):
    n_slots = xs.shape[0]
    blk = lambda b, be, nu: (jnp.minimum(b, nu[0] - 1), 0)
    wsel = lambda b, be, nu: (be[jnp.minimum(b, nu[0] - 1)], 0, 0)
    return pl.pallas_call(
        _gmm_kernel,
        grid_spec=pltpu.PrefetchScalarGridSpec(
            num_scalar_prefetch=2,
            grid=(n_slots // TB_MOE,),
            in_specs=[
                pl.BlockSpec((TB_MOE, D_MODEL), blk),
                pl.BlockSpec((None, D_MODEL, D_EXPERT), wsel),
                pl.BlockSpec((None, D_MODEL, D_EXPERT), wsel),
                pl.BlockSpec((None, D_EXPERT, D_MODEL), wsel),
            ],
            out_specs=pl.BlockSpec((TB_MOE, D_MODEL), lambda b, be, nu: (b, 0)),
            scratch_shapes=[
                pltpu.VMEM((D_MODEL, 2 * D_EXPERT), BF16),
                pltpu.VMEM((D_EXPERT, D_MODEL), BF16),
            ],
        ),
        out_shape=jax.ShapeDtypeStruct((n_slots, D_MODEL), F32),
        compiler_params=pltpu.CompilerParams(
            dimension_semantics=("arbitrary",), vmem_limit_bytes=V7X_VMEM_LIMIT_BYTES),
        name="moe_gmm",
    )(block_expert, n_used, xs, wg, wu, wd)


def _combine_kernel(pos_ref, pos_next_ref, meta_ref, h_ref, gf_ref, ys_ref, o_ref, buf_ref, sem):
    i = pl.program_id(0)
    slot = lax.rem(i, 2)

    def issue(p_ref, s):
        def body(r, carry):
            _row_copy(ys_ref, p_ref[2 * r], buf_ref.at[s, 0], r, sem.at[s]).start()
            _row_copy(ys_ref, p_ref[2 * r + 1], buf_ref.at[s, 1], r, sem.at[s]).start()
            return carry
        lax.fori_loop(0, TM_DISPATCH, body, 0, unroll=8)

    @pl.when(i == 0)
    def _():
        issue(pos_ref, 0)

    @pl.when(i + 1 < pl.num_programs(0))
    def _():
        issue(pos_next_ref, 1 - slot)

    for j in range(2):
        pltpu.make_async_copy(ys_ref.at[pl.ds(0, TM_DISPATCH)], buf_ref.at[slot, j], sem.at[slot]).wait()

    meta = meta_ref[...]
    g1 = meta[:, _R_G1:_R_G1 + 1]
    g2 = meta[:, _R_G2:_R_G2 + 1]
    moe = g1 * buf_ref[slot, 0] + g2 * buf_ref[slot, 1]
    o_ref[...] = _rms(h_ref[...] + moe, gf_ref[...])


def _combine(pos_flat, meta, h, gf, ys):
    t = h.shape[0]
    n_tiles = t // TM_DISPATCH
    return pl.pallas_call(
        _combine_kernel,
        grid=(n_tiles,),
        in_specs=[
            pl.BlockSpec((2 * TM_DISPATCH,), lambda i: (i,), memory_space=pltpu.SMEM),
            pl.BlockSpec((2 * TM_DISPATCH,), lambda i: (jnp.minimum(i + 1, n_tiles - 1),),
                         memory_space=pltpu.SMEM),
            pl.BlockSpec((TM_DISPATCH, ROUTER_LANES), lambda i: (i, 0)),
            pl.BlockSpec((TM_DISPATCH, D_MODEL), lambda i: (i, 0)),
            _const_spec((1, D_MODEL)),
            pl.BlockSpec(memory_space=pl.ANY),
        ],
        out_specs=pl.BlockSpec((TM_DISPATCH, D_MODEL), lambda i: (i, 0)),
        out_shape=jax.ShapeDtypeStruct((t, D_MODEL), F32),
        scratch_shapes=[
            pltpu.VMEM((2, 2, TM_DISPATCH, D_MODEL), F32),
            pltpu.SemaphoreType.DMA((2,)),
        ],
        compiler_params=pltpu.CompilerParams(
            dimension_semantics=("arbitrary",), vmem_limit_bytes=V7X_VMEM_LIMIT_BYTES),
        name="moe_combine",
    )(pos_flat, pos_flat, meta, h, gf, ys)


def _moe(n2, meta, counts, h, wg, wu, wd, gf):
    t = n2.shape[0]
    n_blocks = (2 * t + N_EXPERTS * (TB_MOE - 1) + TB_MOE - 1) // TB_MOE
    cnt = counts[0, :N_EXPERTS].astype(jnp.int32)
    blocks = (cnt + TB_MOE - 1) // TB_MOE
    ends = jnp.cumsum(blocks)
    offs = ((ends - blocks) * TB_MOE).astype(F32)
    offs_row = jnp.zeros((1, ROUTER_LANES), F32).at[0, :N_EXPERTS].set(offs)
    block_ids = jnp.arange(n_blocks, dtype=jnp.int32)
    block_expert = jnp.minimum(
        jnp.sum((ends[None, :] <= block_ids[:, None]).astype(jnp.int32), axis=1), N_EXPERTS - 1)
    n_used = ends[-1:].astype(jnp.int32)

    pos = _slots(meta, offs_row)
    pos_flat = pos[:, :2].reshape(-1)
    xs = _dispatch(pos_flat, n2, n_blocks * TB_MOE)
    ys = _gmm(block_expert, n_used, xs, wg, wu, wd)
    return _combine(pos_flat, meta, h, gf, ys)


def kernel(x, norm1_g, w_in, rwkv_mu, rwkv_w0, rwkv_w_up, rwkv_a0, rwkv_a_up, rwkv_g_up, rwkv_k_k,
           rwkv_k_a, rwkv_r_k, rwkv_ln_w, rwkv_ln_b, hgrn_conv_w, hgrn_lb_logits, hgrn_norm_g, w_out,
           norm2_g, router_g_w, router_g_b, router_e_w, router_e_b, exp_w_gate, exp_w_up, exp_w_down,
           final_norm_g):
    b, s, d = x.shape
    t = b * s
    l = 0
    x2d = x.reshape(t, d)
    row = lambda a: a.reshape(1, -1)

    pa, pb = _inproj(x2d, row(norm1_g[l]), w_in[l].astype(BF16))

    wwa = jnp.zeros((RANK_W + RANK_A, 2 * D_RWKV), F32)
    wwa = wwa.at[:RANK_W, :D_RWKV].set(rwkv_w_up[l]).at[RANK_W:, D_RWKV:].set(rwkv_a_up[l]).astype(BF16)
    ya = _rwkv(pa.reshape(b, s, D_RWKV_IN), row(rwkv_mu[l]), row(rwkv_w0[l]), row(rwkv_a0[l]), wwa,
               rwkv_g_up[l].astype(BF16), row(rwkv_k_k[l]), row(rwkv_k_a[l]), row(rwkv_r_k[l]),
               row(rwkv_ln_w[l]), row(rwkv_ln_b[l]))
    yb = _hgrn(pb.reshape(b, s, D_HGRN_IN), hgrn_conv_w[l], hgrn_lb_logits, row(hgrn_norm_g[l]))

    wr = jnp.zeros((D_MODEL, ROUTER_LANES), F32)
    wr = wr.at[:, :N_EXPERTS].set(router_e_w[l]).at[:, N_EXPERTS:N_EXPERTS + N_GROUPS].set(router_g_w[l])
    br = jnp.zeros((1, ROUTER_LANES), F32)
    br = br.at[0, :N_EXPERTS].set(router_e_b[l]).at[0, N_EXPERTS:N_EXPERTS + N_GROUPS].set(router_g_b[l])
    h, n2, meta, counts = _outproj(x2d, ya.reshape(t, D_RWKV), yb.reshape(t, D_HGRN),
                                   w_out[l].astype(BF16), row(norm2_g[l]), wr, br)

    out = _moe(n2, meta, counts, h, exp_w_gate[l], exp_w_up[l], exp_w_down[l], row(final_norm_g))
    return out.reshape(b, s, d)
```

```python
import functools

import numpy as np
import jax
import jax.numpy as jnp
from jax import lax
from jax.experimental import pallas as pl
from jax.experimental.pallas import tpu as pltpu

F32 = jnp.float32
BF16 = jnp.bfloat16

D_MODEL = 1024
D_RWKV = 512
D_HGRN = 512
RWKV_HEAD = 64
RANK_W = 64
RANK_A = 64
RANK_G = 128
HGRN_HEAD = 128
HGRN_HEADS = D_HGRN // HGRN_HEAD
CONV_W = 4
N_GROUPS = 4
EXPERTS_PER_GROUP = 8
N_EXPERTS = N_GROUPS * EXPERTS_PER_GROUP
D_EXPERT = 512
RMS_EPS = 1e-6
GN_EPS = 64e-5
L2_EPS = 1e-12
D_RWKV_IN = 3 * D_RWKV + RANK_W + RANK_A + RANK_G
D_HGRN_IN = 4 * D_HGRN

V7X_LANES = 128
V7X_SUBLANES = 8
V7X_MXU_DIM = 256
V7X_VMEM_LIMIT_BYTES = 56 * 1024 * 1024

CHUNK = 64
GROUP_LANES = V7X_MXU_DIM
HEADS_PER_GROUP = GROUP_LANES // RWKV_HEAD
N_RWKV_GROUPS = D_RWKV // GROUP_LANES
N_LEVELS = 6
HGRN_BATCH = 2
RWKV_BATCH = 2
TM_PROJ = 512
TM_DISPATCH = 512
TB_MOE = 256


def _dot(a, b):
    return jnp.dot(a, b, preferred_element_type=F32)


def _dot_nt(a, b):
    return lax.dot_general(a, b, (((1,), (1,)), ((), ())), preferred_element_type=F32)


def _dot_tn(a, b):
    return lax.dot_general(a, b, (((0,), (0,)), ((), ())), preferred_element_type=F32)


def _split_bf16(x):
    hi = x.astype(BF16)
    lo = (x - hi.astype(F32)).astype(BF16)
    return hi, lo


def _dot01_left(m01, x):
    hi, lo = _split_bf16(x)
    return _dot(m01, hi) + _dot(m01, lo)


def _dot01_right(x, m01):
    hi, lo = _split_bf16(x)
    return _dot(hi, m01) + _dot(lo, m01)


def _log1p_exp_neg_abs(x):
    return jnp.log1p(jnp.exp(-jnp.abs(x)))


def _log_sigmoid(x):
    return jnp.minimum(x, 0.0) - _log1p_exp_neg_abs(x)


def _sigmoid(x):
    return 1.0 / (1.0 + jnp.exp(-x))


def _rms(x, gain):
    return x * lax.rsqrt(jnp.mean(x * x, axis=-1, keepdims=True) + RMS_EPS) * gain


def _const_spec(shape):
    nd = len(shape)
    return pl.BlockSpec(shape, lambda *_: (0,) * nd)


def _inproj_kernel(x_ref, g_ref, w_ref, pa_ref, pb_ref):
    n = _rms(x_ref[...], g_ref[...]).astype(BF16)
    pa_ref[...] = _dot(n, w_ref[:, :D_RWKV_IN])
    pb_ref[...] = _dot(n, w_ref[:, D_RWKV_IN:])


def _inproj(x2d, gain, w_bf16):
    t = x2d.shape[0]
    return pl.pallas_call(
        _inproj_kernel,
        grid=(t // TM_PROJ,),
        in_specs=[
            pl.BlockSpec((TM_PROJ, D_MODEL), lambda i: (i, 0)),
            _const_spec((1, D_MODEL)),
            _const_spec((D_MODEL, D_RWKV_IN + D_HGRN_IN)),
        ],
        out_specs=[
            pl.BlockSpec((TM_PROJ, D_RWKV_IN), lambda i: (i, 0)),
            pl.BlockSpec((TM_PROJ, D_HGRN_IN), lambda i: (i, 0)),
        ],
        out_shape=[
            jax.ShapeDtypeStruct((t, D_RWKV_IN), F32),
            jax.ShapeDtypeStruct((t, D_HGRN_IN), F32),
        ],
        compiler_params=pltpu.CompilerParams(
            dimension_semantics=("parallel",), vmem_limit_bytes=V7X_VMEM_LIMIT_BYTES),
        name="inproj",
    )(x2d, gain, w_bf16)


_M_EYE, _M_BD, _M_STRICT, _M_INCL, _M_OFF0 = 0, 1, 2, 3, 4


def _rwkv_masks():
    i = np.arange(GROUP_LANES)[:, None]
    j = np.arange(GROUP_LANES)[None, :]
    bd = (i // CHUNK) == (j // CHUNK)
    masks = [i == j, bd, (j % CHUNK) < (i % CHUNK), (j % CHUNK) <= (i % CHUNK)]
    for lvl in range(N_LEVELS):
        s = 1 << lvl
        masks.append(((i // (2 * s)) == (j // (2 * s))) & ((i % (2 * s)) >= s) & ((j % (2 * s)) < s))
    return np.stack(masks).astype(np.float32)


def _tile_rows(x, n):
    return jnp.concatenate([x] * n, axis=0)


def _rwkv_kernel(p_ref, mu_ref, w0_ref, a0_ref, wwa_ref, gup_ref, kk_ref, ka_ref, rk_ref, lnw_ref,
                 lnb_ref, tri_ref, ones_ref, m_ref, mb_ref, o_ref, prev_ref, st_ref):
    c = pl.program_id(1)

    @pl.when(c == 0)
    def _():
        prev_ref[...] = jnp.zeros_like(prev_ref)
        st_ref[...] = jnp.zeros_like(st_ref)

    rows = RWKV_BATCH * CHUNK
    p = p_ref[...].reshape(rows, D_RWKV_IN)
    row = lax.broadcasted_iota(jnp.int32, p.shape, 0)
    shifted = pltpu.roll(p, 1, 0)
    for bi in range(RWKV_BATCH):
        shifted = jnp.where(row == bi * CHUNK, prev_ref[bi, V7X_SUBLANES - 1:V7X_SUBLANES, :], shifted)
        prev_ref[bi] = p[(bi + 1) * CHUNK - V7X_SUBLANES:(bi + 1) * CHUNK, :]
    p = p + mu_ref[...] * (shifted - p)

    r = p[:, 0:D_RWKV]
    k = p[:, D_RWKV:2 * D_RWKV]
    v = p[:, 2 * D_RWKV:3 * D_RWKV]
    x_wa = p[:, 3 * D_RWKV:3 * D_RWKV + RANK_W + RANK_A]
    dg = p[:, 3 * D_RWKV + RANK_W + RANK_A:]

    lane = lax.broadcasted_iota(jnp.int32, x_wa.shape, 1)
    t_wa = jnp.where(lane < RANK_W, jnp.tanh(x_wa), x_wa).astype(BF16)
    wa = _dot(t_wa, wwa_ref[...])
    w_lin = w0_ref[...] + wa[:, :D_RWKV]
    w = _log_sigmoid(w_lin) - 0.5
    log_decay = -jnp.exp(w)
    a = _sigmoid(a0_ref[...] + wa[:, D_RWKV:])
    g = _dot(_sigmoid(dg).astype(BF16), gup_ref[...])

    ones_bd = ones_ref[...]
    kk = k * kk_ref[...]
    kk = kk / jnp.maximum(jnp.sqrt(_dot01_right(kk * kk, ones_bd)), L2_EPS)
    k = k * (1.0 + (a - 1.0) * ka_ref[...])
    kb = kk * a

    sums = _dot01_left(tri_ref[...], log_decay)
    cum = sums[:rows]
    e_in = jnp.exp(cum)
    e_ex = jnp.exp(cum - log_decay)
    e_inv = jnp.exp(-cum)
    e_rem = jnp.exp(sums[rows:])

    r_hat = (r * e_in).astype(BF16)
    kk_hat = (kk * e_ex).astype(BF16)
    k_inv = (k * e_inv).astype(BF16)
    b_inv = (kb * e_inv).astype(BF16)
    k_rem = (k * e_rem).astype(BF16)
    b_rem = (kb * e_rem).astype(BF16)
    v_bf = v.astype(BF16)

    eye = m_ref[_M_EYE]
    bd = m_ref[_M_BD]
    strict_lc = m_ref[_M_STRICT][:CHUNK]
    incl_lc = m_ref[_M_INCL][:CHUNK]
    bd_bf = mb_ref[_M_BD]

    def block_diag(x_bf):
        return _tile_rows(x_bf, HEADS_PER_GROUP) * bd_bf

    chains = [(bi, gi) for bi in range(RWKV_BATCH) for gi in range(N_RWKV_GROUPS)]
    rs_of = lambda bi: slice(bi * CHUNK, (bi + 1) * CHUNK)
    sl_of = lambda gi: slice(gi * GROUP_LANES, (gi + 1) * GROUP_LANES)

    lhs, a_kk, a_rk, a_rb, a_kb_bd, inv = {}, {}, {}, {}, {}, {}
    for ch in chains:
        rs, sl = rs_of(ch[0]), sl_of(ch[1])
        lhs[ch] = jnp.concatenate([kk_hat[rs, sl], r_hat[rs, sl]], axis=0)
        a_k = _dot_nt(lhs[ch], block_diag(k_inv[rs, sl]))
        a_b = _dot_nt(lhs[ch], block_diag(b_inv[rs, sl]))
        a_kk[ch] = (a_k[:CHUNK] * strict_lc).astype(BF16)
        a_rk[ch] = (a_k[CHUNK:] * incl_lc).astype(BF16)
        a_rb[ch] = (a_b[CHUNK:] * incl_lc).astype(BF16)
        a_kb_bd[ch] = block_diag((a_b[:CHUNK] * strict_lc).astype(BF16))
        inv[ch] = eye - (a_kb_bd[ch] * mb_ref[_M_OFF0]).astype(F32)

    for lvl in range(1, N_LEVELS):
        inv_bf = {ch: inv[ch].astype(BF16) for ch in chains}
        mid = {ch: _dot(inv_bf[ch], a_kb_bd[ch] * mb_ref[_M_OFF0 + lvl]).astype(BF16) for ch in chains}
        for ch in chains:
            inv[ch] = inv[ch] - _dot(mid[ch], inv_bf[ch])

    ys = {}
    for ch in chains:
        bi, gi = ch
        rs, sl = rs_of(bi), sl_of(gi)
        inv_lc = inv[ch][0:CHUNK]
        for h in range(1, HEADS_PER_GROUP):
            inv_lc = inv_lc + inv[ch][h * CHUNK:(h + 1) * CHUNK]
        s_bd = st_ref[bi, gi]
        v_bd = block_diag(v_bf[rs, sl])
        from_state = _dot_nt(lhs[ch], s_bd.astype(BF16))
        rhs = from_state[:CHUNK] + _dot(a_kk[ch], v_bd)
        u_bf = _dot(inv_lc.astype(BF16), block_diag(rhs.astype(BF16))).astype(BF16)
        ys[ch] = from_state[CHUNK:] + _dot(a_rk[ch], v_bd) - _dot(a_rb[ch], block_diag(u_bf))

        vu = jnp.concatenate([v_bf[rs, sl], -u_bf], axis=0)
        kb_rem = jnp.concatenate([k_rem[rs, sl], b_rem[rs, sl]], axis=0)
        w_chunk = e_in[(bi + 1) * CHUNK - 1:(bi + 1) * CHUNK, sl]
        st_ref[bi, gi] = s_bd * w_chunk + _dot_tn(vu, kb_rem) * bd

    y = jnp.concatenate(
        [jnp.concatenate([ys[(bi, gi)] for gi in range(N_RWKV_GROUPS)], axis=1) for bi in range(RWKV_BATCH)],
        axis=0)
    inv_n = 1.0 / RWKV_HEAD
    mean = _dot01_right(y, ones_bd) * inv_n
    yc = y - mean
    var = _dot01_right(yc * yc, ones_bd) * inv_n
    yn = yc * lax.rsqrt(var + GN_EPS) * lnw_ref[...] + lnb_ref[...]
    bonus = _dot01_right(r * k * rk_ref[...], ones_bd)
    o_ref[...] = ((yn + bonus * v) * g).reshape(RWKV_BATCH, CHUNK, D_RWKV)


def _rwkv(pa, mu, w0, a0, wwa, gup, k_k, k_a, r_k, ln_w, ln_b):
    b, s, _ = pa.shape
    tril = np.tril(np.ones((CHUNK, CHUNK), np.float32))
    eye_b = np.eye(RWKV_BATCH, dtype=np.float32)
    tri = jnp.asarray(np.concatenate([np.kron(eye_b, tril), np.kron(eye_b, 1.0 - tril)], axis=0), BF16)
    hh = np.arange(D_RWKV) // RWKV_HEAD
    ones_bd = jnp.asarray((hh[:, None] == hh[None, :]).astype(np.float32), BF16)
    masks = jnp.asarray(_rwkv_masks())
    vec = _const_spec((1, D_RWKV))
    return pl.pallas_call(
        _rwkv_kernel,
        grid=(b // RWKV_BATCH, s // CHUNK),
        in_specs=[
            pl.BlockSpec((RWKV_BATCH, CHUNK, D_RWKV_IN), lambda i, j: (i, j, 0)),
            _const_spec((1, D_RWKV_IN)),
            vec, vec,
            _const_spec((RANK_W + RANK_A, 2 * D_RWKV)),
            _const_spec((RANK_G, D_RWKV)),
            vec, vec, vec, vec, vec,
            _const_spec(tri.shape),
            _const_spec((D_RWKV, D_RWKV)),
            _const_spec(masks.shape),
            _const_spec(masks.shape),
        ],
        out_specs=pl.BlockSpec((RWKV_BATCH, CHUNK, D_RWKV), lambda i, j: (i, j, 0)),
        out_shape=jax.ShapeDtypeStruct((b, s, D_RWKV), F32),
        scratch_shapes=[
            pltpu.VMEM((RWKV_BATCH, V7X_SUBLANES, D_RWKV_IN), F32),
            pltpu.VMEM((RWKV_BATCH, N_RWKV_GROUPS, GROUP_LANES, GROUP_LANES), F32),
        ],
        compiler_params=pltpu.CompilerParams(
            dimension_semantics=("parallel", "arbitrary"), vmem_limit_bytes=V7X_VMEM_LIMIT_BYTES),
        name="rwkv7",
    )(pa, mu, w0, a0, wwa, gup, k_k, k_a, r_k, ln_w, ln_b, tri, ones_bd, masks, masks.astype(BF16))


def _hgrn_sum_matrix():
    t = np.arange(CHUNK)[:, None]
    j = np.arange(CHUNK)[None, :]
    mats = [(j <= t), (j > t)]
    for lvl in range(N_LEVELS):
        s = 1 << lvl
        mid = (t // (2 * s)) * (2 * s) + s - 1
        mats.append(((j > mid) & (j <= t)).astype(np.float32) - ((j > t) & (j <= mid)).astype(np.float32))
    return np.concatenate([np.asarray(m, np.float32) for m in mats], axis=0)


def _hgrn_level_masks():
    t = np.arange(CHUNK)[:, None]
    s_ = np.arange(CHUNK)[None, :]
    masks = [t == s_]
    for lvl in range(N_LEVELS):
        s = 1 << lvl
        masks.append(((t // (2 * s)) == (s_ // (2 * s))) & ((t % (2 * s)) >= s) & ((s_ % (2 * s)) < s))
    return np.stack(masks).astype(np.float32)


def _hgrn_kernel(p_ref, cw_ref, lbl_ref, ng_ref, sm_ref, lm_ref, o_ref, prev_ref, st_ref):
    c = pl.program_id(1)

    @pl.when(c == 0)
    def _():
        prev_ref[...] = jnp.zeros_like(prev_ref)
        st_ref[...] = jnp.zeros_like(st_ref)

    l0 = lbl_ref[0:1, :]
    l1 = lbl_ref[1:2, :]
    lmax = jnp.maximum(l0, l1)
    e0 = jnp.exp(l0 - lmax)
    lb = e0 / (e0 + jnp.exp(l1 - lmax))
    ta = jnp.log(lb)

    q_lv, k_lv, q_inter, k_inter, w_chunk, val_bf = {}, {}, {}, {}, {}, {}
    for bi in range(HGRN_BATCH):
        x = p_ref[bi, :, :3 * D_HGRN]
        xe = jnp.concatenate([prev_ref[bi], x], axis=0)
        prev_ref[bi] = x[CHUNK - V7X_SUBLANES:, :]
        conv = x * cw_ref[CONV_W - 1:CONV_W, :]
        for s in range(1, CONV_W):
            conv = conv + pltpu.roll(xe, s, 0)[V7X_SUBLANES:, :] * cw_ref[CONV_W - 1 - s:CONV_W - s, :]

        qc = conv[:, :D_HGRN]
        f = conv[:, D_HGRN:2 * D_HGRN]
        val_bf[bi] = conv[:, 2 * D_HGRN:].astype(BF16)
        q = qc * _sigmoid(qc)
        tb = jnp.log1p(-lb) + _log_sigmoid(f)
        log_f = jnp.maximum(ta, tb) + _log1p_exp_neg_abs(ta - tb)
        kg = (1.0 - lb) * _sigmoid(-f)

        sums = _dot01_left(sm_ref[...], log_f)
        b_in = sums[0:CHUNK]
        q_inter[bi] = (q * jnp.exp(b_in)).astype(BF16)
        k_inter[bi] = (kg * jnp.exp(sums[CHUNK:2 * CHUNK])).astype(BF16)
        w_chunk[bi] = jnp.exp(b_in[CHUNK - 1:CHUNK, :])
        q_lv[bi] = [q.astype(BF16)]
        k_lv[bi] = [kg.astype(BF16)]
        for lvl in range(N_LEVELS):
            e = jnp.exp(-jnp.abs(sums[(2 + lvl) * CHUNK:(3 + lvl) * CHUNK]))
            q_lv[bi].append((q * e).astype(BF16))
            k_lv[bi].append((kg * e).astype(BF16))

    chains = [(bi, h) for bi in range(HGRN_BATCH) for h in range(HGRN_HEADS)]
    sl_of = lambda h: slice(h * HGRN_HEAD, (h + 1) * HGRN_HEAD)
    scores = {}
    for lvl in range(N_LEVELS + 1):
        for ch in chains:
            bi, sl = ch[0], sl_of(ch[1])
            term = _dot_nt(q_lv[bi][lvl][:, sl], k_lv[bi][lvl][:, sl]) * lm_ref[lvl]
            scores[ch] = term if lvl == 0 else scores[ch] + term
    outs = {}
    for ch in chains:
        bi, h = ch
        sl = sl_of(h)
        st = st_ref[bi, h]
        o = _dot(scores[ch].astype(BF16), val_bf[bi][:, sl]) + _dot_nt(q_inter[bi][:, sl], st.astype(BF16))
        st_ref[bi, h] = st * w_chunk[bi][:, sl] + _dot_tn(val_bf[bi][:, sl], k_inter[bi][:, sl])
        outs[ch] = o * lax.rsqrt(jnp.mean(o * o, axis=-1, keepdims=True) + RMS_EPS)
    for bi in range(HGRN_BATCH):
        gate = p_ref[bi, :, 3 * D_HGRN:]
        o = jnp.concatenate([outs[(bi, h)] for h in range(HGRN_HEADS)], axis=1)
        o_ref[bi] = o * ng_ref[...] * (gate * _sigmoid(gate))


def _hgrn(pb, conv_w, lb_logits, norm_g):
    b, s, _ = pb.shape
    sum_m = jnp.asarray(_hgrn_sum_matrix(), BF16)
    lvl_m = jnp.asarray(_hgrn_level_masks())
    return pl.pallas_call(
        _hgrn_kernel,
        grid=(b // HGRN_BATCH, s // CHUNK),
        in_specs=[
            pl.BlockSpec((HGRN_BATCH, CHUNK, D_HGRN_IN), lambda i, j: (i, j, 0)),
            _const_spec((CONV_W, 3 * D_HGRN)),
            _const_spec(lb_logits.shape),
            _const_spec((1, D_HGRN)),
            _const_spec(sum_m.shape),
            _const_spec(lvl_m.shape),
        ],
        out_specs=pl.BlockSpec((HGRN_BATCH, CHUNK, D_HGRN), lambda i, j: (i, j, 0)),
        out_shape=jax.ShapeDtypeStruct((b, s, D_HGRN), F32),
        scratch_shapes=[
            pltpu.VMEM((HGRN_BATCH, V7X_SUBLANES, 3 * D_HGRN), F32),
            pltpu.VMEM((HGRN_BATCH, HGRN_HEADS, HGRN_HEAD, HGRN_HEAD), F32),
        ],
        compiler_params=pltpu.CompilerParams(
            dimension_semantics=("parallel", "arbitrary"), vmem_limit_bytes=V7X_VMEM_LIMIT_BYTES),
        name="hgrn2",
    )(pb, conv_w, lb_logits, norm_g, sum_m, lvl_m)


ROUTER_LANES = V7X_LANES


_R_E1, _R_E2, _R_G1, _R_G2, _R_RANK1, _R_RANK2 = 0, 1, 2, 3, 4, 5


def _outproj_kernel(x_ref, ya_ref, yb_ref, wo_ref, g2_ref, wr_ref, br_ref, tri_ref,
                    h_ref, n_ref, meta_ref, counts_ref, carry_ref):
    @pl.when(pl.program_id(0) == 0)
    def _():
        carry_ref[...] = jnp.zeros_like(carry_ref)

    h = (x_ref[...]
         + _dot(ya_ref[...].astype(BF16), wo_ref[:D_RWKV, :])
         + _dot(yb_ref[...].astype(BF16), wo_ref[D_RWKV:, :]))
    h_ref[...] = h
    n = _rms(h, g2_ref[...])
    n_ref[...] = n

    logits = jnp.dot(n, wr_ref[...], preferred_element_type=F32,
                     precision=lax.Precision.HIGHEST) + br_ref[...]
    lane = lax.broadcasted_iota(jnp.int32, logits.shape, 1).astype(F32)
    neg = -jnp.inf
    gl = jnp.where((lane >= N_EXPERTS) & (lane < N_EXPERTS + N_GROUPS), logits, neg)
    gmax = jnp.max(gl, axis=-1, keepdims=True)
    gidx = jnp.min(jnp.where(gl == gmax, lane - N_EXPERTS, ROUTER_LANES), axis=-1, keepdims=True)
    g_w = 1.0 / jnp.sum(jnp.exp(gl - gmax), axis=-1, keepdims=True)
    lo = gidx * EXPERTS_PER_GROUP
    el = jnp.where((lane >= lo) & (lane < lo + EXPERTS_PER_GROUP), logits, neg)
    m1 = jnp.max(el, axis=-1, keepdims=True)
    i1 = jnp.min(jnp.where(el == m1, lane, ROUTER_LANES), axis=-1, keepdims=True)
    el2 = jnp.where(lane == i1, neg, el)
    m2 = jnp.max(el2, axis=-1, keepdims=True)
    i2 = jnp.min(jnp.where(el2 == m2, lane, ROUTER_LANES), axis=-1, keepdims=True)
    t = jnp.exp(m2 - m1)
    w1 = 1.0 / (1.0 + t)

    onehot1 = lane == i1
    onehot2 = lane == i2
    assigned = jnp.where(onehot1 | onehot2, 1.0, 0.0)
    before = _dot(tri_ref[...], assigned.astype(BF16)) + carry_ref[...]
    rank1 = jnp.sum(jnp.where(onehot1, before, 0.0), axis=-1, keepdims=True)
    rank2 = jnp.sum(jnp.where(onehot2, before, 0.0), axis=-1, keepdims=True)
    carry_ref[...] += jnp.sum(assigned, axis=0, keepdims=True)
    counts_ref[...] = jnp.broadcast_to(carry_ref[...], counts_ref.shape)

    meta = jnp.zeros_like(logits)
    for idx, val in ((_R_E1, i1), (_R_E2, i2), (_R_G1, g_w * w1), (_R_G2, g_w * (t * w1)),
                     (_R_RANK1, rank1), (_R_RANK2, rank2)):
        meta = jnp.where(lane == idx, val, meta)
    meta_ref[...] = meta


def _outproj(x2d, ya, yb, wo_bf16, g2, wr, br):
    t = x2d.shape[0]
    row = lambda w: pl.BlockSpec((TM_PROJ, w), lambda i: (i, 0))
    tri = jnp.asarray(np.tril(np.ones((TM_PROJ, TM_PROJ), np.float32), -1), BF16)
    return pl.pallas_call(
        _outproj_kernel,
        grid=(t // TM_PROJ,),
        in_specs=[
            row(D_MODEL), row(D_RWKV), row(D_HGRN),
            _const_spec((D_RWKV + D_HGRN, D_MODEL)),
            _const_spec((1, D_MODEL)),
            _const_spec((D_MODEL, ROUTER_LANES)),
            _const_spec((1, ROUTER_LANES)),
            _const_spec((TM_PROJ, TM_PROJ)),
        ],
        out_specs=[row(D_MODEL), row(D_MODEL), row(ROUTER_LANES), _const_spec((V7X_SUBLANES, ROUTER_LANES))],
        out_shape=[
            jax.ShapeDtypeStruct((t, D_MODEL), F32),
            jax.ShapeDtypeStruct((t, D_MODEL), F32),
            jax.ShapeDtypeStruct((t, ROUTER_LANES), F32),
            jax.ShapeDtypeStruct((V7X_SUBLANES, ROUTER_LANES), F32),
        ],
        scratch_shapes=[pltpu.VMEM((1, ROUTER_LANES), F32)],
        compiler_params=pltpu.CompilerParams(
            dimension_semantics=("arbitrary",), vmem_limit_bytes=V7X_VMEM_LIMIT_BYTES),
        name="outproj_router",
    )(x2d, ya, yb, wo_bf16, g2, wr, br, tri)


def _slot_kernel(meta_ref, offs_ref, pos_ref):
    meta = meta_ref[...]
    lane = lax.broadcasted_iota(jnp.int32, meta.shape, 1).astype(F32)
    offs = offs_ref[...]

    def slot(e_lane, rank_lane):
        e = meta[:, e_lane:e_lane + 1]
        return jnp.sum(jnp.where(lane == e, offs, 0.0), axis=-1, keepdims=True) + meta[:, rank_lane:rank_lane + 1]

    pos = jnp.where(lane == 0, slot(_R_E1, _R_RANK1), jnp.where(lane == 1, slot(_R_E2, _R_RANK2), 0.0))
    pos_ref[...] = pos.astype(jnp.int32)


def _slots(meta, offs):
    t = meta.shape[0]
    return pl.pallas_call(
        _slot_kernel,
        grid=(t // TM_PROJ,),
        in_specs=[pl.BlockSpec((TM_PROJ, ROUTER_LANES), lambda i: (i, 0)), _const_spec((1, ROUTER_LANES))],
        out_specs=pl.BlockSpec((TM_PROJ, ROUTER_LANES), lambda i: (i, 0)),
        out_shape=jax.ShapeDtypeStruct((t, ROUTER_LANES), jnp.int32),
        compiler_params=pltpu.CompilerParams(dimension_semantics=("parallel",)),
        name="moe_slots",
    )(meta, offs)


def _row_copy(src_ref, src_row, dst_ref, dst_row, sem):
    return pltpu.make_async_copy(src_ref.at[pl.ds(src_row, 1)], dst_ref.at[pl.ds(dst_row, 1)], sem)


def _dispatch_kernel(pos_ref, n_ref, xs_in_ref, xs_ref, sem):
    del xs_in_ref

    def body(r, carry):
        _row_copy(n_ref, r, xs_ref, pos_ref[2 * r], sem).start()
        _row_copy(n_ref, r, xs_ref, pos_ref[2 * r + 1], sem).start()
        return carry

    lax.fori_loop(0, TM_DISPATCH, body, 0, unroll=8)
    for _ in range(2):
        pltpu.make_async_copy(n_ref, xs_ref.at[pl.ds(0, TM_DISPATCH)], sem).wait()


def _dispatch(pos_flat, n2, n_slots):
    t = n2.shape[0]
    xs0 = jnp.zeros((n_slots, D_MODEL), F32)
    return pl.pallas_call(
        _dispatch_kernel,
        grid=(t // TM_DISPATCH,),
        in_specs=[
            pl.BlockSpec((2 * TM_DISPATCH,), lambda i: (i,), memory_space=pltpu.SMEM),
            pl.BlockSpec((TM_DISPATCH, D_MODEL), lambda i: (i, 0)),
            pl.BlockSpec(memory_space=pl.ANY),
        ],
        out_specs=pl.BlockSpec(memory_space=pl.ANY),
        out_shape=jax.ShapeDtypeStruct((n_slots, D_MODEL), F32),
        scratch_shapes=[pltpu.SemaphoreType.DMA(())],
        input_output_aliases={2: 0},
        compiler_params=pltpu.CompilerParams(
            dimension_semantics=("arbitrary",), has_side_effects=True),
        name="moe_dispatch",
    )(pos_flat, n2, xs0)


def _gmm_kernel(be_ref, nu_ref, xs_ref, wg_ref, wu_ref, wd_ref, y_ref, wgu_s, wd_s):
    b = pl.program_id(0)

    @pl.when(b < nu_ref[0])
    def _():
        @pl.when((b == 0) | (be_ref[b] != be_ref[jnp.maximum(b - 1, 0)]))
        def _():
            wgu_s[:, :D_EXPERT] = wg_ref[...].astype(BF16)
            wgu_s[:, D_EXPERT:] = wu_ref[...].astype(BF16)
            wd_s[...] = wd_ref[...].astype(BF16)

        gu = _dot(xs_ref[...].astype(BF16), wgu_s[...])
        hg = gu[:, :D_EXPERT]
        hid = (hg * _sigmoid(hg) * gu[:, D_EXPERT:]).astype(BF16)
        y_ref[...] = _dot(hid, wd_s[...])

    @pl.when(b >= nu_ref[0])
    def _():
        y_ref[...] = jnp.zeros_like(y_ref)


def _gmm(block_expert, n_used, xs, wg, wu, wd):
    n_slots = xs.shape[0]
    blk = lambda b, be, nu: (jnp.minimum(b, nu[0] - 1), 0)
    wsel = lambda b, be, nu: (be[jnp.minimum(b, nu[0] - 1)], 0, 0)
    return pl.pallas_call(
        _gmm_kernel,
        grid_spec=pltpu.PrefetchScalarGridSpec(
            num_scalar_prefetch=2,
            grid=(n_slots // TB_MOE,),
            in_specs=[
                pl.BlockSpec((TB_MOE, D_MODEL), blk),
                pl.BlockSpec((None, D_MODEL, D_EXPERT), wsel),
                pl.BlockSpec((None, D_MODEL, D_EXPERT), wsel),
                pl.BlockSpec((None, D_EXPERT, D_MODEL), wsel),
            ],
            out_specs=pl.BlockSpec((TB_MOE, D_MODEL), lambda b, be, nu: (b, 0)),
            scratch_shapes=[
                pltpu.VMEM((D_MODEL, 2 * D_EXPERT), BF16),
                pltpu.VMEM((D_EXPERT, D_MODEL), BF16),
            ],
        ),
        out_shape=jax.ShapeDtypeStruct((n_slots, D_MODEL), F32),
        compiler_params=pltpu.CompilerParams(
            dimension_semantics=("arbitrary",), vmem_limit_bytes=V7X_VMEM_LIMIT_BYTES),
        name="moe_gmm",
    )(block_expert, n_used, xs, wg, wu, wd)


def _combine_kernel(pos_ref, pos_next_ref, meta_ref, h_ref, gf_ref, ys_ref, o_ref, buf_ref, sem):
    i = pl.program_id(0)
    slot = lax.rem(i, 2)

    def issue(p_ref, s):
        def body(r, carry):
            _row_copy(ys_ref, p_ref[2 * r], buf_ref.at[s, 0], r, sem.at[s]).start()
            _row_copy(ys_ref, p_ref[2 * r + 1], buf_ref.at[s, 1], r, sem.at[s]).start()
            return carry
        lax.fori_loop(0, TM_DISPATCH, body, 0, unroll=8)

    @pl.when(i == 0)
    def _():
        issue(pos_ref, 0)

    @pl.when(i + 1 < pl.num_programs(0))
    def _():
        issue(pos_next_ref, 1 - slot)

    for j in range(2):
        pltpu.make_async_copy(ys_ref.at[pl.ds(0, TM_DISPATCH)], buf_ref.at[slot, j], sem.at[slot]).wait()

    meta = meta_ref[...]
    g1 = meta[:, _R_G1:_R_G1 + 1]
    g2 = meta[:, _R_G2:_R_G2 + 1]
    moe = g1 * buf_ref[slot, 0] + g2 * buf_ref[slot, 1]
    o_ref[...] = _rms(h_ref[...] + moe, gf_ref[...])


def _combine(pos_flat, meta, h, gf, ys):
    t = h.shape[0]
    n_tiles = t // TM_DISPATCH
    return pl.pallas_call(
        _combine_kernel,
        grid=(n_tiles,),
        in_specs=[
            pl.BlockSpec((2 * TM_DISPATCH,), lambda i: (i,), memory_space=pltpu.SMEM),
            pl.BlockSpec((2 * TM_DISPATCH,), lambda i: (jnp.minimum(i + 1, n_tiles - 1),),
                         memory_space=pltpu.SMEM),
            pl.BlockSpec((TM_DISPATCH, ROUTER_LANES), lambda i: (i, 0)),
            pl.BlockSpec((TM_DISPATCH, D_MODEL), lambda i: (i, 0)),
            _const_spec((1, D_MODEL)),
            pl.BlockSpec(memory_space=pl.ANY),
        ],
        out_specs=pl.BlockSpec((TM_DISPATCH, D_MODEL), lambda i: (i, 0)),
        out_shape=jax.ShapeDtypeStruct((t, D_MODEL), F32),
        scratch_shapes=[
            pltpu.VMEM((2, 2, TM_DISPATCH, D_MODEL), F32),
            pltpu.SemaphoreType.DMA((2,)),
        ],
        compiler_params=pltpu.CompilerParams(
            dimension_semantics=("arbitrary",), vmem_limit_bytes=V7X_VMEM_LIMIT_BYTES),
        name="moe_combine",
    )(pos_flat, pos_flat, meta, h, gf, ys)


def _moe(n2, meta, counts, h, wg, wu, wd, gf):
    t = n2.shape[0]
    n_blocks = (2 * t + N_EXPERTS * (TB_MOE - 1) + TB_MOE - 1) // TB_MOE
    cnt = counts[0, :N_EXPERTS].astype(jnp.int32)
    blocks = (cnt + TB_MOE - 1) // TB_MOE
    ends = jnp.cumsum(blocks)
    offs = ((ends - blocks) * TB_MOE).astype(F32)
    offs_row = jnp.zeros((1, ROUTER_LANES), F32).at[0, :N_EXPERTS].set(offs)
    block_ids = jnp.arange(n_blocks, dtype=jnp.int32)
    block_expert = jnp.minimum(
        jnp.sum((ends[None, :] <= block_ids[:, None]).astype(jnp.int32), axis=1), N_EXPERTS - 1)
    n_used = ends[-1:].astype(jnp.int32)

    pos = _slots(meta, offs_row)
    pos_flat = pos[:, :2].reshape(-1)
    xs = _dispatch(pos_flat, n2, n_blocks * TB_MOE)
    ys = _gmm(block_expert, n_used, xs, wg, wu, wd)
    return _combine(pos_flat, meta, h, gf, ys)


def kernel(x, norm1_g, w_in, rwkv_mu, rwkv_w0, rwkv_w_up, rwkv_a0, rwkv_a_up, rwkv_g_up, rwkv_k_k,
           rwkv_k_a, rwkv_r_k, rwkv_ln_w, rwkv_ln_b, hgrn_conv_w, hgrn_lb_logits, hgrn_norm_g, w_out,
           norm2_g, router_g_w, router_g_b, router_e_w, router_e_b, exp_w_gate, exp_w_up, exp_w_down,
           final_norm_g):
    b, s, d = x.shape
    t = b * s
    l = 0
    x2d = x.reshape(t, d)
    row = lambda a: a.reshape(1, -1)

    pa, pb = _inproj(x2d, row(norm1_g[l]), w_in[l].astype(BF16))

    wwa = jnp.zeros((RANK_W + RANK_A, 2 * D_RWKV), F32)
    wwa = wwa.at[:RANK_W, :D_RWKV].set(rwkv_w_up[l]).at[RANK_W:, D_RWKV:].set(rwkv_a_up[l]).astype(BF16)
    ya = _rwkv(pa.reshape(b, s, D_RWKV_IN), row(rwkv_mu[l]), row(rwkv_w0[l]), row(rwkv_a0[l]), wwa,
               rwkv_g_up[l].astype(BF16), row(rwkv_k_k[l]), row(rwkv_k_a[l]), row(rwkv_r_k[l]),
               row(rwkv_ln_w[l]), row(rwkv_ln_b[l]))
    yb = _hgrn(pb.reshape(b, s, D_HGRN_IN), hgrn_conv_w[l], hgrn_lb_logits, row(hgrn_norm_g[l]))

    wr = jnp.zeros((D_MODEL, ROUTER_LANES), F32)
    wr = wr.at[:, :N_EXPERTS].set(router_e_w[l]).at[:, N_EXPERTS:N_EXPERTS + N_GROUPS].set(router_g_w[l])
    br = jnp.zeros((1, ROUTER_LANES), F32)
    br = br.at[0, :N_EXPERTS].set(router_e_b[l]).at[0, N_EXPERTS:N_EXPERTS + N_GROUPS].set(router_g_b[l])
    h, n2, meta, counts = _outproj(x2d, ya.reshape(t, D_RWKV), yb.reshape(t, D_HGRN),
                                   w_out[l].astype(BF16), row(norm2_g[l]), wr, br)

    out = _moe(n2, meta, counts, h, exp_w_gate[l], exp_w_up[l], exp_w_down[l], row(final_norm_g))
    return out.reshape(b, s, d)
```

```python
import functools

import numpy as np
import jax
import jax.numpy as jnp
from jax import lax
from jax.experimental import pallas as pl
from jax.experimental.pallas import tpu as pltpu

F32 = jnp.float32
BF16 = jnp.bfloat16

D_MODEL = 1024
D_RWKV = 512
D_HGRN = 512
RWKV_HEAD = 64
RANK_W = 64
RANK_A = 64
RANK_G = 128
HGRN_HEAD = 128
HGRN_HEADS = D_HGRN // HGRN_HEAD
CONV_W = 4
N_GROUPS = 4
EXPERTS_PER_GROUP = 8
N_EXPERTS = N_GROUPS * EXPERTS_PER_GROUP
D_EXPERT = 512
RMS_EPS = 1e-6
GN_EPS = 64e-5
L2_EPS = 1e-12
D_RWKV_IN = 3 * D_RWKV + RANK_W + RANK_A + RANK_G
D_HGRN_IN = 4 * D_HGRN

V7X_LANES = 128
V7X_SUBLANES = 8
V7X_MXU_DIM = 256
V7X_VMEM_LIMIT_BYTES = 56 * 1024 * 1024

CHUNK = 64
GROUP_LANES = V7X_MXU_DIM
HEADS_PER_GROUP = GROUP_LANES // RWKV_HEAD
N_RWKV_GROUPS = D_RWKV // GROUP_LANES
N_LEVELS = 6
HGRN_BATCH = 2
RWKV_BATCH = 2
TM_PROJ = 512
TM_DISPATCH = 512
TB_MOE = 512


def _dot(a, b):
    return jnp.dot(a, b, preferred_element_type=F32)


def _dot_nt(a, b):
    return lax.dot_general(a, b, (((1,), (1,)), ((), ())), preferred_element_type=F32)


def _dot_tn(a, b):
    return lax.dot_general(a, b, (((0,), (0,)), ((), ())), preferred_element_type=F32)


def _split_bf16(x):
    hi = x.astype(BF16)
    lo = (x - hi.astype(F32)).astype(BF16)
    return hi, lo


def _dot01_left(m01, x):
    hi, lo = _split_bf16(x)
    return _dot(m01, hi) + _dot(m01, lo)


def _log1p_exp_neg_abs(x):
    return jnp.log1p(jnp.exp(-jnp.abs(x)))


def _log_sigmoid(x):
    return jnp.minimum(x, 0.0) - _log1p_exp_neg_abs(x)


def _sigmoid(x):
    return 1.0 / (1.0 + jnp.exp(-x))


def _rms(x, gain):
    return x * lax.rsqrt(jnp.mean(x * x, axis=-1, keepdims=True) + RMS_EPS) * gain


def _const_spec(shape):
    nd = len(shape)
    return pl.BlockSpec(shape, lambda *_: (0,) * nd)


def _inproj_kernel(x_ref, g_ref, w_ref, pa_ref, pb_ref):
    n = _rms(x_ref[...], g_ref[...]).astype(BF16)
    pa_ref[...] = _dot(n, w_ref[:, :D_RWKV_IN])
    pb_ref[...] = _dot(n, w_ref[:, D_RWKV_IN:])


def _inproj(x2d, gain, w_bf16):
    t = x2d.shape[0]
    return pl.pallas_call(
        _inproj_kernel,
        grid=(t // TM_PROJ,),
        in_specs=[
            pl.BlockSpec((TM_PROJ, D_MODEL), lambda i: (i, 0)),
            _const_spec((1, D_MODEL)),
            _const_spec((D_MODEL, D_RWKV_IN + D_HGRN_IN)),
        ],
        out_specs=[
            pl.BlockSpec((TM_PROJ, D_RWKV_IN), lambda i: (i, 0)),
            pl.BlockSpec((TM_PROJ, D_HGRN_IN), lambda i: (i, 0)),
        ],
        out_shape=[
            jax.ShapeDtypeStruct((t, D_RWKV_IN), F32),
            jax.ShapeDtypeStruct((t, D_HGRN_IN), F32),
        ],
        compiler_params=pltpu.CompilerParams(
            dimension_semantics=("parallel",), vmem_limit_bytes=V7X_VMEM_LIMIT_BYTES),
        name="inproj",
    )(x2d, gain, w_bf16)


_M_EYE, _M_BD, _M_STRICT, _M_INCL, _M_OFF0 = 0, 1, 2, 3, 4


def _rwkv_masks():
    i = np.arange(GROUP_LANES)[:, None]
    j = np.arange(GROUP_LANES)[None, :]
    bd = (i // CHUNK) == (j // CHUNK)
    masks = [i == j, bd, (j % CHUNK) < (i % CHUNK), (j % CHUNK) <= (i % CHUNK)]
    for lvl in range(N_LEVELS):
        s = 1 << lvl
        masks.append(((i // (2 * s)) == (j // (2 * s))) & ((i % (2 * s)) >= s) & ((j % (2 * s)) < s))
    return np.stack(masks).astype(np.float32)


def _tile_rows(x, n):
    return jnp.concatenate([x] * n, axis=0)


def _rwkv_kernel(p_ref, mu_ref, w0_ref, a0_ref, wwa_ref, gup_ref, kk_ref, ka_ref, rk_ref, lnw_ref,
                 lnb_ref, tri_ref, m_ref, mb_ref, o_ref, prev_ref, st_ref):
    c = pl.program_id(1)

    @pl.when(c == 0)
    def _():
        prev_ref[...] = jnp.zeros_like(prev_ref)
        st_ref[...] = jnp.zeros_like(st_ref)

    rows = RWKV_BATCH * CHUNK
    p = p_ref[...].reshape(rows, D_RWKV_IN)
    row = lax.broadcasted_iota(jnp.int32, p.shape, 0)
    shifted = pltpu.roll(p, 1, 0)
    for bi in range(RWKV_BATCH):
        shifted = jnp.where(row == bi * CHUNK, prev_ref[bi, V7X_SUBLANES - 1:V7X_SUBLANES, :], shifted)
        prev_ref[bi] = p[(bi + 1) * CHUNK - V7X_SUBLANES:(bi + 1) * CHUNK, :]
    p = p + mu_ref[...] * (shifted - p)

    r = p[:, 0:D_RWKV]
    k = p[:, D_RWKV:2 * D_RWKV]
    v = p[:, 2 * D_RWKV:3 * D_RWKV]
    x_wa = p[:, 3 * D_RWKV:3 * D_RWKV + RANK_W + RANK_A]
    dg = p[:, 3 * D_RWKV + RANK_W + RANK_A:]

    lane = lax.broadcasted_iota(jnp.int32, x_wa.shape, 1)
    t_wa = jnp.where(lane < RANK_W, jnp.tanh(x_wa), x_wa).astype(BF16)
    wa = _dot(t_wa, wwa_ref[...])
    w_lin = w0_ref[...] + wa[:, :D_RWKV]
    w = _log_sigmoid(w_lin) - 0.5
    log_decay = -jnp.exp(w)
    a = _sigmoid(a0_ref[...] + wa[:, D_RWKV:])
    g = _dot(_sigmoid(dg).astype(BF16), gup_ref[...])

    bd_bf = mb_ref[_M_BD]

    def head_sum(x):
        hi, lo = _split_bf16(x)
        parts = [hi[:, :GROUP_LANES], hi[:, GROUP_LANES:], lo[:, :GROUP_LANES], lo[:, GROUP_LANES:]]
        s = _dot(jnp.concatenate(parts, axis=0), bd_bf)
        return jnp.concatenate([s[:rows] + s[2 * rows:3 * rows], s[rows:2 * rows] + s[3 * rows:]], axis=1)

    kk = k * kk_ref[...]
    kk = kk / jnp.maximum(jnp.sqrt(head_sum(kk * kk)), L2_EPS)
    k = k * (1.0 + (a - 1.0) * ka_ref[...])
    kb = kk * a

    sums = _dot01_left(tri_ref[...], log_decay)
    cum = sums[:rows]
    e_in = jnp.exp(cum)
    e_ex = jnp.exp(cum - log_decay)
    e_inv = jnp.exp(-cum)
    e_rem = jnp.exp(sums[rows:])

    r_hat = (r * e_in).astype(BF16)
    kk_hat = (kk * e_ex).astype(BF16)
    k_inv = (k * e_inv).astype(BF16)
    b_inv = (kb * e_inv).astype(BF16)
    k_rem = (k * e_rem).astype(BF16)
    b_rem = (kb * e_rem).astype(BF16)
    v_bf = v.astype(BF16)

    eye = m_ref[_M_EYE]
    bd = m_ref[_M_BD]
    strict_lc = m_ref[_M_STRICT][:CHUNK]
    incl_lc = m_ref[_M_INCL][:CHUNK]

    def block_diag(x_bf):
        return _tile_rows(x_bf, HEADS_PER_GROUP) * bd_bf

    chains = [(bi, gi) for bi in range(RWKV_BATCH) for gi in range(N_RWKV_GROUPS)]
    rs_of = lambda bi: slice(bi * CHUNK, (bi + 1) * CHUNK)
    sl_of = lambda gi: slice(gi * GROUP_LANES, (gi + 1) * GROUP_LANES)

    lhs, a_kk, a_rk, a_rb, a_kb_bd, inv = {}, {}, {}, {}, {}, {}
    for ch in chains:
        rs, sl = rs_of(ch[0]), sl_of(ch[1])
        lhs[ch] = jnp.concatenate([kk_hat[rs, sl], r_hat[rs, sl]], axis=0)
        a_k = _dot_nt(lhs[ch], block_diag(k_inv[rs, sl]))
        a_b = _dot_nt(lhs[ch], block_diag(b_inv[rs, sl]))
        a_kk[ch] = (a_k[:CHUNK] * strict_lc).astype(BF16)
        a_rk[ch] = (a_k[CHUNK:] * incl_lc).astype(BF16)
        a_rb[ch] = (a_b[CHUNK:] * incl_lc).astype(BF16)
        a_kb_bd[ch] = block_diag((a_b[:CHUNK] * strict_lc).astype(BF16))
        inv[ch] = eye - (a_kb_bd[ch] * mb_ref[_M_OFF0]).astype(F32)

    for lvl in range(1, N_LEVELS):
        inv_bf = {ch: inv[ch].astype(BF16) for ch in chains}
        mid = {ch: _dot(inv_bf[ch], a_kb_bd[ch] * mb_ref[_M_OFF0 + lvl]).astype(BF16) for ch in chains}
        for ch in chains:
            inv[ch] = inv[ch] - _dot(mid[ch], inv_bf[ch])

    ys = {}
    for ch in chains:
        bi, gi = ch
        rs, sl = rs_of(bi), sl_of(gi)
        inv_lc = inv[ch][0:CHUNK]
        for h in range(1, HEADS_PER_GROUP):
            inv_lc = inv_lc + inv[ch][h * CHUNK:(h + 1) * CHUNK]
        s_bd = st_ref[bi, gi]
        v_bd = block_diag(v_bf[rs, sl])
        from_state = _dot_nt(lhs[ch], s_bd.astype(BF16))
        rhs = from_state[:CHUNK] + _dot(a_kk[ch], v_bd)
        u_bf = _dot(inv_lc.astype(BF16), block_diag(rhs.astype(BF16))).astype(BF16)
        ys[ch] = from_state[CHUNK:] + _dot(a_rk[ch], v_bd) - _dot(a_rb[ch], block_diag(u_bf))

        vu = jnp.concatenate([v_bf[rs, sl], -u_bf], axis=0)
        kb_rem = jnp.concatenate([k_rem[rs, sl], b_rem[rs, sl]], axis=0)
        w_chunk = e_in[(bi + 1) * CHUNK - 1:(bi + 1) * CHUNK, sl]
        st_ref[bi, gi] = s_bd * w_chunk + _dot_tn(vu, kb_rem) * bd

    y = jnp.concatenate(
        [jnp.concatenate([ys[(bi, gi)] for gi in range(N_RWKV_GROUPS)], axis=1) for bi in range(RWKV_BATCH)],
        axis=0)
    inv_n = 1.0 / RWKV_HEAD
    mean = head_sum(y) * inv_n
    yc = y - mean
    var = head_sum(yc * yc) * inv_n
    yn = yc * lax.rsqrt(var + GN_EPS) * lnw_ref[...] + lnb_ref[...]
    bonus = head_sum(r * k * rk_ref[...])
    o_ref[...] = ((yn + bonus * v) * g).reshape(RWKV_BATCH, CHUNK, D_RWKV)


def _rwkv(pa, mu, w0, a0, wwa, gup, k_k, k_a, r_k, ln_w, ln_b):
    b, s, _ = pa.shape
    tril = np.tril(np.ones((CHUNK, CHUNK), np.float32))
    eye_b = np.eye(RWKV_BATCH, dtype=np.float32)
    tri = jnp.asarray(np.concatenate([np.kron(eye_b, tril), np.kron(eye_b, 1.0 - tril)], axis=0), BF16)
    masks = jnp.asarray(_rwkv_masks())
    vec = _const_spec((1, D_RWKV))
    return pl.pallas_call(
        _rwkv_kernel,
        grid=(b // RWKV_BATCH, s // CHUNK),
        in_specs=[
            pl.BlockSpec((RWKV_BATCH, CHUNK, D_RWKV_IN), lambda i, j: (i, j, 0)),
            _const_spec((1, D_RWKV_IN)),
            vec, vec,
            _const_spec((RANK_W + RANK_A, 2 * D_RWKV)),
            _const_spec((RANK_G, D_RWKV)),
            vec, vec, vec, vec, vec,
            _const_spec(tri.shape),
            _const_spec(masks.shape),
            _const_spec(masks.shape),
        ],
        out_specs=pl.BlockSpec((RWKV_BATCH, CHUNK, D_RWKV), lambda i, j: (i, j, 0)),
        out_shape=jax.ShapeDtypeStruct((b, s, D_RWKV), F32),
        scratch_shapes=[
            pltpu.VMEM((RWKV_BATCH, V7X_SUBLANES, D_RWKV_IN), F32),
            pltpu.VMEM((RWKV_BATCH, N_RWKV_GROUPS, GROUP_LANES, GROUP_LANES), F32),
        ],
        compiler_params=pltpu.CompilerParams(
            dimension_semantics=("parallel", "arbitrary"), vmem_limit_bytes=V7X_VMEM_LIMIT_BYTES),
        name="rwkv7",
    )(pa, mu, w0, a0, wwa, gup, k_k, k_a, r_k, ln_w, ln_b, tri, masks, masks.astype(BF16))


def _hgrn_sum_matrix():
    t = np.arange(CHUNK)[:, None]
    j = np.arange(CHUNK)[None, :]
    mats = [(j <= t), (j > t)]
    for lvl in range(N_LEVELS):
        s = 1 << lvl
        mid = (t // (2 * s)) * (2 * s) + s - 1
        mats.append(((j > mid) & (j <= t)).astype(np.float32) - ((j > t) & (j <= mid)).astype(np.float32))
    return np.concatenate([np.asarray(m, np.float32) for m in mats], axis=0)


def _hgrn_level_masks():
    t = np.arange(CHUNK)[:, None]
    s_ = np.arange(CHUNK)[None, :]
    masks = [t == s_]
    for lvl in range(N_LEVELS):
        s = 1 << lvl
        masks.append(((t // (2 * s)) == (s_ // (2 * s))) & ((t % (2 * s)) >= s) & ((s_ % (2 * s)) < s))
    return np.stack(masks).astype(np.float32)


def _hgrn_kernel(p_ref, cw_ref, lbl_ref, ng_ref, sm_ref, lm_ref, o_ref, prev_ref, st_ref):
    c = pl.program_id(1)

    @pl.when(c == 0)
    def _():
        prev_ref[...] = jnp.zeros_like(prev_ref)
        st_ref[...] = jnp.zeros_like(st_ref)

    l0 = lbl_ref[0:1, :]
    l1 = lbl_ref[1:2, :]
    lmax = jnp.maximum(l0, l1)
    e0 = jnp.exp(l0 - lmax)
    lb = e0 / (e0 + jnp.exp(l1 - lmax))
    ta = jnp.log(lb)

    q_lv, k_lv, q_inter, k_inter, w_chunk, val_bf = {}, {}, {}, {}, {}, {}
    for bi in range(HGRN_BATCH):
        x = p_ref[bi, :, :3 * D_HGRN]
        xe = jnp.concatenate([prev_ref[bi], x], axis=0)
        prev_ref[bi] = x[CHUNK - V7X_SUBLANES:, :]
        conv = x * cw_ref[CONV_W - 1:CONV_W, :]
        for s in range(1, CONV_W):
            conv = conv + pltpu.roll(xe, s, 0)[V7X_SUBLANES:, :] * cw_ref[CONV_W - 1 - s:CONV_W - s, :]

        qc = conv[:, :D_HGRN]
        f = conv[:, D_HGRN:2 * D_HGRN]
        val_bf[bi] = conv[:, 2 * D_HGRN:].astype(BF16)
        q = qc * _sigmoid(qc)
        tb = jnp.log1p(-lb) + _log_sigmoid(f)
        log_f = jnp.maximum(ta, tb) + _log1p_exp_neg_abs(ta - tb)
        kg = (1.0 - lb) * _sigmoid(-f)

        sums = _dot01_left(sm_ref[...], log_f)
        b_in = sums[0:CHUNK]
        q_inter[bi] = (q * jnp.exp(b_in)).astype(BF16)
        k_inter[bi] = (kg * jnp.exp(sums[CHUNK:2 * CHUNK])).astype(BF16)
        w_chunk[bi] = jnp.exp(b_in[CHUNK - 1:CHUNK, :])
        q_lv[bi] = [q.astype(BF16)]
        k_lv[bi] = [kg.astype(BF16)]
        for lvl in range(N_LEVELS):
            e = jnp.exp(-jnp.abs(sums[(2 + lvl) * CHUNK:(3 + lvl) * CHUNK])).astype(BF16)
            q_lv[bi].append(q_lv[bi][0] * e)
            k_lv[bi].append(k_lv[bi][0] * e)

    chains = [(bi, h) for bi in range(HGRN_BATCH) for h in range(HGRN_HEADS)]
    sl_of = lambda h: slice(h * HGRN_HEAD, (h + 1) * HGRN_HEAD)
    scores = {}
    for lvl in range(N_LEVELS + 1):
        for ch in chains:
            bi, sl = ch[0], sl_of(ch[1])
            term = _dot_nt(q_lv[bi][lvl][:, sl], k_lv[bi][lvl][:, sl]) * lm_ref[lvl]
            scores[ch] = term if lvl == 0 else scores[ch] + term
    outs = {}
    for ch in chains:
        bi, h = ch
        sl = sl_of(h)
        st = st_ref[bi, h]
        o = _dot(scores[ch].astype(BF16), val_bf[bi][:, sl]) + _dot_nt(q_inter[bi][:, sl], st.astype(BF16))
        st_ref[bi, h] = st * w_chunk[bi][:, sl] + _dot_tn(val_bf[bi][:, sl], k_inter[bi][:, sl])
        outs[ch] = o * lax.rsqrt(jnp.mean(o * o, axis=-1, keepdims=True) + RMS_EPS)
    for bi in range(HGRN_BATCH):
        gate = p_ref[bi, :, 3 * D_HGRN:]
        o = jnp.concatenate([outs[(bi, h)] for h in range(HGRN_HEADS)], axis=1)
        o_ref[bi] = o * ng_ref[...] * (gate * _sigmoid(gate))


def _hgrn(pb, conv_w, lb_logits, norm_g):
    b, s, _ = pb.shape
    sum_m = jnp.asarray(_hgrn_sum_matrix(), BF16)
    lvl_m = jnp.asarray(_hgrn_level_masks())
    return pl.pallas_call(
        _hgrn_kernel,
        grid=(b // HGRN_BATCH, s // CHUNK),
        in_specs=[
            pl.BlockSpec((HGRN_BATCH, CHUNK, D_HGRN_IN), lambda i, j: (i, j, 0)),
            _const_spec((CONV_W, 3 * D_HGRN)),
            _const_spec(lb_logits.shape),
            _const_spec((1, D_HGRN)),
            _const_spec(sum_m.shape),
            _const_spec(lvl_m.shape),
        ],
        out_specs=pl.BlockSpec((HGRN_BATCH, CHUNK, D_HGRN), lambda i, j: (i, j, 0)),
        out_shape=jax.ShapeDtypeStruct((b, s, D_HGRN), F32),
        scratch_shapes=[
            pltpu.VMEM((HGRN_BATCH, V7X_SUBLANES, 3 * D_HGRN), F32),
            pltpu.VMEM((HGRN_BATCH, HGRN_HEADS, HGRN_HEAD, HGRN_HEAD), F32),
        ],
        compiler_params=pltpu.CompilerParams(
            dimension_semantics=("parallel", "arbitrary"), vmem_limit_bytes=V7X_VMEM_LIMIT_BYTES),
        name="hgrn2",
    )(pb, conv_w, lb_logits, norm_g, sum_m, lvl_m)


ROUTER_LANES = V7X_LANES


_R_E1, _R_E2, _R_G1, _R_G2, _R_RANK1, _R_RANK2 = 0, 1, 2, 3, 4, 5


def _outproj_kernel(x_ref, ya_ref, yb_ref, wo_ref, g2_ref, wr_ref, br_ref, tri_ref,
                    h_ref, n_ref, meta_ref, counts_ref, carry_ref):
    @pl.when(pl.program_id(0) == 0)
    def _():
        carry_ref[...] = jnp.zeros_like(carry_ref)

    h = (x_ref[...]
         + _dot(ya_ref[...].astype(BF16), wo_ref[:D_RWKV, :])
         + _dot(yb_ref[...].astype(BF16), wo_ref[D_RWKV:, :]))
    h_ref[...] = h
    n = _rms(h, g2_ref[...])
    n_ref[...] = n

    n_hi, n_lo = _split_bf16(n)
    logits = (_dot(n_hi, wr_ref[0]) + _dot(n_lo, wr_ref[0]) + _dot(n_hi, wr_ref[1])) + br_ref[...]
    lane = lax.broadcasted_iota(jnp.int32, logits.shape, 1).astype(F32)
    neg = -jnp.inf
    gl = jnp.where((lane >= N_EXPERTS) & (lane < N_EXPERTS + N_GROUPS), logits, neg)
    gmax = jnp.max(gl, axis=-1, keepdims=True)
    gidx = jnp.min(jnp.where(gl == gmax, lane - N_EXPERTS, ROUTER_LANES), axis=-1, keepdims=True)
    g_w = 1.0 / jnp.sum(jnp.exp(gl - gmax), axis=-1, keepdims=True)
    lo = gidx * EXPERTS_PER_GROUP
    el = jnp.where((lane >= lo) & (lane < lo + EXPERTS_PER_GROUP), logits, neg)
    m1 = jnp.max(el, axis=-1, keepdims=True)
    i1 = jnp.min(jnp.where(el == m1, lane, ROUTER_LANES), axis=-1, keepdims=True)
    el2 = jnp.where(lane == i1, neg, el)
    m2 = jnp.max(el2, axis=-1, keepdims=True)
    i2 = jnp.min(jnp.where(el2 == m2, lane, ROUTER_LANES), axis=-1, keepdims=True)
    t = jnp.exp(m2 - m1)
    w1 = 1.0 / (1.0 + t)

    onehot1 = lane == i1
    onehot2 = lane == i2
    assigned = jnp.where(onehot1 | onehot2, 1.0, 0.0)
    before = _dot(tri_ref[...], assigned.astype(BF16)) + carry_ref[...]
    rank1 = jnp.sum(jnp.where(onehot1, before, 0.0), axis=-1, keepdims=True)
    rank2 = jnp.sum(jnp.where(onehot2, before, 0.0), axis=-1, keepdims=True)
    carry_ref[...] += jnp.sum(assigned, axis=0, keepdims=True)
    counts_ref[...] = jnp.broadcast_to(carry_ref[...], counts_ref.shape)

    meta = jnp.zeros_like(logits)
    for idx, val in ((_R_E1, i1), (_R_E2, i2), (_R_G1, g_w * w1), (_R_G2, g_w * (t * w1)),
                     (_R_RANK1, rank1), (_R_RANK2, rank2)):
        meta = jnp.where(lane == idx, val, meta)
    meta_ref[...] = meta


def _outproj(x2d, ya, yb, wo_bf16, g2, wr, br):
    t = x2d.shape[0]
    row = lambda w: pl.BlockSpec((TM_PROJ, w), lambda i: (i, 0))
    tri = jnp.asarray(np.tril(np.ones((TM_PROJ, TM_PROJ), np.float32), -1), BF16)
    return pl.pallas_call(
        _outproj_kernel,
        grid=(t // TM_PROJ,),
        in_specs=[
            row(D_MODEL), row(D_RWKV), row(D_HGRN),
            _const_spec((D_RWKV + D_HGRN, D_MODEL)),
            _const_spec((1, D_MODEL)),
            _const_spec((2, D_MODEL, ROUTER_LANES)),
            _const_spec((1, ROUTER_LANES)),
            _const_spec((TM_PROJ, TM_PROJ)),
        ],
        out_specs=[row(D_MODEL), row(D_MODEL), row(ROUTER_LANES), _const_spec((V7X_SUBLANES, ROUTER_LANES))],
        out_shape=[
            jax.ShapeDtypeStruct((t, D_MODEL), F32),
            jax.ShapeDtypeStruct((t, D_MODEL), F32),
            jax.ShapeDtypeStruct((t, ROUTER_LANES), F32),
            jax.ShapeDtypeStruct((V7X_SUBLANES, ROUTER_LANES), F32),
        ],
        scratch_shapes=[pltpu.VMEM((1, ROUTER_LANES), F32)],
        compiler_params=pltpu.CompilerParams(
            dimension_semantics=("arbitrary",), vmem_limit_bytes=V7X_VMEM_LIMIT_BYTES),
        name="outproj_router",
    )(x2d, ya, yb, wo_bf16, g2, wr, br, tri)


def _slot_kernel(meta_ref, offs_ref, pos_ref):
    meta = meta_ref[...]
    lane = lax.broadcasted_iota(jnp.int32, meta.shape, 1).astype(F32)
    offs = offs_ref[...]

    def slot(e_lane, rank_lane):
        e = meta[:, e_lane:e_lane + 1]
        return jnp.sum(jnp.where(lane == e, offs, 0.0), axis=-1, keepdims=True) + meta[:, rank_lane:rank_lane + 1]

    pos = jnp.where(lane == 0, slot(_R_E1, _R_RANK1), jnp.where(lane == 1, slot(_R_E2, _R_RANK2), 0.0))
    pos_ref[...] = pos.astype(jnp.int32)


def _slots(meta, offs):
    t = meta.shape[0]
    return pl.pallas_call(
        _slot_kernel,
        grid=(t // TM_PROJ,),
        in_specs=[pl.BlockSpec((TM_PROJ, ROUTER_LANES), lambda i: (i, 0)), _const_spec((1, ROUTER_LANES))],
        out_specs=pl.BlockSpec((TM_PROJ, ROUTER_LANES), lambda i: (i, 0)),
        out_shape=jax.ShapeDtypeStruct((t, ROUTER_LANES), jnp.int32),
        compiler_params=pltpu.CompilerParams(dimension_semantics=("parallel",)),
        name="moe_slots",
    )(meta, offs)


def _row_copy(src_ref, src_row, dst_ref, dst_row, sem):
    return pltpu.make_async_copy(src_ref.at[pl.ds(src_row, 1)], dst_ref.at[pl.ds(dst_row, 1)], sem)


def _dispatch_kernel(pos_ref, n_ref, xs_in_ref, xs_ref, sem):
    del xs_in_ref

    def body(r, carry):
        _row_copy(n_ref, r, xs_ref, pos_ref[2 * r], sem).start()
        _row_copy(n_ref, r, xs_ref, pos_ref[2 * r + 1], sem).start()
        return carry

    lax.fori_loop(0, TM_DISPATCH, body, 0, unroll=8)
    for _ in range(2):
        pltpu.make_async_copy(n_ref, xs_ref.at[pl.ds(0, TM_DISPATCH)], sem).wait()


def _dispatch(pos_flat, n2, n_slots):
    t = n2.shape[0]
    xs0 = jnp.zeros((n_slots, D_MODEL), F32)
    return pl.pallas_call(
        _dispatch_kernel,
        grid=(t // TM_DISPATCH,),
        in_specs=[
            pl.BlockSpec((2 * TM_DISPATCH,), lambda i: (i,), memory_space=pltpu.SMEM),
            pl.BlockSpec((TM_DISPATCH, D_MODEL), lambda i: (i, 0)),
            pl.BlockSpec(memory_space=pl.ANY),
        ],
        out_specs=pl.BlockSpec(memory_space=pl.ANY),
        out_shape=jax.ShapeDtypeStruct((n_slots, D_MODEL), F32),
        scratch_shapes=[pltpu.SemaphoreType.DMA(())],
        input_output_aliases={2: 0},
        compiler_params=pltpu.CompilerParams(
            dimension_semantics=("arbitrary",), has_side_effects=True),
        name="moe_dispatch",
    )(pos_flat, n2, xs0)


def _gmm_kernel(be_ref, nu_ref, xs_ref, wg_ref, wu_ref, wd_ref, y_ref, wgu_s, wd_s):
    b = pl.program_id(0)

    @pl.when(b < nu_ref[0])
    def _():
        @pl.when((b == 0) | (be_ref[b] != be_ref[jnp.maximum(b - 1, 0)]))
        def _():
            wgu_s[:, :D_EXPERT] = wg_ref[...].astype(BF16)
            wgu_s[:, D_EXPERT:] = wu_ref[...].astype(BF16)
            wd_s[...] = wd_ref[...].astype(BF16)

        gu = _dot(xs_ref[...].astype(BF16), wgu_s[...])
        hg = gu[:, :D_EXPERT]
        hid = (hg * _sigmoid(hg) * gu[:, D_EXPERT:]).astype(BF16)
        y_ref[...] = _dot(hid, wd_s[...])

    @pl.when(b >= nu_ref[0])
    def _():
        y_ref[...] = jnp.zeros_like(y_ref)


def _gmm(block_expert, n_used, xs, wg, wu, wd):
    n_slots = xs.shape[0]
    last = lambda b, nu: jnp.maximum(jnp.minimum(b, nu[0] - 1), 0)
    blk = lambda b, be, nu: (last(b, nu), 0)
    wsel = lambda b, be, nu: (be[last(b, nu)], 0, 0)
    return pl.pallas_call(
        _gmm_kernel,
        grid_spec=pltpu.PrefetchScalarGridSpec(
            num_scalar_prefetch=2,
            grid=(n_slots // TB_MOE,),
            in_specs=[
                pl.BlockSpec((TB_MOE, D_MODEL), blk),
                pl.BlockSpec((None, D_MODEL, D_EXPERT), wsel),
                pl.BlockSpec((None, D_MODEL, D_EXPERT), wsel),
                pl.BlockSpec((None, D_EXPERT, D_MODEL), wsel),
            ],
            out_specs=pl.BlockSpec((TB_MOE, D_MODEL), lambda b, be, nu: (b, 0)),
            scratch_shapes=[
                pltpu.VMEM((D_MODEL, 2 * D_EXPERT), BF16),
                pltpu.VMEM((D_EXPERT, D_MODEL), BF16),
            ],
        ),
        out_shape=jax.ShapeDtypeStruct((n_slots, D_MODEL), F32),
        compiler_params=pltpu.CompilerParams(
            dimension_semantics=("arbitrary",), vmem_limit_bytes=V7X_VMEM_LIMIT_BYTES),
        name="moe_gmm",
    )(block_expert, n_used, xs, wg, wu, wd)


def _combine_kernel(pos_ref, pos_next_ref, meta_ref, h_ref, gf_ref, ys_ref, o_ref, buf_ref, sem):
    i = pl.program_id(0)
    slot = lax.rem(i, 2)

    def issue(p_ref, s):
        def body(r, carry):
            _row_copy(ys_ref, p_ref[2 * r], buf_ref.at[s, 0], r, sem.at[s]).start()
            _row_copy(ys_ref, p_ref[2 * r + 1], buf_ref.at[s, 1], r, sem.at[s]).start()
            return carry
        lax.fori_loop(0, TM_DISPATCH, body, 0, unroll=8)

    @pl.when(i == 0)
    def _():
        issue(pos_ref, 0)

    @pl.when(i + 1 < pl.num_programs(0))
    def _():
        issue(pos_next_ref, 1 - slot)

    for j in range(2):
        pltpu.make_async_copy(ys_ref.at[pl.ds(0, TM_DISPATCH)], buf_ref.at[slot, j], sem.at[slot]).wait()

    meta = meta_ref[...]
    g1 = meta[:, _R_G1:_R_G1 + 1]
    g2 = meta[:, _R_G2:_R_G2 + 1]
    moe = g1 * buf_ref[slot, 0] + g2 * buf_ref[slot, 1]
    o_ref[...] = _rms(h_ref[...] + moe, gf_ref[...])


def _combine(pos_flat, meta, h, gf, ys):
    t = h.shape[0]
    n_tiles = t // TM_DISPATCH
    return pl.pallas_call(
        _combine_kernel,
        grid=(n_tiles,),
        in_specs=[
            pl.BlockSpec((2 * TM_DISPATCH,), lambda i: (i,), memory_space=pltpu.SMEM),
            pl.BlockSpec((2 * TM_DISPATCH,), lambda i: (jnp.minimum(i + 1, n_tiles - 1),),
                         memory_space=pltpu.SMEM),
            pl.BlockSpec((TM_DISPATCH, ROUTER_LANES), lambda i: (i, 0)),
            pl.BlockSpec((TM_DISPATCH, D_MODEL), lambda i: (i, 0)),
            _const_spec((1, D_MODEL)),
            pl.BlockSpec(memory_space=pl.ANY),
        ],
        out_specs=pl.BlockSpec((TM_DISPATCH, D_MODEL), lambda i: (i, 0)),
        out_shape=jax.ShapeDtypeStruct((t, D_MODEL), F32),
        scratch_shapes=[
            pltpu.VMEM((2, 2, TM_DISPATCH, D_MODEL), F32),
            pltpu.SemaphoreType.DMA((2,)),
        ],
        compiler_params=pltpu.CompilerParams(
            dimension_semantics=("arbitrary",), vmem_limit_bytes=V7X_VMEM_LIMIT_BYTES),
        name="moe_combine",
    )(pos_flat, pos_flat, meta, h, gf, ys)


def _moe(n2, meta, counts, h, wg, wu, wd, gf):
    t = n2.shape[0]
    n_blocks = (2 * t + N_EXPERTS * (TB_MOE - 1) + TB_MOE - 1) // TB_MOE
    cnt = counts[0, :N_EXPERTS].astype(jnp.int32)
    blocks = (cnt + TB_MOE - 1) // TB_MOE
    ends = jnp.cumsum(blocks)
    offs = ((ends - blocks) * TB_MOE).astype(F32)
    offs_row = jnp.zeros((1, ROUTER_LANES), F32).at[0, :N_EXPERTS].set(offs)
    block_ids = jnp.arange(n_blocks, dtype=jnp.int32)
    block_expert = jnp.minimum(
        jnp.sum((ends[None, :] <= block_ids[:, None]).astype(jnp.int32), axis=1), N_EXPERTS - 1)
    n_used = ends[-1:].astype(jnp.int32)

    pos = _slots(meta, offs_row)
    pos_flat = pos[:, :2].reshape(-1)
    xs = _dispatch(pos_flat, n2, n_blocks * TB_MOE)
    ys = _gmm(block_expert, n_used, xs, wg, wu, wd)
    return _combine(pos_flat, meta, h, gf, ys)


def kernel(x, norm1_g, w_in, rwkv_mu, rwkv_w0, rwkv_w_up, rwkv_a0, rwkv_a_up, rwkv_g_up, rwkv_k_k,
           rwkv_k_a, rwkv_r_k, rwkv_ln_w, rwkv_ln_b, hgrn_conv_w, hgrn_lb_logits, hgrn_norm_g, w_out,
           norm2_g, router_g_w, router_g_b, router_e_w, router_e_b, exp_w_gate, exp_w_up, exp_w_down,
           final_norm_g):
    b, s, d = x.shape
    t = b * s
    l = 0
    x2d = x.reshape(t, d)
    row = lambda a: a.reshape(1, -1)

    pa, pb = _inproj(x2d, row(norm1_g[l]), w_in[l].astype(BF16))

    wwa = jnp.zeros((RANK_W + RANK_A, 2 * D_RWKV), F32)
    wwa = wwa.at[:RANK_W, :D_RWKV].set(rwkv_w_up[l]).at[RANK_W:, D_RWKV:].set(rwkv_a_up[l]).astype(BF16)
    ya = _rwkv(pa.reshape(b, s, D_RWKV_IN), row(rwkv_mu[l]), row(rwkv_w0[l]), row(rwkv_a0[l]), wwa,
               rwkv_g_up[l].astype(BF16), row(rwkv_k_k[l]), row(rwkv_k_a[l]), row(rwkv_r_k[l]),
               row(rwkv_ln_w[l]), row(rwkv_ln_b[l]))
    yb = _hgrn(pb.reshape(b, s, D_HGRN_IN), hgrn_conv_w[l], hgrn_lb_logits, row(hgrn_norm_g[l]))

    wr = jnp.zeros((D_MODEL, ROUTER_LANES), F32)
    wr = wr.at[:, :N_EXPERTS].set(router_e_w[l]).at[:, N_EXPERTS:N_EXPERTS + N_GROUPS].set(router_g_w[l])
    br = jnp.zeros((1, ROUTER_LANES), F32)
    br = br.at[0, :N_EXPERTS].set(router_e_b[l]).at[0, N_EXPERTS:N_EXPERTS + N_GROUPS].set(router_g_b[l])
    wr_hi = wr.astype(BF16)
    wr_split = jnp.stack([wr_hi, (wr - wr_hi.astype(F32)).astype(BF16)])
    h, n2, meta, counts = _outproj(x2d, ya.reshape(t, D_RWKV), yb.reshape(t, D_HGRN),
                                   w_out[l].astype(BF16), row(norm2_g[l]), wr_split, br)

    out = _moe(n2, meta, counts, h, exp_w_gate[l], exp_w_up[l], exp_w_down[l], row(final_norm_g))
    return out.reshape(b, s, d)
```

```python
import functools

import numpy as np
import jax
import jax.numpy as jnp
from jax import lax
from jax.experimental import pallas as pl
from jax.experimental.pallas import tpu as pltpu

F32 = jnp.float32
BF16 = jnp.bfloat16

D_MODEL = 1024
D_RWKV = 512
D_HGRN = 512
RWKV_HEAD = 64
RANK_W = 64
RANK_A = 64
RANK_G = 128
HGRN_HEAD = 128
HGRN_HEADS = D_HGRN // HGRN_HEAD
CONV_W = 4
N_GROUPS = 4
EXPERTS_PER_GROUP = 8
N_EXPERTS = N_GROUPS * EXPERTS_PER_GROUP
D_EXPERT = 512
RMS_EPS = 1e-6
GN_EPS = 64e-5
L2_EPS = 1e-12
D_RWKV_IN = 3 * D_RWKV + RANK_W + RANK_A + RANK_G
D_HGRN_IN = 4 * D_HGRN

V7X_LANES = 128
V7X_SUBLANES = 8
V7X_MXU_DIM = 256
V7X_VMEM_LIMIT_BYTES = 56 * 1024 * 1024

CHUNK = 64
GROUP_LANES = V7X_MXU_DIM
HEADS_PER_GROUP = GROUP_LANES // RWKV_HEAD
N_RWKV_GROUPS = D_RWKV // GROUP_LANES
N_LEVELS = 6
HGRN_BATCH = 2
RWKV_BATCH = 2
TM_PROJ = 512
TM_DISPATCH = 512
TB_MOE = 512


def _dot(a, b):
    return jnp.dot(a, b, preferred_element_type=F32)


def _dot_nt(a, b):
    return lax.dot_general(a, b, (((1,), (1,)), ((), ())), preferred_element_type=F32)


def _dot_tn(a, b):
    return lax.dot_general(a, b, (((0,), (0,)), ((), ())), preferred_element_type=F32)


def _split_bf16(x):
    hi = x.astype(BF16)
    lo = (x - hi.astype(F32)).astype(BF16)
    return hi, lo


def _dot01_left(m01, x):
    hi, lo = _split_bf16(x)
    return _dot(m01, hi) + _dot(m01, lo)


def _log1p_exp_neg_abs(x):
    return jnp.log1p(jnp.exp(-jnp.abs(x)))


def _log_sigmoid(x):
    return jnp.minimum(x, 0.0) - _log1p_exp_neg_abs(x)


def _sigmoid(x):
    return 1.0 / (1.0 + jnp.exp(-x))


def _rms(x, gain):
    return x * lax.rsqrt(jnp.mean(x * x, axis=-1, keepdims=True) + RMS_EPS) * gain


def _const_spec(shape):
    nd = len(shape)
    return pl.BlockSpec(shape, lambda *_: (0,) * nd)


def _inproj_kernel(x_ref, g_ref, w_ref, pa_ref, pb_ref):
    n = _rms(x_ref[...], g_ref[...]).astype(BF16)
    pa_ref[...] = _dot(n, w_ref[:, :D_RWKV_IN])
    pb_ref[...] = _dot(n, w_ref[:, D_RWKV_IN:])


def _inproj(x2d, gain, w_bf16):
    t = x2d.shape[0]
    return pl.pallas_call(
        _inproj_kernel,
        grid=(t // TM_PROJ,),
        in_specs=[
            pl.BlockSpec((TM_PROJ, D_MODEL), lambda i: (i, 0)),
            _const_spec((1, D_MODEL)),
            _const_spec((D_MODEL, D_RWKV_IN + D_HGRN_IN)),
        ],
        out_specs=[
            pl.BlockSpec((TM_PROJ, D_RWKV_IN), lambda i: (i, 0)),
            pl.BlockSpec((TM_PROJ, D_HGRN_IN), lambda i: (i, 0)),
        ],
        out_shape=[
            jax.ShapeDtypeStruct((t, D_RWKV_IN), F32),
            jax.ShapeDtypeStruct((t, D_HGRN_IN), F32),
        ],
        compiler_params=pltpu.CompilerParams(
            dimension_semantics=("parallel",), vmem_limit_bytes=V7X_VMEM_LIMIT_BYTES),
        name="inproj",
    )(x2d, gain, w_bf16)


_M_EYE, _M_BD, _M_STRICT, _M_INCL, _M_OFF0 = 0, 1, 2, 3, 4


def _rwkv_masks():
    i = np.arange(GROUP_LANES)[:, None]
    j = np.arange(GROUP_LANES)[None, :]
    bd = (i // CHUNK) == (j // CHUNK)
    masks = [i == j, bd, (j % CHUNK) < (i % CHUNK), (j % CHUNK) <= (i % CHUNK)]
    for lvl in range(N_LEVELS):
        s = 1 << lvl
        masks.append(((i // (2 * s)) == (j // (2 * s))) & ((i % (2 * s)) >= s) & ((j % (2 * s)) < s))
    return np.stack(masks).astype(np.float32)


def _tile_rows(x, n):
    return jnp.concatenate([x] * n, axis=0)


def _rwkv_kernel(p_ref, mu_ref, w0_ref, a0_ref, wwa_ref, gup_ref, kk_ref, ka_ref, rk_ref, lnw_ref,
                 lnb_ref, tri_ref, m_ref, mb_ref, o_ref, prev_ref, st_ref):
    c = pl.program_id(1)

    @pl.when(c == 0)
    def _():
        prev_ref[...] = jnp.zeros_like(prev_ref)
        st_ref[...] = jnp.zeros_like(st_ref)

    rows = RWKV_BATCH * CHUNK
    p = p_ref[...].reshape(rows, D_RWKV_IN)
    row = lax.broadcasted_iota(jnp.int32, p.shape, 0)
    shifted = pltpu.roll(p, 1, 0)
    for bi in range(RWKV_BATCH):
        shifted = jnp.where(row == bi * CHUNK, prev_ref[bi, V7X_SUBLANES - 1:V7X_SUBLANES, :], shifted)
        prev_ref[bi] = p[(bi + 1) * CHUNK - V7X_SUBLANES:(bi + 1) * CHUNK, :]
    p = p + mu_ref[...] * (shifted - p)

    r = p[:, 0:D_RWKV]
    k = p[:, D_RWKV:2 * D_RWKV]
    v = p[:, 2 * D_RWKV:3 * D_RWKV]
    x_wa = p[:, 3 * D_RWKV:3 * D_RWKV + RANK_W + RANK_A]
    dg = p[:, 3 * D_RWKV + RANK_W + RANK_A:]

    lane = lax.broadcasted_iota(jnp.int32, x_wa.shape, 1)
    t_wa = jnp.where(lane < RANK_W, jnp.tanh(x_wa), x_wa).astype(BF16)
    wa = _dot(t_wa, wwa_ref[...])
    w_lin = w0_ref[...] + wa[:, :D_RWKV]
    w = _log_sigmoid(w_lin) - 0.5
    log_decay = -jnp.exp(w)
    a = _sigmoid(a0_ref[...] + wa[:, D_RWKV:])
    g = _dot(_sigmoid(dg).astype(BF16), gup_ref[...])

    bd_bf = mb_ref[_M_BD]

    def head_sum(x):
        hi, lo = _split_bf16(x)
        parts = [hi[:, :GROUP_LANES], hi[:, GROUP_LANES:], lo[:, :GROUP_LANES], lo[:, GROUP_LANES:]]
        s = _dot(jnp.concatenate(parts, axis=0), bd_bf)
        return jnp.concatenate([s[:rows] + s[2 * rows:3 * rows], s[rows:2 * rows] + s[3 * rows:]], axis=1)

    kk = k * kk_ref[...]
    kk = kk / jnp.maximum(jnp.sqrt(head_sum(kk * kk)), L2_EPS)
    k = k * (1.0 + (a - 1.0) * ka_ref[...])
    kb = kk * a

    sums = _dot01_left(tri_ref[...], log_decay)
    cum = sums[:rows]
    e_in = jnp.exp(cum)
    e_ex = jnp.exp(cum - log_decay)
    e_inv = jnp.exp(-cum)
    e_rem = jnp.exp(sums[rows:])

    r_hat = (r * e_in).astype(BF16)
    kk_hat = (kk * e_ex).astype(BF16)
    k_inv = (k * e_inv).astype(BF16)
    b_inv = (kb * e_inv).astype(BF16)
    k_rem = (k * e_rem).astype(BF16)
    b_rem = (kb * e_rem).astype(BF16)
    v_bf = v.astype(BF16)

    eye = m_ref[_M_EYE]
    bd = m_ref[_M_BD]
    strict_lc = m_ref[_M_STRICT][:CHUNK]
    incl_lc = m_ref[_M_INCL][:CHUNK]

    def block_diag(x_bf):
        return _tile_rows(x_bf, HEADS_PER_GROUP) * bd_bf

    chains = [(bi, gi) for bi in range(RWKV_BATCH) for gi in range(N_RWKV_GROUPS)]
    rs_of = lambda bi: slice(bi * CHUNK, (bi + 1) * CHUNK)
    sl_of = lambda gi: slice(gi * GROUP_LANES, (gi + 1) * GROUP_LANES)

    lhs, a_kk, a_rk, a_rb, a_kb_bd, inv = {}, {}, {}, {}, {}, {}
    for ch in chains:
        rs, sl = rs_of(ch[0]), sl_of(ch[1])
        lhs[ch] = jnp.concatenate([kk_hat[rs, sl], r_hat[rs, sl]], axis=0)
        a_k = _dot_nt(lhs[ch], block_diag(k_inv[rs, sl]))
        a_b = _dot_nt(lhs[ch], block_diag(b_inv[rs, sl]))
        a_kk[ch] = (a_k[:CHUNK] * strict_lc).astype(BF16)
        a_rk[ch] = (a_k[CHUNK:] * incl_lc).astype(BF16)
        a_rb[ch] = (a_b[CHUNK:] * incl_lc).astype(BF16)
        a_kb_bd[ch] = block_diag((a_b[:CHUNK] * strict_lc).astype(BF16))
        inv[ch] = eye - (a_kb_bd[ch] * mb_ref[_M_OFF0]).astype(F32)

    for lvl in range(1, N_LEVELS):
        inv_bf = {ch: inv[ch].astype(BF16) for ch in chains}
        mid = {ch: _dot(inv_bf[ch], a_kb_bd[ch] * mb_ref[_M_OFF0 + lvl]).astype(BF16) for ch in chains}
        for ch in chains:
            inv[ch] = inv[ch] - _dot(mid[ch], inv_bf[ch])

    ys = {}
    for ch in chains:
        bi, gi = ch
        rs, sl = rs_of(bi), sl_of(gi)
        inv_lc = inv[ch][0:CHUNK]
        for h in range(1, HEADS_PER_GROUP):
            inv_lc = inv_lc + inv[ch][h * CHUNK:(h + 1) * CHUNK]
        s_bd = st_ref[bi, gi]
        v_bd = block_diag(v_bf[rs, sl])
        from_state = _dot_nt(lhs[ch], s_bd.astype(BF16))
        rhs = from_state[:CHUNK] + _dot(a_kk[ch], v_bd)
        u_bf = _dot(inv_lc.astype(BF16), block_diag(rhs.astype(BF16))).astype(BF16)
        ys[ch] = from_state[CHUNK:] + _dot(a_rk[ch], v_bd) - _dot(a_rb[ch], block_diag(u_bf))

        vu = jnp.concatenate([v_bf[rs, sl], -u_bf], axis=0)
        kb_rem = jnp.concatenate([k_rem[rs, sl], b_rem[rs, sl]], axis=0)
        w_chunk = e_in[(bi + 1) * CHUNK - 1:(bi + 1) * CHUNK, sl]
        st_ref[bi, gi] = s_bd * w_chunk + _dot_tn(vu, kb_rem) * bd

    y = jnp.concatenate(
        [jnp.concatenate([ys[(bi, gi)] for gi in range(N_RWKV_GROUPS)], axis=1) for bi in range(RWKV_BATCH)],
        axis=0)
    inv_n = 1.0 / RWKV_HEAD
    mean = head_sum(y) * inv_n
    yc = y - mean
    var = head_sum(yc * yc) * inv_n
    yn = yc * lax.rsqrt(var + GN_EPS) * lnw_ref[...] + lnb_ref[...]
    bonus = head_sum(r * k * rk_ref[...])
    o_ref[...] = ((yn + bonus * v) * g).reshape(RWKV_BATCH, CHUNK, D_RWKV)


def _rwkv(pa, mu, w0, a0, wwa, gup, k_k, k_a, r_k, ln_w, ln_b):
    b, s, _ = pa.shape
    tril = np.tril(np.ones((CHUNK, CHUNK), np.float32))
    eye_b = np.eye(RWKV_BATCH, dtype=np.float32)
    tri = jnp.asarray(np.concatenate([np.kron(eye_b, tril), np.kron(eye_b, 1.0 - tril)], axis=0), BF16)
    masks = jnp.asarray(_rwkv_masks())
    vec = _const_spec((1, D_RWKV))
    return pl.pallas_call(
        _rwkv_kernel,
        grid=(b // RWKV_BATCH, s // CHUNK),
        in_specs=[
            pl.BlockSpec((RWKV_BATCH, CHUNK, D_RWKV_IN), lambda i, j: (i, j, 0)),
            _const_spec((1, D_RWKV_IN)),
            vec, vec,
            _const_spec((RANK_W + RANK_A, 2 * D_RWKV)),
            _const_spec((RANK_G, D_RWKV)),
            vec, vec, vec, vec, vec,
            _const_spec(tri.shape),
            _const_spec(masks.shape),
            _const_spec(masks.shape),
        ],
        out_specs=pl.BlockSpec((RWKV_BATCH, CHUNK, D_RWKV), lambda i, j: (i, j, 0)),
        out_shape=jax.ShapeDtypeStruct((b, s, D_RWKV), F32),
        scratch_shapes=[
            pltpu.VMEM((RWKV_BATCH, V7X_SUBLANES, D_RWKV_IN), F32),
            pltpu.VMEM((RWKV_BATCH, N_RWKV_GROUPS, GROUP_LANES, GROUP_LANES), F32),
        ],
        compiler_params=pltpu.CompilerParams(
            dimension_semantics=("parallel", "arbitrary"), vmem_limit_bytes=V7X_VMEM_LIMIT_BYTES),
        name="rwkv7",
    )(pa, mu, w0, a0, wwa, gup, k_k, k_a, r_k, ln_w, ln_b, tri, masks, masks.astype(BF16))


def _hgrn_sum_matrix():
    t = np.arange(CHUNK)[:, None]
    j = np.arange(CHUNK)[None, :]
    mats = [(j <= t), (j > t)]
    for lvl in range(N_LEVELS):
        s = 1 << lvl
        mid = (t // (2 * s)) * (2 * s) + s - 1
        mats.append(((j > mid) & (j <= t)).astype(np.float32) - ((j > t) & (j <= mid)).astype(np.float32))
    return np.concatenate([np.asarray(m, np.float32) for m in mats], axis=0)


def _hgrn_level_masks():
    t = np.arange(CHUNK)[:, None]
    s_ = np.arange(CHUNK)[None, :]
    masks = [t == s_]
    for lvl in range(N_LEVELS):
        s = 1 << lvl
        masks.append(((t // (2 * s)) == (s_ // (2 * s))) & ((t % (2 * s)) >= s) & ((s_ % (2 * s)) < s))
    return np.stack(masks).astype(np.float32)


def _hgrn_kernel(p_ref, cw_ref, lbl_ref, ng_ref, sm_ref, lm_ref, o_ref, prev_ref, st_ref):
    c = pl.program_id(1)

    @pl.when(c == 0)
    def _():
        prev_ref[...] = jnp.zeros_like(prev_ref)
        st_ref[...] = jnp.zeros_like(st_ref)

    l0 = lbl_ref[0:1, :]
    l1 = lbl_ref[1:2, :]
    lmax = jnp.maximum(l0, l1)
    e0 = jnp.exp(l0 - lmax)
    lb = e0 / (e0 + jnp.exp(l1 - lmax))
    ta = jnp.log(lb)

    q_lv, k_lv, q_inter, k_inter, w_chunk, val_bf = {}, {}, {}, {}, {}, {}
    for bi in range(HGRN_BATCH):
        x = p_ref[bi, :, :3 * D_HGRN]
        xe = jnp.concatenate([prev_ref[bi], x], axis=0)
        prev_ref[bi] = x[CHUNK - V7X_SUBLANES:, :]
        conv = x * cw_ref[CONV_W - 1:CONV_W, :]
        for s in range(1, CONV_W):
            conv = conv + pltpu.roll(xe, s, 0)[V7X_SUBLANES:, :] * cw_ref[CONV_W - 1 - s:CONV_W - s, :]

        qc = conv[:, :D_HGRN]
        f = conv[:, D_HGRN:2 * D_HGRN]
        val_bf[bi] = conv[:, 2 * D_HGRN:].astype(BF16)
        q = qc * _sigmoid(qc)
        tb = jnp.log1p(-lb) + _log_sigmoid(f)
        log_f = jnp.maximum(ta, tb) + _log1p_exp_neg_abs(ta - tb)
        kg = (1.0 - lb) * _sigmoid(-f)

        sums = _dot01_left(sm_ref[...], log_f)
        b_in = sums[0:CHUNK]
        q_inter[bi] = (q * jnp.exp(b_in)).astype(BF16)
        k_inter[bi] = (kg * jnp.exp(sums[CHUNK:2 * CHUNK])).astype(BF16)
        w_chunk[bi] = jnp.exp(b_in[CHUNK - 1:CHUNK, :])
        q_lv[bi] = [q.astype(BF16)]
        k_lv[bi] = [kg.astype(BF16)]
        for lvl in range(N_LEVELS):
            e = jnp.exp(-jnp.abs(sums[(2 + lvl) * CHUNK:(3 + lvl) * CHUNK])).astype(BF16)
            q_lv[bi].append(q_lv[bi][0] * e)
            k_lv[bi].append(k_lv[bi][0] * e)

    chains = [(bi, h) for bi in range(HGRN_BATCH) for h in range(HGRN_HEADS)]
    sl_of = lambda h: slice(h * HGRN_HEAD, (h + 1) * HGRN_HEAD)
    scores = {}
    for lvl in range(N_LEVELS + 1):
        for ch in chains:
            bi, sl = ch[0], sl_of(ch[1])
            term = _dot_nt(q_lv[bi][lvl][:, sl], k_lv[bi][lvl][:, sl]) * lm_ref[lvl]
            scores[ch] = term if lvl == 0 else scores[ch] + term
    outs = {}
    for ch in chains:
        bi, h = ch
        sl = sl_of(h)
        st = st_ref[bi, h]
        o = _dot(scores[ch].astype(BF16), val_bf[bi][:, sl]) + _dot_nt(q_inter[bi][:, sl], st.astype(BF16))
        st_ref[bi, h] = st * w_chunk[bi][:, sl] + _dot_tn(val_bf[bi][:, sl], k_inter[bi][:, sl])
        outs[ch] = o * lax.rsqrt(jnp.mean(o * o, axis=-1, keepdims=True) + RMS_EPS)
    for bi in range(HGRN_BATCH):
        gate = p_ref[bi, :, 3 * D_HGRN:]
        o = jnp.concatenate([outs[(bi, h)] for h in range(HGRN_HEADS)], axis=1)
        o_ref[bi] = o * ng_ref[...] * (gate * _sigmoid(gate))


def _hgrn(pb, conv_w, lb_logits, norm_g):
    b, s, _ = pb.shape
    sum_m = jnp.asarray(_hgrn_sum_matrix(), BF16)
    lvl_m = jnp.asarray(_hgrn_level_masks())
    return pl.pallas_call(
        _hgrn_kernel,
        grid=(b // HGRN_BATCH, s // CHUNK),
        in_specs=[
            pl.BlockSpec((HGRN_BATCH, CHUNK, D_HGRN_IN), lambda i, j: (i, j, 0)),
            _const_spec((CONV_W, 3 * D_HGRN)),
            _const_spec(lb_logits.shape),
            _const_spec((1, D_HGRN)),
            _const_spec(sum_m.shape),
            _const_spec(lvl_m.shape),
        ],
        out_specs=pl.BlockSpec((HGRN_BATCH, CHUNK, D_HGRN), lambda i, j: (i, j, 0)),
        out_shape=jax.ShapeDtypeStruct((b, s, D_HGRN), F32),
        scratch_shapes=[
            pltpu.VMEM((HGRN_BATCH, V7X_SUBLANES, 3 * D_HGRN), F32),
            pltpu.VMEM((HGRN_BATCH, HGRN_HEADS, HGRN_HEAD, HGRN_HEAD), F32),
        ],
        compiler_params=pltpu.CompilerParams(
            dimension_semantics=("parallel", "arbitrary"), vmem_limit_bytes=V7X_VMEM_LIMIT_BYTES),
        name="hgrn2",
    )(pb, conv_w, lb_logits, norm_g, sum_m, lvl_m)


ROUTER_LANES = V7X_LANES


TOKEN_ROWS = V7X_SUBLANES
assert D_MODEL == TOKEN_ROWS * V7X_LANES


def _tiles_shape(tokens):
    return (tokens * TOKEN_ROWS, V7X_LANES)


def _store_token_tiles(ref, x):
    rows = x.shape[0]
    for s in range(TOKEN_ROWS):
        ref[pl.ds(s, rows, stride=TOKEN_ROWS), :] = x[:, s * V7X_LANES:(s + 1) * V7X_LANES]


def _load_token_tiles(ref):
    rows = ref.shape[0] // TOKEN_ROWS
    return jnp.concatenate([ref[pl.ds(s, rows, stride=TOKEN_ROWS), :] for s in range(TOKEN_ROWS)], axis=1)


_R_E1, _R_E2, _R_G1, _R_G2, _R_RANK1, _R_RANK2 = 0, 1, 2, 3, 4, 5


def _outproj_kernel(x_ref, ya_ref, yb_ref, wo_ref, g2_ref, wr_ref, br_ref, tri_ref,
                    h_ref, n_ref, meta_ref, counts_ref, carry_ref):
    @pl.when(pl.program_id(0) == 0)
    def _():
        carry_ref[...] = jnp.zeros_like(carry_ref)

    h = (x_ref[...]
         + _dot(ya_ref[...].astype(BF16), wo_ref[:D_RWKV, :])
         + _dot(yb_ref[...].astype(BF16), wo_ref[D_RWKV:, :]))
    h_ref[...] = h
    n = _rms(h, g2_ref[...])
    _store_token_tiles(n_ref, n)

    n_hi, n_lo = _split_bf16(n)
    logits = (_dot(n_hi, wr_ref[0]) + _dot(n_lo, wr_ref[0]) + _dot(n_hi, wr_ref[1])) + br_ref[...]
    lane = lax.broadcasted_iota(jnp.int32, logits.shape, 1).astype(F32)
    neg = -jnp.inf
    gl = jnp.where((lane >= N_EXPERTS) & (lane < N_EXPERTS + N_GROUPS), logits, neg)
    gmax = jnp.max(gl, axis=-1, keepdims=True)
    gidx = jnp.min(jnp.where(gl == gmax, lane - N_EXPERTS, ROUTER_LANES), axis=-1, keepdims=True)
    g_w = 1.0 / jnp.sum(jnp.exp(gl - gmax), axis=-1, keepdims=True)
    lo = gidx * EXPERTS_PER_GROUP
    el = jnp.where((lane >= lo) & (lane < lo + EXPERTS_PER_GROUP), logits, neg)
    m1 = jnp.max(el, axis=-1, keepdims=True)
    i1 = jnp.min(jnp.where(el == m1, lane, ROUTER_LANES), axis=-1, keepdims=True)
    el2 = jnp.where(lane == i1, neg, el)
    m2 = jnp.max(el2, axis=-1, keepdims=True)
    i2 = jnp.min(jnp.where(el2 == m2, lane, ROUTER_LANES), axis=-1, keepdims=True)
    t = jnp.exp(m2 - m1)
    w1 = 1.0 / (1.0 + t)

    onehot1 = lane == i1
    onehot2 = lane == i2
    assigned = jnp.where(onehot1 | onehot2, 1.0, 0.0)
    before = _dot(tri_ref[...], assigned.astype(BF16)) + carry_ref[...]
    rank1 = jnp.sum(jnp.where(onehot1, before, 0.0), axis=-1, keepdims=True)
    rank2 = jnp.sum(jnp.where(onehot2, before, 0.0), axis=-1, keepdims=True)
    carry_ref[...] += jnp.sum(assigned, axis=0, keepdims=True)
    counts_ref[...] = jnp.broadcast_to(carry_ref[...], counts_ref.shape)

    meta = jnp.zeros_like(logits)
    for idx, val in ((_R_E1, i1), (_R_E2, i2), (_R_G1, g_w * w1), (_R_G2, g_w * (t * w1)),
                     (_R_RANK1, rank1), (_R_RANK2, rank2)):
        meta = jnp.where(lane == idx, val, meta)
    meta_ref[...] = meta


def _outproj(x2d, ya, yb, wo_bf16, g2, wr, br):
    t = x2d.shape[0]
    row = lambda w: pl.BlockSpec((TM_PROJ, w), lambda i: (i, 0))
    tri = jnp.asarray(np.tril(np.ones((TM_PROJ, TM_PROJ), np.float32), -1), BF16)
    return pl.pallas_call(
        _outproj_kernel,
        grid=(t // TM_PROJ,),
        in_specs=[
            row(D_MODEL), row(D_RWKV), row(D_HGRN),
            _const_spec((D_RWKV + D_HGRN, D_MODEL)),
            _const_spec((1, D_MODEL)),
            _const_spec((2, D_MODEL, ROUTER_LANES)),
            _const_spec((1, ROUTER_LANES)),
            _const_spec((TM_PROJ, TM_PROJ)),
        ],
        out_specs=[row(D_MODEL), pl.BlockSpec(_tiles_shape(TM_PROJ), lambda i: (i, 0)), row(ROUTER_LANES),
                   _const_spec((V7X_SUBLANES, ROUTER_LANES))],
        out_shape=[
            jax.ShapeDtypeStruct((t, D_MODEL), F32),
            jax.ShapeDtypeStruct(_tiles_shape(t), F32),
            jax.ShapeDtypeStruct((t, ROUTER_LANES), F32),
            jax.ShapeDtypeStruct((V7X_SUBLANES, ROUTER_LANES), F32),
        ],
        scratch_shapes=[pltpu.VMEM((1, ROUTER_LANES), F32)],
        compiler_params=pltpu.CompilerParams(
            dimension_semantics=("arbitrary",), vmem_limit_bytes=V7X_VMEM_LIMIT_BYTES),
        name="outproj_router",
    )(x2d, ya, yb, wo_bf16, g2, wr, br, tri)


def _slot_kernel(meta_ref, offs_ref, pos_ref):
    meta = meta_ref[...]
    lane = lax.broadcasted_iota(jnp.int32, meta.shape, 1).astype(F32)
    offs = offs_ref[...]

    def slot(e_lane, rank_lane):
        e = meta[:, e_lane:e_lane + 1]
        return jnp.sum(jnp.where(lane == e, offs, 0.0), axis=-1, keepdims=True) + meta[:, rank_lane:rank_lane + 1]

    pos = jnp.where(lane == 0, slot(_R_E1, _R_RANK1), jnp.where(lane == 1, slot(_R_E2, _R_RANK2), 0.0))
    pos_ref[...] = pos.astype(jnp.int32)


def _slots(meta, offs):
    t = meta.shape[0]
    return pl.pallas_call(
        _slot_kernel,
        grid=(t // TM_PROJ,),
        in_specs=[pl.BlockSpec((TM_PROJ, ROUTER_LANES), lambda i: (i, 0)), _const_spec((1, ROUTER_LANES))],
        out_specs=pl.BlockSpec((TM_PROJ, ROUTER_LANES), lambda i: (i, 0)),
        out_shape=jax.ShapeDtypeStruct((t, ROUTER_LANES), jnp.int32),
        compiler_params=pltpu.CompilerParams(dimension_semantics=("parallel",)),
        name="moe_slots",
    )(meta, offs)


def _token_rows(token, count=1):
    return pl.ds(pl.multiple_of(token * TOKEN_ROWS, TOKEN_ROWS), count * TOKEN_ROWS)


def _row_copy(src_ref, src_token, dst_ref, dst_token, sem):
    return pltpu.make_async_copy(src_ref.at[_token_rows(src_token)], dst_ref.at[_token_rows(dst_token)], sem)


def _dispatch_kernel(pos_ref, n_ref, xs_in_ref, xs_ref, sem):
    del xs_in_ref

    def body(r, carry):
        _row_copy(n_ref, r, xs_ref, pos_ref[2 * r], sem).start()
        _row_copy(n_ref, r, xs_ref, pos_ref[2 * r + 1], sem).start()
        return carry

    lax.fori_loop(0, TM_DISPATCH, body, 0, unroll=8)
    for _ in range(2):
        pltpu.make_async_copy(n_ref, xs_ref.at[_token_rows(0, TM_DISPATCH)], sem).wait()


def _dispatch(pos_flat, n2, n_slots):
    t = n2.shape[0] // TOKEN_ROWS
    xs0 = jnp.zeros(_tiles_shape(n_slots), F32)
    return pl.pallas_call(
        _dispatch_kernel,
        grid=(t // TM_DISPATCH,),
        in_specs=[
            pl.BlockSpec((2 * TM_DISPATCH,), lambda i: (i,), memory_space=pltpu.SMEM),
            pl.BlockSpec(_tiles_shape(TM_DISPATCH), lambda i: (i, 0)),
            pl.BlockSpec(memory_space=pl.ANY),
        ],
        out_specs=pl.BlockSpec(memory_space=pl.ANY),
        out_shape=jax.ShapeDtypeStruct(_tiles_shape(n_slots), F32),
        scratch_shapes=[pltpu.SemaphoreType.DMA(())],
        input_output_aliases={2: 0},
        compiler_params=pltpu.CompilerParams(
            dimension_semantics=("arbitrary",), has_side_effects=True),
        name="moe_dispatch",
    )(pos_flat, n2, xs0)


def _gmm_kernel(be_ref, nu_ref, xs_ref, wg_ref, wu_ref, wd_ref, y_ref, wgu_s, wd_s):
    b = pl.program_id(0)

    @pl.when(b < nu_ref[0])
    def _():
        @pl.when((b == 0) | (be_ref[b] != be_ref[jnp.maximum(b - 1, 0)]))
        def _():
            wgu_s[:, :D_EXPERT] = wg_ref[...].astype(BF16)
            wgu_s[:, D_EXPERT:] = wu_ref[...].astype(BF16)
            wd_s[...] = wd_ref[...].astype(BF16)

        gu = _dot(_load_token_tiles(xs_ref).astype(BF16), wgu_s[...])
        hg = gu[:, :D_EXPERT]
        hid = (hg * _sigmoid(hg) * gu[:, D_EXPERT:]).astype(BF16)
        _store_token_tiles(y_ref, _dot(hid, wd_s[...]))

    @pl.when(b >= nu_ref[0])
    def _():
        y_ref[...] = jnp.zeros_like(y_ref)


def _gmm(block_expert, n_used, xs, wg, wu, wd):
    n_slots = xs.shape[0] // TOKEN_ROWS
    last = lambda b, nu: jnp.maximum(jnp.minimum(b, nu[0] - 1), 0)
    blk = lambda b, be, nu: (last(b, nu), 0)
    wsel = lambda b, be, nu: (be[last(b, nu)], 0, 0)
    return pl.pallas_call(
        _gmm_kernel,
        grid_spec=pltpu.PrefetchScalarGridSpec(
            num_scalar_prefetch=2,
            grid=(n_slots // TB_MOE,),
            in_specs=[
                pl.BlockSpec(_tiles_shape(TB_MOE), blk),
                pl.BlockSpec((None, D_MODEL, D_EXPERT), wsel),
                pl.BlockSpec((None, D_MODEL, D_EXPERT), wsel),
                pl.BlockSpec((None, D_EXPERT, D_MODEL), wsel),
            ],
            out_specs=pl.BlockSpec(_tiles_shape(TB_MOE), lambda b, be, nu: (b, 0)),
            scratch_shapes=[
                pltpu.VMEM((D_MODEL, 2 * D_EXPERT), BF16),
                pltpu.VMEM((D_EXPERT, D_MODEL), BF16),
            ],
        ),
        out_shape=jax.ShapeDtypeStruct(xs.shape, F32),
        compiler_params=pltpu.CompilerParams(
            dimension_semantics=("arbitrary",), vmem_limit_bytes=V7X_VMEM_LIMIT_BYTES),
        name="moe_gmm",
    )(block_expert, n_used, xs, wg, wu, wd)


def _combine_kernel(pos_ref, pos_next_ref, meta_ref, h_ref, gf_ref, ys_ref, o_ref, buf_ref, sem):
    i = pl.program_id(0)
    slot = lax.rem(i, 2)

    def issue(p_ref, s):
        def body(r, carry):
            _row_copy(ys_ref, p_ref[2 * r], buf_ref.at[s, 0], r, sem.at[s]).start()
            _row_copy(ys_ref, p_ref[2 * r + 1], buf_ref.at[s, 1], r, sem.at[s]).start()
            return carry
        lax.fori_loop(0, TM_DISPATCH, body, 0, unroll=8)

    @pl.when(i == 0)
    def _():
        issue(pos_ref, 0)

    @pl.when(i + 1 < pl.num_programs(0))
    def _():
        issue(pos_next_ref, 1 - slot)

    for j in range(2):
        pltpu.make_async_copy(ys_ref.at[_token_rows(0, TM_DISPATCH)], buf_ref.at[slot, j], sem.at[slot]).wait()

    meta = meta_ref[...]
    g1 = meta[:, _R_G1:_R_G1 + 1]
    g2 = meta[:, _R_G2:_R_G2 + 1]
    moe = g1 * _load_token_tiles(buf_ref.at[slot, 0]) + g2 * _load_token_tiles(buf_ref.at[slot, 1])
    o_ref[...] = _rms(h_ref[...] + moe, gf_ref[...])


def _combine(pos_flat, meta, h, gf, ys):
    t = h.shape[0]
    n_tiles = t // TM_DISPATCH
    return pl.pallas_call(
        _combine_kernel,
        grid=(n_tiles,),
        in_specs=[
            pl.BlockSpec((2 * TM_DISPATCH,), lambda i: (i,), memory_space=pltpu.SMEM),
            pl.BlockSpec((2 * TM_DISPATCH,), lambda i: (jnp.minimum(i + 1, n_tiles - 1),),
                         memory_space=pltpu.SMEM),
            pl.BlockSpec((TM_DISPATCH, ROUTER_LANES), lambda i: (i, 0)),
            pl.BlockSpec((TM_DISPATCH, D_MODEL), lambda i: (i, 0)),
            _const_spec((1, D_MODEL)),
            pl.BlockSpec(memory_space=pl.ANY),
        ],
        out_specs=pl.BlockSpec((TM_DISPATCH, D_MODEL), lambda i: (i, 0)),
        out_shape=jax.ShapeDtypeStruct((t, D_MODEL), F32),
        scratch_shapes=[
            pltpu.VMEM((2, 2) + _tiles_shape(TM_DISPATCH), F32),
            pltpu.SemaphoreType.DMA((2,)),
        ],
        compiler_params=pltpu.CompilerParams(
            dimension_semantics=("arbitrary",), vmem_limit_bytes=V7X_VMEM_LIMIT_BYTES),
        name="moe_combine",
    )(pos_flat, pos_flat, meta, h, gf, ys)


def _moe(n2, meta, counts, h, wg, wu, wd, gf):
    t = h.shape[0]
    n_blocks = (2 * t + N_EXPERTS * (TB_MOE - 1) + TB_MOE - 1) // TB_MOE
    cnt = counts[0, :N_EXPERTS].astype(jnp.int32)
    blocks = (cnt + TB_MOE - 1) // TB_MOE
    ends = jnp.cumsum(blocks)
    offs = ((ends - blocks) * TB_MOE).astype(F32)
    offs_row = jnp.zeros((1, ROUTER_LANES), F32).at[0, :N_EXPERTS].set(offs)
    block_ids = jnp.arange(n_blocks, dtype=jnp.int32)
    block_expert = jnp.minimum(
        jnp.sum((ends[None, :] <= block_ids[:, None]).astype(jnp.int32), axis=1), N_EXPERTS - 1)
    n_used = ends[-1:].astype(jnp.int32)

    pos = _slots(meta, offs_row)
    pos_flat = pos[:, :2].reshape(-1)
    xs = _dispatch(pos_flat, n2, n_blocks * TB_MOE)
    ys = _gmm(block_expert, n_used, xs, wg, wu, wd)
    return _combine(pos_flat, meta, h, gf, ys)


def kernel(x, norm1_g, w_in, rwkv_mu, rwkv_w0, rwkv_w_up, rwkv_a0, rwkv_a_up, rwkv_g_up, rwkv_k_k,
           rwkv_k_a, rwkv_r_k, rwkv_ln_w, rwkv_ln_b, hgrn_conv_w, hgrn_lb_logits, hgrn_norm_g, w_out,
           norm2_g, router_g_w, router_g_b, router_e_w, router_e_b, exp_w_gate, exp_w_up, exp_w_down,
           final_norm_g):
    b, s, d = x.shape
    t = b * s
    l = 0
    x2d = x.reshape(t, d)
    row = lambda a: a.reshape(1, -1)

    pa, pb = _inproj(x2d, row(norm1_g[l]), w_in[l].astype(BF16))

    wwa = jnp.zeros((RANK_W + RANK_A, 2 * D_RWKV), F32)
    wwa = wwa.at[:RANK_W, :D_RWKV].set(rwkv_w_up[l]).at[RANK_W:, D_RWKV:].set(rwkv_a_up[l]).astype(BF16)
    ya = _rwkv(pa.reshape(b, s, D_RWKV_IN), row(rwkv_mu[l]), row(rwkv_w0[l]), row(rwkv_a0[l]), wwa,
               rwkv_g_up[l].astype(BF16), row(rwkv_k_k[l]), row(rwkv_k_a[l]), row(rwkv_r_k[l]),
               row(rwkv_ln_w[l]), row(rwkv_ln_b[l]))
    yb = _hgrn(pb.reshape(b, s, D_HGRN_IN), hgrn_conv_w[l], hgrn_lb_logits, row(hgrn_norm_g[l]))

    wr = jnp.zeros((D_MODEL, ROUTER_LANES), F32)
    wr = wr.at[:, :N_EXPERTS].set(router_e_w[l]).at[:, N_EXPERTS:N_EXPERTS + N_GROUPS].set(router_g_w[l])
    br = jnp.zeros((1, ROUTER_LANES), F32)
    br = br.at[0, :N_EXPERTS].set(router_e_b[l]).at[0, N_EXPERTS:N_EXPERTS + N_GROUPS].set(router_g_b[l])
    wr_hi = wr.astype(BF16)
    wr_split = jnp.stack([wr_hi, (wr - wr_hi.astype(F32)).astype(BF16)])
    h, n2, meta, counts = _outproj(x2d, ya.reshape(t, D_RWKV), yb.reshape(t, D_HGRN),
                                   w_out[l].astype(BF16), row(norm2_g[l]), wr_split, br)

    out = _moe(n2, meta, counts, h, exp_w_gate[l], exp_w_up[l], exp_w_down[l], row(final_norm_g))
    return out.reshape(b, s, d)
```

```python
import functools

import numpy as np
import jax
import jax.numpy as jnp
from jax import lax
from jax.experimental import pallas as pl
from jax.experimental.pallas import tpu as pltpu

F32 = jnp.float32
BF16 = jnp.bfloat16

D_MODEL = 1024
D_RWKV = 512
D_HGRN = 512
RWKV_HEAD = 64
RANK_W = 64
RANK_A = 64
RANK_G = 128
HGRN_HEAD = 128
HGRN_HEADS = D_HGRN // HGRN_HEAD
CONV_W = 4
N_GROUPS = 4
EXPERTS_PER_GROUP = 8
N_EXPERTS = N_GROUPS * EXPERTS_PER_GROUP
D_EXPERT = 512
RMS_EPS = 1e-6
GN_EPS = 64e-5
L2_EPS = 1e-12
D_RWKV_IN = 3 * D_RWKV + RANK_W + RANK_A + RANK_G
D_HGRN_IN = 4 * D_HGRN

V7X_LANES = 128
V7X_SUBLANES = 8
V7X_MXU_DIM = 256
V7X_VMEM_LIMIT_BYTES = 56 * 1024 * 1024

CHUNK = 64
GROUP_LANES = V7X_MXU_DIM
HEADS_PER_GROUP = GROUP_LANES // RWKV_HEAD
N_RWKV_GROUPS = D_RWKV // GROUP_LANES
N_LEVELS = 6
HGRN_BATCH = 2
RWKV_BATCH = 2
TM_PROJ = 512
TM_DISPATCH = 512
TB_MOE = 512


def _dot(a, b):
    return jnp.dot(a, b, preferred_element_type=F32)


def _dot_nt(a, b):
    return lax.dot_general(a, b, (((1,), (1,)), ((), ())), preferred_element_type=F32)


def _dot_tn(a, b):
    return lax.dot_general(a, b, (((0,), (0,)), ((), ())), preferred_element_type=F32)


def _split_bf16(x):
    hi = x.astype(BF16)
    lo = (x - hi.astype(F32)).astype(BF16)
    return hi, lo


def _dot01_left(m01, x):
    hi, lo = _split_bf16(x)
    return _dot(m01, hi) + _dot(m01, lo)


def _log1p_exp_neg_abs(x):
    return jnp.log1p(jnp.exp(-jnp.abs(x)))


def _log_sigmoid(x):
    return jnp.minimum(x, 0.0) - _log1p_exp_neg_abs(x)


def _sigmoid(x):
    return 1.0 / (1.0 + jnp.exp(-x))


def _rms(x, gain):
    return x * lax.rsqrt(jnp.mean(x * x, axis=-1, keepdims=True) + RMS_EPS) * gain


def _const_spec(shape):
    nd = len(shape)
    return pl.BlockSpec(shape, lambda *_: (0,) * nd)


def _inproj_kernel(x_ref, g_ref, w_ref, pa_ref, pb_ref):
    n = _rms(x_ref[...], g_ref[...]).astype(BF16)
    pa_ref[...] = _dot(n, w_ref[:, :D_RWKV_IN])
    pb_ref[...] = _dot(n, w_ref[:, D_RWKV_IN:])


def _inproj(x2d, gain, w_bf16):
    t = x2d.shape[0]
    return pl.pallas_call(
        _inproj_kernel,
        grid=(t // TM_PROJ,),
        in_specs=[
            pl.BlockSpec((TM_PROJ, D_MODEL), lambda i: (i, 0)),
            _const_spec((1, D_MODEL)),
            _const_spec((D_MODEL, D_RWKV_IN + D_HGRN_IN)),
        ],
        out_specs=[
            pl.BlockSpec((TM_PROJ, D_RWKV_IN), lambda i: (i, 0)),
            pl.BlockSpec((TM_PROJ, D_HGRN_IN), lambda i: (i, 0)),
        ],
        out_shape=[
            jax.ShapeDtypeStruct((t, D_RWKV_IN), F32),
            jax.ShapeDtypeStruct((t, D_HGRN_IN), F32),
        ],
        compiler_params=pltpu.CompilerParams(
            dimension_semantics=("parallel",), vmem_limit_bytes=V7X_VMEM_LIMIT_BYTES),
        name="inproj",
    )(x2d, gain, w_bf16)


_M_EYE, _M_BD, _M_STRICT, _M_INCL, _M_OFF0 = 0, 1, 2, 3, 4


def _rwkv_masks():
    i = np.arange(GROUP_LANES)[:, None]
    j = np.arange(GROUP_LANES)[None, :]
    bd = (i // CHUNK) == (j // CHUNK)
    masks = [i == j, bd, (j % CHUNK) < (i % CHUNK), (j % CHUNK) <= (i % CHUNK)]
    for lvl in range(N_LEVELS):
        s = 1 << lvl
        masks.append(((i // (2 * s)) == (j // (2 * s))) & ((i % (2 * s)) >= s) & ((j % (2 * s)) < s))
    return np.stack(masks).astype(np.float32)


def _tile_rows(x, n):
    return jnp.concatenate([x] * n, axis=0)


def _rwkv_kernel(p_ref, mu_ref, w0_ref, a0_ref, wwa_ref, gup_ref, kk_ref, ka_ref, rk_ref, lnw_ref,
                 lnb_ref, tri_ref, m_ref, mb_ref, o_ref, prev_ref, st_ref):
    c = pl.program_id(1)

    @pl.when(c == 0)
    def _():
        prev_ref[...] = jnp.zeros_like(prev_ref)
        st_ref[...] = jnp.zeros_like(st_ref)

    rows = RWKV_BATCH * CHUNK
    p = p_ref[...].reshape(rows, D_RWKV_IN)
    row = lax.broadcasted_iota(jnp.int32, p.shape, 0)
    shifted = pltpu.roll(p, 1, 0)
    for bi in range(RWKV_BATCH):
        shifted = jnp.where(row == bi * CHUNK, prev_ref[bi, V7X_SUBLANES - 1:V7X_SUBLANES, :], shifted)
        prev_ref[bi] = p[(bi + 1) * CHUNK - V7X_SUBLANES:(bi + 1) * CHUNK, :]
    p = p + mu_ref[...] * (shifted - p)

    r = p[:, 0:D_RWKV]
    k = p[:, D_RWKV:2 * D_RWKV]
    v = p[:, 2 * D_RWKV:3 * D_RWKV]
    x_wa = p[:, 3 * D_RWKV:3 * D_RWKV + RANK_W + RANK_A]
    dg = p[:, 3 * D_RWKV + RANK_W + RANK_A:]

    lane = lax.broadcasted_iota(jnp.int32, x_wa.shape, 1)
    t_wa = jnp.where(lane < RANK_W, jnp.tanh(x_wa), x_wa).astype(BF16)
    wa = _dot(t_wa, wwa_ref[...])
    w_lin = w0_ref[...] + wa[:, :D_RWKV]
    w = _log_sigmoid(w_lin) - 0.5
    log_decay = -jnp.exp(w)
    a = _sigmoid(a0_ref[...] + wa[:, D_RWKV:])
    g = _dot(_sigmoid(dg).astype(BF16), gup_ref[...])

    bd_bf = mb_ref[_M_BD]

    def head_sum(x):
        hi, lo = _split_bf16(x)
        parts = [hi[:, :GROUP_LANES], hi[:, GROUP_LANES:], lo[:, :GROUP_LANES], lo[:, GROUP_LANES:]]
        s = _dot(jnp.concatenate(parts, axis=0), bd_bf)
        return jnp.concatenate([s[:rows] + s[2 * rows:3 * rows], s[rows:2 * rows] + s[3 * rows:]], axis=1)

    kk = k * kk_ref[...]
    kk = kk / jnp.maximum(jnp.sqrt(head_sum(kk * kk)), L2_EPS)
    k = k * (1.0 + (a - 1.0) * ka_ref[...])
    kb = kk * a

    sums = _dot01_left(tri_ref[...], log_decay)
    cum = sums[:rows]
    e_in = jnp.exp(cum)
    e_ex = jnp.exp(cum - log_decay)
    e_inv = jnp.exp(-cum)
    e_rem = jnp.exp(sums[rows:])

    r_hat = (r * e_in).astype(BF16)
    kk_hat = (kk * e_ex).astype(BF16)
    k_inv = (k * e_inv).astype(BF16)
    b_inv = (kb * e_inv).astype(BF16)
    k_rem = (k * e_rem).astype(BF16)
    b_rem = (kb * e_rem).astype(BF16)
    v_bf = v.astype(BF16)

    eye = m_ref[_M_EYE]
    bd = m_ref[_M_BD]
    strict_lc = m_ref[_M_STRICT][:CHUNK]
    incl_lc = m_ref[_M_INCL][:CHUNK]

    def block_diag(x_bf):
        return _tile_rows(x_bf, HEADS_PER_GROUP) * bd_bf

    chains = [(bi, gi) for bi in range(RWKV_BATCH) for gi in range(N_RWKV_GROUPS)]
    rs_of = lambda bi: slice(bi * CHUNK, (bi + 1) * CHUNK)
    sl_of = lambda gi: slice(gi * GROUP_LANES, (gi + 1) * GROUP_LANES)

    lhs, a_kk, a_rk, a_rb, a_kb_bd, inv = {}, {}, {}, {}, {}, {}
    for ch in chains:
        rs, sl = rs_of(ch[0]), sl_of(ch[1])
        lhs[ch] = jnp.concatenate([kk_hat[rs, sl], r_hat[rs, sl]], axis=0)
        a_k = _dot_nt(lhs[ch], block_diag(k_inv[rs, sl]))
        a_b = _dot_nt(lhs[ch], block_diag(b_inv[rs, sl]))
        a_kk[ch] = (a_k[:CHUNK] * strict_lc).astype(BF16)
        a_rk[ch] = (a_k[CHUNK:] * incl_lc).astype(BF16)
        a_rb[ch] = (a_b[CHUNK:] * incl_lc).astype(BF16)
        a_kb_bd[ch] = block_diag((a_b[:CHUNK] * strict_lc).astype(BF16))
        inv[ch] = eye - (a_kb_bd[ch] * mb_ref[_M_OFF0]).astype(F32)

    for lvl in range(1, N_LEVELS):
        inv_bf = {ch: inv[ch].astype(BF16) for ch in chains}
        mid = {ch: _dot(inv_bf[ch], a_kb_bd[ch] * mb_ref[_M_OFF0 + lvl]).astype(BF16) for ch in chains}
        for ch in chains:
            inv[ch] = inv[ch] - _dot(mid[ch], inv_bf[ch])

    ys = {}
    for ch in chains:
        bi, gi = ch
        rs, sl = rs_of(bi), sl_of(gi)
        inv_lc = inv[ch][0:CHUNK]
        for h in range(1, HEADS_PER_GROUP):
            inv_lc = inv_lc + inv[ch][h * CHUNK:(h + 1) * CHUNK]
        s_bd = st_ref[bi, gi]
        v_bd = block_diag(v_bf[rs, sl])
        from_state = _dot_nt(lhs[ch], s_bd.astype(BF16))
        rhs = from_state[:CHUNK] + _dot(a_kk[ch], v_bd)
        u_bf = _dot(inv_lc.astype(BF16), block_diag(rhs.astype(BF16))).astype(BF16)
        ys[ch] = from_state[CHUNK:] + _dot(a_rk[ch], v_bd) - _dot(a_rb[ch], block_diag(u_bf))

        vu = jnp.concatenate([v_bf[rs, sl], -u_bf], axis=0)
        kb_rem = jnp.concatenate([k_rem[rs, sl], b_rem[rs, sl]], axis=0)
        w_chunk = e_in[(bi + 1) * CHUNK - 1:(bi + 1) * CHUNK, sl]
        st_ref[bi, gi] = s_bd * w_chunk + _dot_tn(vu, kb_rem) * bd

    y = jnp.concatenate(
        [jnp.concatenate([ys[(bi, gi)] for gi in range(N_RWKV_GROUPS)], axis=1) for bi in range(RWKV_BATCH)],
        axis=0)
    inv_n = 1.0 / RWKV_HEAD
    mean = head_sum(y) * inv_n
    yc = y - mean
    var = head_sum(yc * yc) * inv_n
    yn = yc * lax.rsqrt(var + GN_EPS) * lnw_ref[...] + lnb_ref[...]
    bonus = head_sum(r * k * rk_ref[...])
    o_ref[...] = ((yn + bonus * v) * g).reshape(RWKV_BATCH, CHUNK, D_RWKV)


def _rwkv(pa, mu, w0, a0, wwa, gup, k_k, k_a, r_k, ln_w, ln_b):
    b, s, _ = pa.shape
    tril = np.tril(np.ones((CHUNK, CHUNK), np.float32))
    eye_b = np.eye(RWKV_BATCH, dtype=np.float32)
    tri = jnp.asarray(np.concatenate([np.kron(eye_b, tril), np.kron(eye_b, 1.0 - tril)], axis=0), BF16)
    masks = jnp.asarray(_rwkv_masks())
    vec = _const_spec((1, D_RWKV))
    return pl.pallas_call(
        _rwkv_kernel,
        grid=(b // RWKV_BATCH, s // CHUNK),
        in_specs=[
            pl.BlockSpec((RWKV_BATCH, CHUNK, D_RWKV_IN), lambda i, j: (i, j, 0)),
            _const_spec((1, D_RWKV_IN)),
            vec, vec,
            _const_spec((RANK_W + RANK_A, 2 * D_RWKV)),
            _const_spec((RANK_G, D_RWKV)),
            vec, vec, vec, vec, vec,
            _const_spec(tri.shape),
            _const_spec(masks.shape),
            _const_spec(masks.shape),
        ],
        out_specs=pl.BlockSpec((RWKV_BATCH, CHUNK, D_RWKV), lambda i, j: (i, j, 0)),
        out_shape=jax.ShapeDtypeStruct((b, s, D_RWKV), F32),
        scratch_shapes=[
            pltpu.VMEM((RWKV_BATCH, V7X_SUBLANES, D_RWKV_IN), F32),
            pltpu.VMEM((RWKV_BATCH, N_RWKV_GROUPS, GROUP_LANES, GROUP_LANES), F32),
        ],
        compiler_params=pltpu.CompilerParams(
            dimension_semantics=("parallel", "arbitrary"), vmem_limit_bytes=V7X_VMEM_LIMIT_BYTES),
        name="rwkv7",
    )(pa, mu, w0, a0, wwa, gup, k_k, k_a, r_k, ln_w, ln_b, tri, masks, masks.astype(BF16))


def _hgrn_sum_matrix():
    t = np.arange(CHUNK)[:, None]
    j = np.arange(CHUNK)[None, :]
    mats = [(j <= t), (j > t)]
    for lvl in range(N_LEVELS):
        s = 1 << lvl
        mid = (t // (2 * s)) * (2 * s) + s - 1
        mats.append(((j > mid) & (j <= t)).astype(np.float32) - ((j > t) & (j <= mid)).astype(np.float32))
    return np.concatenate([np.asarray(m, np.float32) for m in mats], axis=0)


def _hgrn_level_masks():
    t = np.arange(CHUNK)[:, None]
    s_ = np.arange(CHUNK)[None, :]
    masks = [t == s_]
    for lvl in range(N_LEVELS):
        s = 1 << lvl
        masks.append(((t // (2 * s)) == (s_ // (2 * s))) & ((t % (2 * s)) >= s) & ((s_ % (2 * s)) < s))
    return np.stack(masks).astype(np.float32)


def _hgrn_kernel(p_ref, cw_ref, lbl_ref, ng_ref, sm_ref, lm_ref, o_ref, prev_ref, st_ref):
    c = pl.program_id(1)

    @pl.when(c == 0)
    def _():
        prev_ref[...] = jnp.zeros_like(prev_ref)
        st_ref[...] = jnp.zeros_like(st_ref)

    l0 = lbl_ref[0:1, :]
    l1 = lbl_ref[1:2, :]
    lmax = jnp.maximum(l0, l1)
    e0 = jnp.exp(l0 - lmax)
    lb = e0 / (e0 + jnp.exp(l1 - lmax))
    ta = jnp.log(lb)

    q_lv, k_lv, q_inter, k_inter, w_chunk, val_bf = {}, {}, {}, {}, {}, {}
    for bi in range(HGRN_BATCH):
        x = p_ref[bi, :, :3 * D_HGRN]
        xe = jnp.concatenate([prev_ref[bi], x], axis=0)
        prev_ref[bi] = x[CHUNK - V7X_SUBLANES:, :]
        conv = x * cw_ref[CONV_W - 1:CONV_W, :]
        for s in range(1, CONV_W):
            conv = conv + pltpu.roll(xe, s, 0)[V7X_SUBLANES:, :] * cw_ref[CONV_W - 1 - s:CONV_W - s, :]

        qc = conv[:, :D_HGRN]
        f = conv[:, D_HGRN:2 * D_HGRN]
        val_bf[bi] = conv[:, 2 * D_HGRN:].astype(BF16)
        q = qc * _sigmoid(qc)
        tb = jnp.log1p(-lb) + _log_sigmoid(f)
        log_f = jnp.maximum(ta, tb) + _log1p_exp_neg_abs(ta - tb)
        kg = (1.0 - lb) * _sigmoid(-f)

        sums = _dot01_left(sm_ref[...], log_f)
        b_in = sums[0:CHUNK]
        q_inter[bi] = (q * jnp.exp(b_in)).astype(BF16)
        k_inter[bi] = (kg * jnp.exp(sums[CHUNK:2 * CHUNK])).astype(BF16)
        w_chunk[bi] = jnp.exp(b_in[CHUNK - 1:CHUNK, :])
        q_lv[bi] = [q.astype(BF16)]
        k_lv[bi] = [kg.astype(BF16)]
        for lvl in range(N_LEVELS):
            e = jnp.exp(-jnp.abs(sums[(2 + lvl) * CHUNK:(3 + lvl) * CHUNK])).astype(BF16)
            q_lv[bi].append(q_lv[bi][0] * e)
            k_lv[bi].append(k_lv[bi][0] * e)

    chains = [(bi, h) for bi in range(HGRN_BATCH) for h in range(HGRN_HEADS)]
    sl_of = lambda h: slice(h * HGRN_HEAD, (h + 1) * HGRN_HEAD)
    scores = {}
    for lvl in range(N_LEVELS + 1):
        for ch in chains:
            bi, sl = ch[0], sl_of(ch[1])
            term = _dot_nt(q_lv[bi][lvl][:, sl], k_lv[bi][lvl][:, sl]) * lm_ref[lvl]
            scores[ch] = term if lvl == 0 else scores[ch] + term
    outs = {}
    for ch in chains:
        bi, h = ch
        sl = sl_of(h)
        st = st_ref[bi, h]
        o = _dot(scores[ch].astype(BF16), val_bf[bi][:, sl]) + _dot_nt(q_inter[bi][:, sl], st.astype(BF16))
        st_ref[bi, h] = st * w_chunk[bi][:, sl] + _dot_tn(val_bf[bi][:, sl], k_inter[bi][:, sl])
        outs[ch] = o * lax.rsqrt(jnp.mean(o * o, axis=-1, keepdims=True) + RMS_EPS)
    for bi in range(HGRN_BATCH):
        gate = p_ref[bi, :, 3 * D_HGRN:]
        o = jnp.concatenate([outs[(bi, h)] for h in range(HGRN_HEADS)], axis=1)
        o_ref[bi] = o * ng_ref[...] * (gate * _sigmoid(gate))


def _hgrn(pb, conv_w, lb_logits, norm_g):
    b, s, _ = pb.shape
    sum_m = jnp.asarray(_hgrn_sum_matrix(), BF16)
    lvl_m = jnp.asarray(_hgrn_level_masks())
    return pl.pallas_call(
        _hgrn_kernel,
        grid=(b // HGRN_BATCH, s // CHUNK),
        in_specs=[
            pl.BlockSpec((HGRN_BATCH, CHUNK, D_HGRN_IN), lambda i, j: (i, j, 0)),
            _const_spec((CONV_W, 3 * D_HGRN)),
            _const_spec(lb_logits.shape),
            _const_spec((1, D_HGRN)),
            _const_spec(sum_m.shape),
            _const_spec(lvl_m.shape),
        ],
        out_specs=pl.BlockSpec((HGRN_BATCH, CHUNK, D_HGRN), lambda i, j: (i, j, 0)),
        out_shape=jax.ShapeDtypeStruct((b, s, D_HGRN), F32),
        scratch_shapes=[
            pltpu.VMEM((HGRN_BATCH, V7X_SUBLANES, 3 * D_HGRN), F32),
            pltpu.VMEM((HGRN_BATCH, HGRN_HEADS, HGRN_HEAD, HGRN_HEAD), F32),
        ],
        compiler_params=pltpu.CompilerParams(
            dimension_semantics=("parallel", "arbitrary"), vmem_limit_bytes=V7X_VMEM_LIMIT_BYTES),
        name="hgrn2",
    )(pb, conv_w, lb_logits, norm_g, sum_m, lvl_m)


ROUTER_LANES = V7X_LANES


TOKEN_ROWS = V7X_SUBLANES
assert D_MODEL == TOKEN_ROWS * V7X_LANES


def _tiles_shape(tokens):
    return (tokens * TOKEN_ROWS, V7X_LANES)


def _store_token_tiles(ref, x):
    rows = x.shape[0]
    for s in range(TOKEN_ROWS):
        ref[pl.ds(s, rows, stride=TOKEN_ROWS), :] = x[:, s * V7X_LANES:(s + 1) * V7X_LANES]


def _load_token_tiles(ref):
    rows = ref.shape[0] // TOKEN_ROWS
    return jnp.concatenate([ref[pl.ds(s, rows, stride=TOKEN_ROWS), :] for s in range(TOKEN_ROWS)], axis=1)


_R_E1, _R_E2, _R_G1, _R_G2, _R_RANK1, _R_RANK2 = 0, 1, 2, 3, 4, 5


def _outproj_kernel(x_ref, ya_ref, yb_ref, wo_ref, g2_ref, wr_ref, br_ref, tri_ref,
                    h_ref, n_ref, meta_ref, counts_ref, carry_ref):
    @pl.when(pl.program_id(0) == 0)
    def _():
        carry_ref[...] = jnp.zeros_like(carry_ref)

    h = (x_ref[...]
         + _dot(ya_ref[...].astype(BF16), wo_ref[:D_RWKV, :])
         + _dot(yb_ref[...].astype(BF16), wo_ref[D_RWKV:, :]))
    h_ref[...] = h
    n = _rms(h, g2_ref[...])
    _store_token_tiles(n_ref, n)

    n_hi, n_lo = _split_bf16(n)
    logits = (_dot(n_hi, wr_ref[0]) + _dot(n_lo, wr_ref[0]) + _dot(n_hi, wr_ref[1])) + br_ref[...]
    lane = lax.broadcasted_iota(jnp.int32, logits.shape, 1).astype(F32)
    neg = -jnp.inf
    gl = jnp.where((lane >= N_EXPERTS) & (lane < N_EXPERTS + N_GROUPS), logits, neg)
    gmax = jnp.max(gl, axis=-1, keepdims=True)
    gidx = jnp.min(jnp.where(gl == gmax, lane - N_EXPERTS, ROUTER_LANES), axis=-1, keepdims=True)
    g_w = 1.0 / jnp.sum(jnp.exp(gl - gmax), axis=-1, keepdims=True)
    lo = gidx * EXPERTS_PER_GROUP
    el = jnp.where((lane >= lo) & (lane < lo + EXPERTS_PER_GROUP), logits, neg)
    m1 = jnp.max(el, axis=-1, keepdims=True)
    i1 = jnp.min(jnp.where(el == m1, lane, ROUTER_LANES), axis=-1, keepdims=True)
    el2 = jnp.where(lane == i1, neg, el)
    m2 = jnp.max(el2, axis=-1, keepdims=True)
    i2 = jnp.min(jnp.where(el2 == m2, lane, ROUTER_LANES), axis=-1, keepdims=True)
    t = jnp.exp(m2 - m1)
    w1 = 1.0 / (1.0 + t)

    onehot1 = lane == i1
    onehot2 = lane == i2
    assigned = jnp.where(onehot1 | onehot2, 1.0, 0.0)
    before = _dot(tri_ref[...], assigned.astype(BF16)) + carry_ref[...]
    rank1 = jnp.sum(jnp.where(onehot1, before, 0.0), axis=-1, keepdims=True)
    rank2 = jnp.sum(jnp.where(onehot2, before, 0.0), axis=-1, keepdims=True)
    carry_ref[...] += jnp.sum(assigned, axis=0, keepdims=True)
    counts_ref[...] = jnp.broadcast_to(carry_ref[...], counts_ref.shape)

    meta = jnp.zeros_like(logits)
    for idx, val in ((_R_E1, i1), (_R_E2, i2), (_R_G1, g_w * w1), (_R_G2, g_w * (t * w1)),
                     (_R_RANK1, rank1), (_R_RANK2, rank2)):
        meta = jnp.where(lane == idx, val, meta)
    meta_ref[...] = meta


def _outproj(x2d, ya, yb, wo_bf16, g2, wr, br):
    t = x2d.shape[0]
    row = lambda w: pl.BlockSpec((TM_PROJ, w), lambda i: (i, 0))
    tri = jnp.asarray(np.tril(np.ones((TM_PROJ, TM_PROJ), np.float32), -1), BF16)
    return pl.pallas_call(
        _outproj_kernel,
        grid=(t // TM_PROJ,),
        in_specs=[
            row(D_MODEL), row(D_RWKV), row(D_HGRN),
            _const_spec((D_RWKV + D_HGRN, D_MODEL)),
            _const_spec((1, D_MODEL)),
            _const_spec((2, D_MODEL, ROUTER_LANES)),
            _const_spec((1, ROUTER_LANES)),
            _const_spec((TM_PROJ, TM_PROJ)),
        ],
        out_specs=[row(D_MODEL), pl.BlockSpec(_tiles_shape(TM_PROJ), lambda i: (i, 0)), row(ROUTER_LANES),
                   _const_spec((V7X_SUBLANES, ROUTER_LANES))],
        out_shape=[
            jax.ShapeDtypeStruct((t, D_MODEL), F32),
            jax.ShapeDtypeStruct(_tiles_shape(t), F32),
            jax.ShapeDtypeStruct((t, ROUTER_LANES), F32),
            jax.ShapeDtypeStruct((V7X_SUBLANES, ROUTER_LANES), F32),
        ],
        scratch_shapes=[pltpu.VMEM((1, ROUTER_LANES), F32)],
        compiler_params=pltpu.CompilerParams(
            dimension_semantics=("arbitrary",), vmem_limit_bytes=V7X_VMEM_LIMIT_BYTES),
        name="outproj_router",
    )(x2d, ya, yb, wo_bf16, g2, wr, br, tri)


def _slot_kernel(meta_ref, offs_ref, pos_ref):
    meta = meta_ref[...]
    lane = lax.broadcasted_iota(jnp.int32, meta.shape, 1).astype(F32)
    offs = offs_ref[...]

    def slot(e_lane, rank_lane):
        e = meta[:, e_lane:e_lane + 1]
        return jnp.sum(jnp.where(lane == e, offs, 0.0), axis=-1, keepdims=True) + meta[:, rank_lane:rank_lane + 1]

    pos = jnp.where(lane == 0, slot(_R_E1, _R_RANK1), jnp.where(lane == 1, slot(_R_E2, _R_RANK2), 0.0))
    pos_ref[...] = pos.astype(jnp.int32)


def _slots(meta, offs):
    t = meta.shape[0]
    return pl.pallas_call(
        _slot_kernel,
        grid=(t // TM_PROJ,),
        in_specs=[pl.BlockSpec((TM_PROJ, ROUTER_LANES), lambda i: (i, 0)), _const_spec((1, ROUTER_LANES))],
        out_specs=pl.BlockSpec((TM_PROJ, ROUTER_LANES), lambda i: (i, 0)),
        out_shape=jax.ShapeDtypeStruct((t, ROUTER_LANES), jnp.int32),
        compiler_params=pltpu.CompilerParams(dimension_semantics=("parallel",)),
        name="moe_slots",
    )(meta, offs)


def _token_rows(token, count=1):
    return pl.ds(pl.multiple_of(token * TOKEN_ROWS, TOKEN_ROWS), count * TOKEN_ROWS)


def _row_copy(src_ref, src_token, dst_ref, dst_token, sem):
    return pltpu.make_async_copy(src_ref.at[_token_rows(src_token)], dst_ref.at[_token_rows(dst_token)], sem)


def _dispatch_kernel(pos_ref, n_ref, xs_in_ref, xs_ref, sem):
    del xs_in_ref

    def body(r, carry):
        _row_copy(n_ref, r, xs_ref, pos_ref[2 * r], sem).start(priority=0)
        _row_copy(n_ref, r, xs_ref, pos_ref[2 * r + 1], sem).start(priority=1)
        return carry

    lax.fori_loop(0, TM_DISPATCH, body, 0, unroll=8)
    for _ in range(2):
        pltpu.make_async_copy(n_ref, xs_ref.at[_token_rows(0, TM_DISPATCH)], sem).wait()


def _dispatch(pos_flat, n2, n_slots):
    t = n2.shape[0] // TOKEN_ROWS
    xs0 = jnp.zeros(_tiles_shape(n_slots), F32)
    return pl.pallas_call(
        _dispatch_kernel,
        grid=(t // TM_DISPATCH,),
        in_specs=[
            pl.BlockSpec((2 * TM_DISPATCH,), lambda i: (i,), memory_space=pltpu.SMEM),
            pl.BlockSpec(_tiles_shape(TM_DISPATCH), lambda i: (i, 0)),
            pl.BlockSpec(memory_space=pl.ANY),
        ],
        out_specs=pl.BlockSpec(memory_space=pl.ANY),
        out_shape=jax.ShapeDtypeStruct(_tiles_shape(n_slots), F32),
        scratch_shapes=[pltpu.SemaphoreType.DMA(())],
        input_output_aliases={2: 0},
        compiler_params=pltpu.CompilerParams(
            dimension_semantics=("arbitrary",), has_side_effects=True),
        name="moe_dispatch",
    )(pos_flat, n2, xs0)


def _gmm_kernel(be_ref, nu_ref, xs_ref, wg_ref, wu_ref, wd_ref, y_ref, wgu_s, wd_s):
    b = pl.program_id(0)

    @pl.when(b < nu_ref[0])
    def _():
        @pl.when((b == 0) | (be_ref[b] != be_ref[jnp.maximum(b - 1, 0)]))
        def _():
            wgu_s[:, :D_EXPERT] = wg_ref[...].astype(BF16)
            wgu_s[:, D_EXPERT:] = wu_ref[...].astype(BF16)
            wd_s[...] = wd_ref[...].astype(BF16)

        gu = _dot(_load_token_tiles(xs_ref).astype(BF16), wgu_s[...])
        hg = gu[:, :D_EXPERT]
        hid = (hg * _sigmoid(hg) * gu[:, D_EXPERT:]).astype(BF16)
        _store_token_tiles(y_ref, _dot(hid, wd_s[...]))

    @pl.when(b >= nu_ref[0])
    def _():
        y_ref[...] = jnp.zeros_like(y_ref)


def _gmm(block_expert, n_used, xs, wg, wu, wd):
    n_slots = xs.shape[0] // TOKEN_ROWS
    last = lambda b, nu: jnp.maximum(jnp.minimum(b, nu[0] - 1), 0)
    blk = lambda b, be, nu: (last(b, nu), 0)
    wsel = lambda b, be, nu: (be[last(b, nu)], 0, 0)
    return pl.pallas_call(
        _gmm_kernel,
        grid_spec=pltpu.PrefetchScalarGridSpec(
            num_scalar_prefetch=2,
            grid=(n_slots // TB_MOE,),
            in_specs=[
                pl.BlockSpec(_tiles_shape(TB_MOE), blk),
                pl.BlockSpec((None, D_MODEL, D_EXPERT), wsel),
                pl.BlockSpec((None, D_MODEL, D_EXPERT), wsel),
                pl.BlockSpec((None, D_EXPERT, D_MODEL), wsel),
            ],
            out_specs=pl.BlockSpec(_tiles_shape(TB_MOE), lambda b, be, nu: (b, 0)),
            scratch_shapes=[
                pltpu.VMEM((D_MODEL, 2 * D_EXPERT), BF16),
                pltpu.VMEM((D_EXPERT, D_MODEL), BF16),
            ],
        ),
        out_shape=jax.ShapeDtypeStruct(xs.shape, F32),
        compiler_params=pltpu.CompilerParams(
            dimension_semantics=("arbitrary",), vmem_limit_bytes=V7X_VMEM_LIMIT_BYTES),
        name="moe_gmm",
    )(block_expert, n_used, xs, wg, wu, wd)


def _combine_kernel(pos_ref, pos_next_ref, meta_ref, h_ref, gf_ref, ys_ref, o_ref, buf_ref, sem):
    i = pl.program_id(0)
    slot = lax.rem(i, 2)

    def issue(p_ref, s):
        def body(r, carry):
            _row_copy(ys_ref, p_ref[2 * r], buf_ref.at[s, 0], r, sem.at[s]).start(priority=0)
            _row_copy(ys_ref, p_ref[2 * r + 1], buf_ref.at[s, 1], r, sem.at[s]).start(priority=1)
            return carry
        lax.fori_loop(0, TM_DISPATCH, body, 0, unroll=8)

    @pl.when(i == 0)
    def _():
        issue(pos_ref, 0)

    @pl.when(i + 1 < pl.num_programs(0))
    def _():
        issue(pos_next_ref, 1 - slot)

    for j in range(2):
        pltpu.make_async_copy(ys_ref.at[_token_rows(0, TM_DISPATCH)], buf_ref.at[slot, j], sem.at[slot]).wait()

    meta = meta_ref[...]
    g1 = meta[:, _R_G1:_R_G1 + 1]
    g2 = meta[:, _R_G2:_R_G2 + 1]
    moe = g1 * _load_token_tiles(buf_ref.at[slot, 0]) + g2 * _load_token_tiles(buf_ref.at[slot, 1])
    o_ref[...] = _rms(h_ref[...] + moe, gf_ref[...])


def _combine(pos_flat, meta, h, gf, ys):
    t = h.shape[0]
    n_tiles = t // TM_DISPATCH
    return pl.pallas_call(
        _combine_kernel,
        grid=(n_tiles,),
        in_specs=[
            pl.BlockSpec((2 * TM_DISPATCH,), lambda i: (i,), memory_space=pltpu.SMEM),
            pl.BlockSpec((2 * TM_DISPATCH,), lambda i: (jnp.minimum(i + 1, n_tiles - 1),),
                         memory_space=pltpu.SMEM),
            pl.BlockSpec((TM_DISPATCH, ROUTER_LANES), lambda i: (i, 0)),
            pl.BlockSpec((TM_DISPATCH, D_MODEL), lambda i: (i, 0)),
            _const_spec((1, D_MODEL)),
            pl.BlockSpec(memory_space=pl.ANY),
        ],
        out_specs=pl.BlockSpec((TM_DISPATCH, D_MODEL), lambda i: (i, 0)),
        out_shape=jax.ShapeDtypeStruct((t, D_MODEL), F32),
        scratch_shapes=[
            pltpu.VMEM((2, 2) + _tiles_shape(TM_DISPATCH), F32),
            pltpu.SemaphoreType.DMA((2,)),
        ],
        compiler_params=pltpu.CompilerParams(
            dimension_semantics=("arbitrary",), vmem_limit_bytes=V7X_VMEM_LIMIT_BYTES),
        name="moe_combine",
    )(pos_flat, pos_flat, meta, h, gf, ys)


def _moe(n2, meta, counts, h, wg, wu, wd, gf):
    t = h.shape[0]
    n_blocks = (2 * t + N_EXPERTS * (TB_MOE - 1) + TB_MOE - 1) // TB_MOE
    cnt = counts[0, :N_EXPERTS].astype(jnp.int32)
    blocks = (cnt + TB_MOE - 1) // TB_MOE
    ends = jnp.cumsum(blocks)
    offs = ((ends - blocks) * TB_MOE).astype(F32)
    offs_row = jnp.zeros((1, ROUTER_LANES), F32).at[0, :N_EXPERTS].set(offs)
    block_ids = jnp.arange(n_blocks, dtype=jnp.int32)
    block_expert = jnp.minimum(
        jnp.sum((ends[None, :] <= block_ids[:, None]).astype(jnp.int32), axis=1), N_EXPERTS - 1)
    n_used = ends[-1:].astype(jnp.int32)

    pos = _slots(meta, offs_row)
    pos_flat = pos[:, :2].reshape(-1)
    xs = _dispatch(pos_flat, n2, n_blocks * TB_MOE)
    ys = _gmm(block_expert, n_used, xs, wg, wu, wd)
    return _combine(pos_flat, meta, h, gf, ys)


def kernel(x, norm1_g, w_in, rwkv_mu, rwkv_w0, rwkv_w_up, rwkv_a0, rwkv_a_up, rwkv_g_up, rwkv_k_k,
           rwkv_k_a, rwkv_r_k, rwkv_ln_w, rwkv_ln_b, hgrn_conv_w, hgrn_lb_logits, hgrn_norm_g, w_out,
           norm2_g, router_g_w, router_g_b, router_e_w, router_e_b, exp_w_gate, exp_w_up, exp_w_down,
           final_norm_g):
    b, s, d = x.shape
    t = b * s
    l = 0
    x2d = x.reshape(t, d)
    row = lambda a: a.reshape(1, -1)

    pa, pb = _inproj(x2d, row(norm1_g[l]), w_in[l].astype(BF16))

    wwa = jnp.zeros((RANK_W + RANK_A, 2 * D_RWKV), F32)
    wwa = wwa.at[:RANK_W, :D_RWKV].set(rwkv_w_up[l]).at[RANK_W:, D_RWKV:].set(rwkv_a_up[l]).astype(BF16)
    ya = _rwkv(pa.reshape(b, s, D_RWKV_IN), row(rwkv_mu[l]), row(rwkv_w0[l]), row(rwkv_a0[l]), wwa,
               rwkv_g_up[l].astype(BF16), row(rwkv_k_k[l]), row(rwkv_k_a[l]), row(rwkv_r_k[l]),
               row(rwkv_ln_w[l]), row(rwkv_ln_b[l]))
    yb = _hgrn(pb.reshape(b, s, D_HGRN_IN), hgrn_conv_w[l], hgrn_lb_logits, row(hgrn_norm_g[l]))

    wr = jnp.zeros((D_MODEL, ROUTER_LANES), F32)
    wr = wr.at[:, :N_EXPERTS].set(router_e_w[l]).at[:, N_EXPERTS:N_EXPERTS + N_GROUPS].set(router_g_w[l])
    br = jnp.zeros((1, ROUTER_LANES), F32)
    br = br.at[0, :N_EXPERTS].set(router_e_b[l]).at[0, N_EXPERTS:N_EXPERTS + N_GROUPS].set(router_g_b[l])
    wr_hi = wr.astype(BF16)
    wr_split = jnp.stack([wr_hi, (wr - wr_hi.astype(F32)).astype(BF16)])
    h, n2, meta, counts = _outproj(x2d, ya.reshape(t, D_RWKV), yb.reshape(t, D_HGRN),
                                   w_out[l].astype(BF16), row(norm2_g[l]), wr_split, br)

    out = _moe(n2, meta, counts, h, exp_w_gate[l], exp_w_up[l], exp_w_down[l], row(final_norm_g))
    return out.reshape(b, s, d)
```

```python
import functools

import numpy as np
import jax
import jax.numpy as jnp
from jax import lax
from jax.experimental import pallas as pl
from jax.experimental.pallas import tpu as pltpu

F32 = jnp.float32
BF16 = jnp.bfloat16

D_MODEL = 1024
D_RWKV = 512
D_HGRN = 512
RWKV_HEAD = 64
RANK_W = 64
RANK_A = 64
RANK_G = 128
HGRN_HEAD = 128
HGRN_HEADS = D_HGRN // HGRN_HEAD
CONV_W = 4
N_GROUPS = 4
EXPERTS_PER_GROUP = 8
N_EXPERTS = N_GROUPS * EXPERTS_PER_GROUP
D_EXPERT = 512
RMS_EPS = 1e-6
GN_EPS = 64e-5
L2_EPS = 1e-12
D_RWKV_IN = 3 * D_RWKV + RANK_W + RANK_A + RANK_G
D_HGRN_IN = 4 * D_HGRN

V7X_LANES = 128
V7X_SUBLANES = 8
V7X_MXU_DIM = 256
V7X_VMEM_LIMIT_BYTES = 56 * 1024 * 1024

CHUNK = 64
GROUP_LANES = V7X_MXU_DIM
HEADS_PER_GROUP = GROUP_LANES // RWKV_HEAD
N_RWKV_GROUPS = D_RWKV // GROUP_LANES
N_LEVELS = 6
HGRN_BATCH = 2
RWKV_BATCH = 2
TM_PROJ = 512
TM_DISPATCH = 512
TB_MOE = 512


def _dot(a, b):
    return jnp.dot(a, b, preferred_element_type=F32)


def _dot_nt(a, b):
    return lax.dot_general(a, b, (((1,), (1,)), ((), ())), preferred_element_type=F32)


def _dot_tn(a, b):
    return lax.dot_general(a, b, (((0,), (0,)), ((), ())), preferred_element_type=F32)


def _split_bf16(x):
    hi = x.astype(BF16)
    lo = (x - hi.astype(F32)).astype(BF16)
    return hi, lo


def _dot01_left(m01, x):
    hi, lo = _split_bf16(x)
    return _dot(m01, hi) + _dot(m01, lo)


def _log1p_exp_neg_abs(x):
    return jnp.log1p(jnp.exp(-jnp.abs(x)))


def _log_sigmoid(x):
    return jnp.minimum(x, 0.0) - _log1p_exp_neg_abs(x)


def _sigmoid(x):
    return 1.0 / (1.0 + jnp.exp(-x))


def _rms(x, gain):
    return x * lax.rsqrt(jnp.mean(x * x, axis=-1, keepdims=True) + RMS_EPS) * gain


def _const_spec(shape):
    nd = len(shape)
    return pl.BlockSpec(shape, lambda *_: (0,) * nd)


def _inproj_kernel(x_ref, g_ref, w_ref, pa_ref, pb_ref):
    n = _rms(x_ref[...], g_ref[...]).astype(BF16)
    pa_ref[...] = _dot(n, w_ref[:, :D_RWKV_IN])
    pb_ref[...] = _dot(n, w_ref[:, D_RWKV_IN:])


def _inproj(x2d, gain, w_bf16):
    t = x2d.shape[0]
    return pl.pallas_call(
        _inproj_kernel,
        grid=(t // TM_PROJ,),
        in_specs=[
            pl.BlockSpec((TM_PROJ, D_MODEL), lambda i: (i, 0)),
            _const_spec((1, D_MODEL)),
            _const_spec((D_MODEL, D_RWKV_IN + D_HGRN_IN)),
        ],
        out_specs=[
            pl.BlockSpec((TM_PROJ, D_RWKV_IN), lambda i: (i, 0)),
            pl.BlockSpec((TM_PROJ, D_HGRN_IN), lambda i: (i, 0)),
        ],
        out_shape=[
            jax.ShapeDtypeStruct((t, D_RWKV_IN), F32),
            jax.ShapeDtypeStruct((t, D_HGRN_IN), F32),
        ],
        compiler_params=pltpu.CompilerParams(
            dimension_semantics=("parallel",), vmem_limit_bytes=V7X_VMEM_LIMIT_BYTES),
        name="inproj",
    )(x2d, gain, w_bf16)


_M_EYE, _M_BD, _M_STRICT, _M_INCL, _M_OFF0 = 0, 1, 2, 3, 4


def _rwkv_masks():
    i = np.arange(GROUP_LANES)[:, None]
    j = np.arange(GROUP_LANES)[None, :]
    bd = (i // CHUNK) == (j // CHUNK)
    masks = [i == j, bd, (j % CHUNK) < (i % CHUNK), (j % CHUNK) <= (i % CHUNK)]
    for lvl in range(N_LEVELS):
        s = 1 << lvl
        masks.append(((i // (2 * s)) == (j // (2 * s))) & ((i % (2 * s)) >= s) & ((j % (2 * s)) < s))
    return np.stack(masks).astype(np.float32)


def _tile_rows(x, n):
    return jnp.concatenate([x] * n, axis=0)


def _rwkv_kernel(p_ref, mu_ref, w0_ref, a0_ref, wwa_ref, gup_ref, kk_ref, ka_ref, rk_ref, lnw_ref,
                 lnb_ref, tri_ref, m_ref, mb_ref, o_ref, z_ref, prev_ref, st_ref):
    c = pl.program_id(1)

    @pl.when(c == 0)
    def _():
        prev_ref[...] = jnp.zeros_like(prev_ref)
        st_ref[...] = jnp.zeros_like(st_ref)

    rows = RWKV_BATCH * CHUNK
    p = p_ref[...].reshape(rows, D_RWKV_IN)
    row = lax.broadcasted_iota(jnp.int32, p.shape, 0)
    shifted = pltpu.roll(p, 1, 0)
    for bi in range(RWKV_BATCH):
        shifted = jnp.where(row == bi * CHUNK, prev_ref[bi, V7X_SUBLANES - 1:V7X_SUBLANES, :], shifted)
        prev_ref[bi] = p[(bi + 1) * CHUNK - V7X_SUBLANES:(bi + 1) * CHUNK, :]
    p = p + mu_ref[...] * (shifted - p)

    r = p[:, 0:D_RWKV]
    k = p[:, D_RWKV:2 * D_RWKV]
    v = p[:, 2 * D_RWKV:3 * D_RWKV]
    x_wa = p[:, 3 * D_RWKV:3 * D_RWKV + RANK_W + RANK_A]
    dg = p[:, 3 * D_RWKV + RANK_W + RANK_A:]

    lane = lax.broadcasted_iota(jnp.int32, x_wa.shape, 1)
    t_wa = jnp.where(lane < RANK_W, jnp.tanh(x_wa), x_wa).astype(BF16)
    wa = _dot(t_wa, wwa_ref[...])
    w_lin = w0_ref[...] + wa[:, :D_RWKV]
    w = _log_sigmoid(w_lin) - 0.5
    log_decay = -jnp.exp(w)
    a = _sigmoid(a0_ref[...] + wa[:, D_RWKV:])
    g = _dot(_sigmoid(dg).astype(BF16), gup_ref[...])

    bd_bf = mb_ref[_M_BD]

    def head_sum(x):
        hi, lo = _split_bf16(x)
        parts = [hi[:, :GROUP_LANES], hi[:, GROUP_LANES:], lo[:, :GROUP_LANES], lo[:, GROUP_LANES:]]
        s = _dot(jnp.concatenate(parts, axis=0), bd_bf)
        return jnp.concatenate([s[:rows] + s[2 * rows:3 * rows], s[rows:2 * rows] + s[3 * rows:]], axis=1)

    kk = k * kk_ref[...]
    kk = kk / jnp.maximum(jnp.sqrt(head_sum(kk * kk)), L2_EPS)
    k = k * (1.0 + (a - 1.0) * ka_ref[...])
    kb = kk * a

    sums = _dot01_left(tri_ref[...], log_decay)
    cum = sums[:rows]
    e_in = jnp.exp(cum)
    e_ex = jnp.exp(cum - log_decay)
    e_inv = jnp.exp(-cum)
    e_rem = jnp.exp(sums[rows:])

    r_hat = (r * e_in).astype(BF16)
    kk_hat = (kk * e_ex).astype(BF16)
    k_inv = (k * e_inv).astype(BF16)
    b_inv = (kb * e_inv).astype(BF16)
    k_rem = (k * e_rem).astype(BF16)
    b_rem = (kb * e_rem).astype(BF16)
    v_bf = v.astype(BF16)

    eye = m_ref[_M_EYE]
    bd = m_ref[_M_BD]
    strict_lc = m_ref[_M_STRICT][:CHUNK]
    incl_lc = m_ref[_M_INCL][:CHUNK]

    def block_diag(x_bf):
        return _tile_rows(x_bf, HEADS_PER_GROUP) * bd_bf

    chains = [(bi, gi) for bi in range(RWKV_BATCH) for gi in range(N_RWKV_GROUPS)]
    rs_of = lambda bi: slice(bi * CHUNK, (bi + 1) * CHUNK)
    sl_of = lambda gi: slice(gi * GROUP_LANES, (gi + 1) * GROUP_LANES)

    lhs, a_kk, a_rk, a_rb, a_kb_bd, inv = {}, {}, {}, {}, {}, {}
    for ch in chains:
        rs, sl = rs_of(ch[0]), sl_of(ch[1])
        lhs[ch] = jnp.concatenate([kk_hat[rs, sl], r_hat[rs, sl]], axis=0)
        a_k = _dot_nt(lhs[ch], block_diag(k_inv[rs, sl]))
        a_b = _dot_nt(lhs[ch], block_diag(b_inv[rs, sl]))
        a_kk[ch] = (a_k[:CHUNK] * strict_lc).astype(BF16)
        a_rk[ch] = (a_k[CHUNK:] * incl_lc).astype(BF16)
        a_rb[ch] = (a_b[CHUNK:] * incl_lc).astype(BF16)
        a_kb_bd[ch] = block_diag((a_b[:CHUNK] * strict_lc).astype(BF16))
        inv[ch] = eye - (a_kb_bd[ch] * mb_ref[_M_OFF0]).astype(F32)

    for lvl in range(1, N_LEVELS):
        inv_bf = {ch: inv[ch].astype(BF16) for ch in chains}
        mid = {ch: _dot(inv_bf[ch], a_kb_bd[ch] * mb_ref[_M_OFF0 + lvl]).astype(BF16) for ch in chains}
        for ch in chains:
            inv[ch] = inv[ch] - _dot(mid[ch], inv_bf[ch])

    ys = {}
    for ch in chains:
        bi, gi = ch
        rs, sl = rs_of(bi), sl_of(gi)
        inv_lc = inv[ch][0:CHUNK]
        for h in range(1, HEADS_PER_GROUP):
            inv_lc = inv_lc + inv[ch][h * CHUNK:(h + 1) * CHUNK]
        s_bd = st_ref[bi, gi]
        v_bd = block_diag(v_bf[rs, sl])
        from_state = _dot_nt(lhs[ch], s_bd.astype(BF16))
        rhs = from_state[:CHUNK] + _dot(a_kk[ch], v_bd)
        u_bf = _dot(inv_lc.astype(BF16), block_diag(rhs.astype(BF16))).astype(BF16)
        ys[ch] = from_state[CHUNK:] + _dot(a_rk[ch], v_bd) - _dot(a_rb[ch], block_diag(u_bf))

        vu = jnp.concatenate([v_bf[rs, sl], -u_bf], axis=0)
        kb_rem = jnp.concatenate([k_rem[rs, sl], b_rem[rs, sl]], axis=0)
        w_chunk = e_in[(bi + 1) * CHUNK - 1:(bi + 1) * CHUNK, sl]
        st_ref[bi, gi] = s_bd * w_chunk + _dot_tn(vu, kb_rem) * bd

    y = jnp.concatenate(
        [jnp.concatenate([ys[(bi, gi)] for gi in range(N_RWKV_GROUPS)], axis=1) for bi in range(RWKV_BATCH)],
        axis=0)
    inv_n = 1.0 / RWKV_HEAD
    mean = head_sum(y) * inv_n
    yc = y - mean
    var = head_sum(yc * yc) * inv_n
    yn = yc * lax.rsqrt(var + GN_EPS) * lnw_ref[...] + lnb_ref[...]
    bonus = head_sum(r * k * rk_ref[...])
    o_ref[...] = ((yn + bonus * v) * g).reshape(RWKV_BATCH, CHUNK, D_RWKV)
    z_ref[...] = jnp.zeros_like(z_ref)


def _rwkv(pa, mu, w0, a0, wwa, gup, k_k, k_a, r_k, ln_w, ln_b, zero_rows):
    b, s, _ = pa.shape
    n_steps = (b // RWKV_BATCH) * (s // CHUNK)
    z_block = -(-zero_rows // (n_steps * V7X_SUBLANES)) * V7X_SUBLANES
    tril = np.tril(np.ones((CHUNK, CHUNK), np.float32))
    eye_b = np.eye(RWKV_BATCH, dtype=np.float32)
    tri = jnp.asarray(np.concatenate([np.kron(eye_b, tril), np.kron(eye_b, 1.0 - tril)], axis=0), BF16)
    masks = jnp.asarray(_rwkv_masks())
    vec = _const_spec((1, D_RWKV))
    return pl.pallas_call(
        _rwkv_kernel,
        grid=(b // RWKV_BATCH, s // CHUNK),
        in_specs=[
            pl.BlockSpec((RWKV_BATCH, CHUNK, D_RWKV_IN), lambda i, j: (i, j, 0)),
            _const_spec((1, D_RWKV_IN)),
            vec, vec,
            _const_spec((RANK_W + RANK_A, 2 * D_RWKV)),
            _const_spec((RANK_G, D_RWKV)),
            vec, vec, vec, vec, vec,
            _const_spec(tri.shape),
            _const_spec(masks.shape),
            _const_spec(masks.shape),
        ],
        out_specs=[
            pl.BlockSpec((RWKV_BATCH, CHUNK, D_RWKV), lambda i, j: (i, j, 0)),
            pl.BlockSpec((z_block, V7X_LANES), lambda i, j: (i * (s // CHUNK) + j, 0)),
        ],
        out_shape=[
            jax.ShapeDtypeStruct((b, s, D_RWKV), F32),
            jax.ShapeDtypeStruct((z_block * n_steps, V7X_LANES), F32),
        ],
        scratch_shapes=[
            pltpu.VMEM((RWKV_BATCH, V7X_SUBLANES, D_RWKV_IN), F32),
            pltpu.VMEM((RWKV_BATCH, N_RWKV_GROUPS, GROUP_LANES, GROUP_LANES), F32),
        ],
        compiler_params=pltpu.CompilerParams(
            dimension_semantics=("parallel", "arbitrary"), vmem_limit_bytes=V7X_VMEM_LIMIT_BYTES),
        name="rwkv7",
    )(pa, mu, w0, a0, wwa, gup, k_k, k_a, r_k, ln_w, ln_b, tri, masks, masks.astype(BF16))


def _hgrn_sum_matrix():
    t = np.arange(CHUNK)[:, None]
    j = np.arange(CHUNK)[None, :]
    mats = [(j <= t), (j > t)]
    for lvl in range(N_LEVELS):
        s = 1 << lvl
        mid = (t // (2 * s)) * (2 * s) + s - 1
        mats.append(((j > mid) & (j <= t)).astype(np.float32) - ((j > t) & (j <= mid)).astype(np.float32))
    return np.concatenate([np.asarray(m, np.float32) for m in mats], axis=0)


def _hgrn_level_masks():
    t = np.arange(CHUNK)[:, None]
    s_ = np.arange(CHUNK)[None, :]
    masks = [t == s_]
    for lvl in range(N_LEVELS):
        s = 1 << lvl
        masks.append(((t // (2 * s)) == (s_ // (2 * s))) & ((t % (2 * s)) >= s) & ((s_ % (2 * s)) < s))
    return np.stack(masks).astype(np.float32)


def _hgrn_kernel(p_ref, cw_ref, lbl_ref, ng_ref, sm_ref, lm_ref, o_ref, prev_ref, st_ref):
    c = pl.program_id(1)

    @pl.when(c == 0)
    def _():
        prev_ref[...] = jnp.zeros_like(prev_ref)
        st_ref[...] = jnp.zeros_like(st_ref)

    l0 = lbl_ref[0:1, :]
    l1 = lbl_ref[1:2, :]
    lmax = jnp.maximum(l0, l1)
    e0 = jnp.exp(l0 - lmax)
    lb = e0 / (e0 + jnp.exp(l1 - lmax))
    ta = jnp.log(lb)

    q_lv, k_lv, q_inter, k_inter, w_chunk, val_bf = {}, {}, {}, {}, {}, {}
    for bi in range(HGRN_BATCH):
        x = p_ref[bi, :, :3 * D_HGRN]
        xe = jnp.concatenate([prev_ref[bi], x], axis=0)
        prev_ref[bi] = x[CHUNK - V7X_SUBLANES:, :]
        conv = x * cw_ref[CONV_W - 1:CONV_W, :]
        for s in range(1, CONV_W):
            conv = conv + pltpu.roll(xe, s, 0)[V7X_SUBLANES:, :] * cw_ref[CONV_W - 1 - s:CONV_W - s, :]

        qc = conv[:, :D_HGRN]
        f = conv[:, D_HGRN:2 * D_HGRN]
        val_bf[bi] = conv[:, 2 * D_HGRN:].astype(BF16)
        q = qc * _sigmoid(qc)
        tb = jnp.log1p(-lb) + _log_sigmoid(f)
        log_f = jnp.maximum(ta, tb) + _log1p_exp_neg_abs(ta - tb)
        kg = (1.0 - lb) * _sigmoid(-f)

        sums = _dot01_left(sm_ref[...], log_f)
        b_in = sums[0:CHUNK]
        q_inter[bi] = (q * jnp.exp(b_in)).astype(BF16)
        k_inter[bi] = (kg * jnp.exp(sums[CHUNK:2 * CHUNK])).astype(BF16)
        w_chunk[bi] = jnp.exp(b_in[CHUNK - 1:CHUNK, :])
        q_lv[bi] = [q.astype(BF16)]
        k_lv[bi] = [kg.astype(BF16)]
        for lvl in range(N_LEVELS):
            e = jnp.exp(-jnp.abs(sums[(2 + lvl) * CHUNK:(3 + lvl) * CHUNK])).astype(BF16)
            q_lv[bi].append(q_lv[bi][0] * e)
            k_lv[bi].append(k_lv[bi][0] * e)

    chains = [(bi, h) for bi in range(HGRN_BATCH) for h in range(HGRN_HEADS)]
    sl_of = lambda h: slice(h * HGRN_HEAD, (h + 1) * HGRN_HEAD)
    scores = {}
    for lvl in range(N_LEVELS + 1):
        for ch in chains:
            bi, sl = ch[0], sl_of(ch[1])
            term = _dot_nt(q_lv[bi][lvl][:, sl], k_lv[bi][lvl][:, sl]) * lm_ref[lvl]
            scores[ch] = term if lvl == 0 else scores[ch] + term
    outs = {}
    for ch in chains:
        bi, h = ch
        sl = sl_of(h)
        st = st_ref[bi, h]
        o = _dot(scores[ch].astype(BF16), val_bf[bi][:, sl]) + _dot_nt(q_inter[bi][:, sl], st.astype(BF16))
        st_ref[bi, h] = st * w_chunk[bi][:, sl] + _dot_tn(val_bf[bi][:, sl], k_inter[bi][:, sl])
        outs[ch] = o * lax.rsqrt(jnp.mean(o * o, axis=-1, keepdims=True) + RMS_EPS)
    for bi in range(HGRN_BATCH):
        gate = p_ref[bi, :, 3 * D_HGRN:]
        o = jnp.concatenate([outs[(bi, h)] for h in range(HGRN_HEADS)], axis=1)
        o_ref[bi] = o * ng_ref[...] * (gate * _sigmoid(gate))


def _hgrn(pb, conv_w, lb_logits, norm_g):
    b, s, _ = pb.shape
    sum_m = jnp.asarray(_hgrn_sum_matrix(), BF16)
    lvl_m = jnp.asarray(_hgrn_level_masks())
    return pl.pallas_call(
        _hgrn_kernel,
        grid=(b // HGRN_BATCH, s // CHUNK),
        in_specs=[
            pl.BlockSpec((HGRN_BATCH, CHUNK, D_HGRN_IN), lambda i, j: (i, j, 0)),
            _const_spec((CONV_W, 3 * D_HGRN)),
            _const_spec(lb_logits.shape),
            _const_spec((1, D_HGRN)),
            _const_spec(sum_m.shape),
            _const_spec(lvl_m.shape),
        ],
        out_specs=pl.BlockSpec((HGRN_BATCH, CHUNK, D_HGRN), lambda i, j: (i, j, 0)),
        out_shape=jax.ShapeDtypeStruct((b, s, D_HGRN), F32),
        scratch_shapes=[
            pltpu.VMEM((HGRN_BATCH, V7X_SUBLANES, 3 * D_HGRN), F32),
            pltpu.VMEM((HGRN_BATCH, HGRN_HEADS, HGRN_HEAD, HGRN_HEAD), F32),
        ],
        compiler_params=pltpu.CompilerParams(
            dimension_semantics=("parallel", "arbitrary"), vmem_limit_bytes=V7X_VMEM_LIMIT_BYTES),
        name="hgrn2",
    )(pb, conv_w, lb_logits, norm_g, sum_m, lvl_m)


ROUTER_LANES = V7X_LANES


TOKEN_ROWS = V7X_SUBLANES
assert D_MODEL == TOKEN_ROWS * V7X_LANES


def _tiles_shape(tokens):
    return (tokens * TOKEN_ROWS, V7X_LANES)


def _store_token_tiles(ref, x):
    rows = x.shape[0]
    for s in range(TOKEN_ROWS):
        ref[pl.ds(s, rows, stride=TOKEN_ROWS), :] = x[:, s * V7X_LANES:(s + 1) * V7X_LANES]


def _load_token_tiles(ref):
    rows = ref.shape[0] // TOKEN_ROWS
    return jnp.concatenate([ref[pl.ds(s, rows, stride=TOKEN_ROWS), :] for s in range(TOKEN_ROWS)], axis=1)


_R_E1, _R_E2, _R_G1, _R_G2, _R_RANK1, _R_RANK2 = 0, 1, 2, 3, 4, 5


def _outproj_kernel(x_ref, ya_ref, yb_ref, wo_ref, g2_ref, wr_ref, br_ref, tri_ref,
                    h_ref, n_ref, meta_ref, counts_ref, carry_ref):
    @pl.when(pl.program_id(0) == 0)
    def _():
        carry_ref[...] = jnp.zeros_like(carry_ref)

    h = (x_ref[...]
         + _dot(ya_ref[...].astype(BF16), wo_ref[:D_RWKV, :])
         + _dot(yb_ref[...].astype(BF16), wo_ref[D_RWKV:, :]))
    h_ref[...] = h
    n = _rms(h, g2_ref[...])
    _store_token_tiles(n_ref, n)

    n_hi, n_lo = _split_bf16(n)
    logits = (_dot(n_hi, wr_ref[0]) + _dot(n_lo, wr_ref[0]) + _dot(n_hi, wr_ref[1])) + br_ref[...]
    lane = lax.broadcasted_iota(jnp.int32, logits.shape, 1).astype(F32)
    neg = -jnp.inf
    gl = jnp.where((lane >= N_EXPERTS) & (lane < N_EXPERTS + N_GROUPS), logits, neg)
    gmax = jnp.max(gl, axis=-1, keepdims=True)
    gidx = jnp.min(jnp.where(gl == gmax, lane - N_EXPERTS, ROUTER_LANES), axis=-1, keepdims=True)
    g_w = 1.0 / jnp.sum(jnp.exp(gl - gmax), axis=-1, keepdims=True)
    lo = gidx * EXPERTS_PER_GROUP
    el = jnp.where((lane >= lo) & (lane < lo + EXPERTS_PER_GROUP), logits, neg)
    m1 = jnp.max(el, axis=-1, keepdims=True)
    i1 = jnp.min(jnp.where(el == m1, lane, ROUTER_LANES), axis=-1, keepdims=True)
    el2 = jnp.where(lane == i1, neg, el)
    m2 = jnp.max(el2, axis=-1, keepdims=True)
    i2 = jnp.min(jnp.where(el2 == m2, lane, ROUTER_LANES), axis=-1, keepdims=True)
    t = jnp.exp(m2 - m1)
    w1 = 1.0 / (1.0 + t)

    onehot1 = lane == i1
    onehot2 = lane == i2
    assigned = jnp.where(onehot1 | onehot2, 1.0, 0.0)
    before = _dot(tri_ref[...], assigned.astype(BF16)) + carry_ref[...]
    rank1 = jnp.sum(jnp.where(onehot1, before, 0.0), axis=-1, keepdims=True)
    rank2 = jnp.sum(jnp.where(onehot2, before, 0.0), axis=-1, keepdims=True)
    carry_ref[...] += jnp.sum(assigned, axis=0, keepdims=True)
    counts_ref[...] = jnp.broadcast_to(carry_ref[...], counts_ref.shape)

    meta = jnp.zeros_like(logits)
    for idx, val in ((_R_E1, i1), (_R_E2, i2), (_R_G1, g_w * w1), (_R_G2, g_w * (t * w1)),
                     (_R_RANK1, rank1), (_R_RANK2, rank2)):
        meta = jnp.where(lane == idx, val, meta)
    meta_ref[...] = meta


def _outproj(x2d, ya, yb, wo_bf16, g2, wr, br):
    t = x2d.shape[0]
    row = lambda w: pl.BlockSpec((TM_PROJ, w), lambda i: (i, 0))
    tri = jnp.asarray(np.tril(np.ones((TM_PROJ, TM_PROJ), np.float32), -1), BF16)
    return pl.pallas_call(
        _outproj_kernel,
        grid=(t // TM_PROJ,),
        in_specs=[
            row(D_MODEL), row(D_RWKV), row(D_HGRN),
            _const_spec((D_RWKV + D_HGRN, D_MODEL)),
            _const_spec((1, D_MODEL)),
            _const_spec((2, D_MODEL, ROUTER_LANES)),
            _const_spec((1, ROUTER_LANES)),
            _const_spec((TM_PROJ, TM_PROJ)),
        ],
        out_specs=[row(D_MODEL), pl.BlockSpec(_tiles_shape(TM_PROJ), lambda i: (i, 0)), row(ROUTER_LANES),
                   _const_spec((V7X_SUBLANES, ROUTER_LANES))],
        out_shape=[
            jax.ShapeDtypeStruct((t, D_MODEL), F32),
            jax.ShapeDtypeStruct(_tiles_shape(t), F32),
            jax.ShapeDtypeStruct((t, ROUTER_LANES), F32),
            jax.ShapeDtypeStruct((V7X_SUBLANES, ROUTER_LANES), F32),
        ],
        scratch_shapes=[pltpu.VMEM((1, ROUTER_LANES), F32)],
        compiler_params=pltpu.CompilerParams(
            dimension_semantics=("arbitrary",), vmem_limit_bytes=V7X_VMEM_LIMIT_BYTES),
        name="outproj_router",
    )(x2d, ya, yb, wo_bf16, g2, wr, br, tri)


def _slot_kernel(meta_ref, offs_ref, pos_ref):
    meta = meta_ref[...]
    lane = lax.broadcasted_iota(jnp.int32, meta.shape, 1).astype(F32)
    offs = offs_ref[...]

    def slot(e_lane, rank_lane):
        e = meta[:, e_lane:e_lane + 1]
        return jnp.sum(jnp.where(lane == e, offs, 0.0), axis=-1, keepdims=True) + meta[:, rank_lane:rank_lane + 1]

    pos = jnp.where(lane == 0, slot(_R_E1, _R_RANK1), jnp.where(lane == 1, slot(_R_E2, _R_RANK2), 0.0))
    pos_ref[...] = pos.astype(jnp.int32)


def _slots(meta, offs):
    t = meta.shape[0]
    return pl.pallas_call(
        _slot_kernel,
        grid=(t // TM_PROJ,),
        in_specs=[pl.BlockSpec((TM_PROJ, ROUTER_LANES), lambda i: (i, 0)), _const_spec((1, ROUTER_LANES))],
        out_specs=pl.BlockSpec((TM_PROJ, ROUTER_LANES), lambda i: (i, 0)),
        out_shape=jax.ShapeDtypeStruct((t, ROUTER_LANES), jnp.int32),
        compiler_params=pltpu.CompilerParams(dimension_semantics=("parallel",)),
        name="moe_slots",
    )(meta, offs)


def _token_rows(token, count=1):
    return pl.ds(pl.multiple_of(token * TOKEN_ROWS, TOKEN_ROWS), count * TOKEN_ROWS)


def _row_copy(src_ref, src_token, dst_ref, dst_token, sem):
    return pltpu.make_async_copy(src_ref.at[_token_rows(src_token)], dst_ref.at[_token_rows(dst_token)], sem)


def _dispatch_kernel(pos_ref, n_ref, xs_in_ref, xs_ref, sem):
    del xs_in_ref

    def body(r, carry):
        _row_copy(n_ref, r, xs_ref, pos_ref[2 * r], sem).start(priority=0)
        _row_copy(n_ref, r, xs_ref, pos_ref[2 * r + 1], sem).start(priority=1)
        return carry

    lax.fori_loop(0, TM_DISPATCH, body, 0, unroll=8)
    for _ in range(2):
        pltpu.make_async_copy(n_ref, xs_ref.at[_token_rows(0, TM_DISPATCH)], sem).wait()


def _dispatch(pos_flat, n2, xs0):
    t = n2.shape[0] // TOKEN_ROWS
    return pl.pallas_call(
        _dispatch_kernel,
        grid=(t // TM_DISPATCH,),
        in_specs=[
            pl.BlockSpec((2 * TM_DISPATCH,), lambda i: (i,), memory_space=pltpu.SMEM),
            pl.BlockSpec(_tiles_shape(TM_DISPATCH), lambda i: (i, 0)),
            pl.BlockSpec(memory_space=pl.ANY),
        ],
        out_specs=pl.BlockSpec(memory_space=pl.ANY),
        out_shape=jax.ShapeDtypeStruct(xs0.shape, F32),
        scratch_shapes=[pltpu.SemaphoreType.DMA(())],
        input_output_aliases={2: 0},
        compiler_params=pltpu.CompilerParams(
            dimension_semantics=("arbitrary",), has_side_effects=True),
        name="moe_dispatch",
    )(pos_flat, n2, xs0)


def _gmm_kernel(be_ref, nu_ref, xs_ref, wg_ref, wu_ref, wd_ref, y_ref, wgu_s, wd_s):
    b = pl.program_id(0)

    @pl.when(b < nu_ref[0])
    def _():
        @pl.when((b == 0) | (be_ref[b] != be_ref[jnp.maximum(b - 1, 0)]))
        def _():
            wgu_s[:, :D_EXPERT] = wg_ref[...].astype(BF16)
            wgu_s[:, D_EXPERT:] = wu_ref[...].astype(BF16)
            wd_s[...] = wd_ref[...].astype(BF16)

        gu = _dot(_load_token_tiles(xs_ref).astype(BF16), wgu_s[...])
        hg = gu[:, :D_EXPERT]
        hid = (hg * _sigmoid(hg) * gu[:, D_EXPERT:]).astype(BF16)
        _store_token_tiles(y_ref, _dot(hid, wd_s[...]))

    @pl.when(b >= nu_ref[0])
    def _():
        y_ref[...] = jnp.zeros_like(y_ref)


def _gmm(block_expert, n_used, xs, wg, wu, wd):
    n_slots = block_expert.shape[0] * TB_MOE
    last = lambda b, nu: jnp.maximum(jnp.minimum(b, nu[0] - 1), 0)
    blk = lambda b, be, nu: (last(b, nu), 0)
    wsel = lambda b, be, nu: (be[last(b, nu)], 0, 0)
    return pl.pallas_call(
        _gmm_kernel,
        grid_spec=pltpu.PrefetchScalarGridSpec(
            num_scalar_prefetch=2,
            grid=(n_slots // TB_MOE,),
            in_specs=[
                pl.BlockSpec(_tiles_shape(TB_MOE), blk),
                pl.BlockSpec((None, D_MODEL, D_EXPERT), wsel),
                pl.BlockSpec((None, D_MODEL, D_EXPERT), wsel),
                pl.BlockSpec((None, D_EXPERT, D_MODEL), wsel),
            ],
            out_specs=pl.BlockSpec(_tiles_shape(TB_MOE), lambda b, be, nu: (b, 0)),
            scratch_shapes=[
                pltpu.VMEM((D_MODEL, 2 * D_EXPERT), BF16),
                pltpu.VMEM((D_EXPERT, D_MODEL), BF16),
            ],
        ),
        out_shape=jax.ShapeDtypeStruct(_tiles_shape(n_slots), F32),
        compiler_params=pltpu.CompilerParams(
            dimension_semantics=("arbitrary",), vmem_limit_bytes=V7X_VMEM_LIMIT_BYTES),
        name="moe_gmm",
    )(block_expert, n_used, xs, wg, wu, wd)


def _combine_kernel(pos_ref, pos_next_ref, meta_ref, h_ref, gf_ref, ys_ref, o_ref, buf_ref, sem):
    i = pl.program_id(0)
    slot = lax.rem(i, 2)

    def issue(p_ref, s):
        def body(r, carry):
            _row_copy(ys_ref, p_ref[2 * r], buf_ref.at[s, 0], r, sem.at[s]).start(priority=0)
            _row_copy(ys_ref, p_ref[2 * r + 1], buf_ref.at[s, 1], r, sem.at[s]).start(priority=1)
            return carry
        lax.fori_loop(0, TM_DISPATCH, body, 0, unroll=8)

    @pl.when(i == 0)
    def _():
        issue(pos_ref, 0)

    @pl.when(i + 1 < pl.num_programs(0))
    def _():
        issue(pos_next_ref, 1 - slot)

    for j in range(2):
        pltpu.make_async_copy(ys_ref.at[_token_rows(0, TM_DISPATCH)], buf_ref.at[slot, j], sem.at[slot]).wait()

    meta = meta_ref[...]
    g1 = meta[:, _R_G1:_R_G1 + 1]
    g2 = meta[:, _R_G2:_R_G2 + 1]
    moe = g1 * _load_token_tiles(buf_ref.at[slot, 0]) + g2 * _load_token_tiles(buf_ref.at[slot, 1])
    o_ref[...] = _rms(h_ref[...] + moe, gf_ref[...])


def _combine(pos_flat, meta, h, gf, ys):
    t = h.shape[0]
    n_tiles = t // TM_DISPATCH
    return pl.pallas_call(
        _combine_kernel,
        grid=(n_tiles,),
        in_specs=[
            pl.BlockSpec((2 * TM_DISPATCH,), lambda i: (i,), memory_space=pltpu.SMEM),
            pl.BlockSpec((2 * TM_DISPATCH,), lambda i: (jnp.minimum(i + 1, n_tiles - 1),),
                         memory_space=pltpu.SMEM),
            pl.BlockSpec((TM_DISPATCH, ROUTER_LANES), lambda i: (i, 0)),
            pl.BlockSpec((TM_DISPATCH, D_MODEL), lambda i: (i, 0)),
            _const_spec((1, D_MODEL)),
            pl.BlockSpec(memory_space=pl.ANY),
        ],
        out_specs=pl.BlockSpec((TM_DISPATCH, D_MODEL), lambda i: (i, 0)),
        out_shape=jax.ShapeDtypeStruct((t, D_MODEL), F32),
        scratch_shapes=[
            pltpu.VMEM((2, 2) + _tiles_shape(TM_DISPATCH), F32),
            pltpu.SemaphoreType.DMA((2,)),
        ],
        compiler_params=pltpu.CompilerParams(
            dimension_semantics=("arbitrary",), vmem_limit_bytes=V7X_VMEM_LIMIT_BYTES),
        name="moe_combine",
    )(pos_flat, pos_flat, meta, h, gf, ys)


def _moe_blocks(t):
    return (2 * t + N_EXPERTS * (TB_MOE - 1) + TB_MOE - 1) // TB_MOE


def _moe(n2, meta, counts, h, wg, wu, wd, gf, xs0):
    n_blocks = _moe_blocks(h.shape[0])
    cnt = counts[0, :N_EXPERTS].astype(jnp.int32)
    blocks = (cnt + TB_MOE - 1) // TB_MOE
    ends = jnp.cumsum(blocks)
    offs = ((ends - blocks) * TB_MOE).astype(F32)
    offs_row = jnp.zeros((1, ROUTER_LANES), F32).at[0, :N_EXPERTS].set(offs)
    block_ids = jnp.arange(n_blocks, dtype=jnp.int32)
    block_expert = jnp.minimum(
        jnp.sum((ends[None, :] <= block_ids[:, None]).astype(jnp.int32), axis=1), N_EXPERTS - 1)
    n_used = ends[-1:].astype(jnp.int32)

    pos = _slots(meta, offs_row)
    pos_flat = pos[:, :2].reshape(-1)
    xs = _dispatch(pos_flat, n2, xs0)
    ys = _gmm(block_expert, n_used, xs, wg, wu, wd)
    return _combine(pos_flat, meta, h, gf, ys)


def kernel(x, norm1_g, w_in, rwkv_mu, rwkv_w0, rwkv_w_up, rwkv_a0, rwkv_a_up, rwkv_g_up, rwkv_k_k,
           rwkv_k_a, rwkv_r_k, rwkv_ln_w, rwkv_ln_b, hgrn_conv_w, hgrn_lb_logits, hgrn_norm_g, w_out,
           norm2_g, router_g_w, router_g_b, router_e_w, router_e_b, exp_w_gate, exp_w_up, exp_w_down,
           final_norm_g):
    b, s, d = x.shape
    t = b * s
    l = 0
    x2d = x.reshape(t, d)
    row = lambda a: a.reshape(1, -1)

    pa, pb = _inproj(x2d, row(norm1_g[l]), w_in[l].astype(BF16))

    wwa = jnp.zeros((RANK_W + RANK_A, 2 * D_RWKV), F32)
    wwa = wwa.at[:RANK_W, :D_RWKV].set(rwkv_w_up[l]).at[RANK_W:, D_RWKV:].set(rwkv_a_up[l]).astype(BF16)
    ya, xs0 = _rwkv(pa.reshape(b, s, D_RWKV_IN), row(rwkv_mu[l]), row(rwkv_w0[l]), row(rwkv_a0[l]), wwa,
                    rwkv_g_up[l].astype(BF16), row(rwkv_k_k[l]), row(rwkv_k_a[l]), row(rwkv_r_k[l]),
                    row(rwkv_ln_w[l]), row(rwkv_ln_b[l]), _moe_blocks(t) * TB_MOE * TOKEN_ROWS)
    yb = _hgrn(pb.reshape(b, s, D_HGRN_IN), hgrn_conv_w[l], hgrn_lb_logits, row(hgrn_norm_g[l]))

    wr = jnp.zeros((D_MODEL, ROUTER_LANES), F32)
    wr = wr.at[:, :N_EXPERTS].set(router_e_w[l]).at[:, N_EXPERTS:N_EXPERTS + N_GROUPS].set(router_g_w[l])
    br = jnp.zeros((1, ROUTER_LANES), F32)
    br = br.at[0, :N_EXPERTS].set(router_e_b[l]).at[0, N_EXPERTS:N_EXPERTS + N_GROUPS].set(router_g_b[l])
    wr_hi = wr.astype(BF16)
    wr_split = jnp.stack([wr_hi, (wr - wr_hi.astype(F32)).astype(BF16)])
    h, n2, meta, counts = _outproj(x2d, ya.reshape(t, D_RWKV), yb.reshape(t, D_HGRN),
                                   w_out[l].astype(BF16), row(norm2_g[l]), wr_split, br)

    out = _moe(n2, meta, counts, h, exp_w_gate[l], exp_w_up[l], exp_w_down[l], row(final_norm_g), xs0)
    return out.reshape(b, s, d)
```

```python
import functools

import numpy as np
import jax
import jax.numpy as jnp
from jax import lax
from jax.experimental import pallas as pl
from jax.experimental.pallas import tpu as pltpu

F32 = jnp.float32
BF16 = jnp.bfloat16

D_MODEL = 1024
D_RWKV = 512
D_HGRN = 512
RWKV_HEAD = 64
RANK_W = 64
RANK_A = 64
RANK_G = 128
HGRN_HEAD = 128
HGRN_HEADS = D_HGRN // HGRN_HEAD
CONV_W = 4
N_GROUPS = 4
EXPERTS_PER_GROUP = 8
N_EXPERTS = N_GROUPS * EXPERTS_PER_GROUP
D_EXPERT = 512
RMS_EPS = 1e-6
GN_EPS = 64e-5
L2_EPS = 1e-12
D_RWKV_IN = 3 * D_RWKV + RANK_W + RANK_A + RANK_G
D_HGRN_IN = 4 * D_HGRN

V7X_LANES = 128
V7X_SUBLANES = 8
V7X_MXU_DIM = 256
V7X_VMEM_LIMIT_BYTES = 56 * 1024 * 1024

CHUNK = 64
GROUP_LANES = V7X_MXU_DIM
HEADS_PER_GROUP = GROUP_LANES // RWKV_HEAD
N_RWKV_GROUPS = D_RWKV // GROUP_LANES
N_LEVELS = 6
HGRN_BATCH = 2
RWKV_BATCH = 4
TM_PROJ = 512
TM_SLOTS = 2048
TM_DISPATCH = 512
TB_MOE = 512


def _dot(a, b):
    return jnp.dot(a, b, preferred_element_type=F32)


def _dot_nt(a, b):
    return lax.dot_general(a, b, (((1,), (1,)), ((), ())), preferred_element_type=F32)


def _dot_tn(a, b):
    return lax.dot_general(a, b, (((0,), (0,)), ((), ())), preferred_element_type=F32)


def _split_bf16(x):
    hi = x.astype(BF16)
    lo = (x - hi.astype(F32)).astype(BF16)
    return hi, lo


def _dot01_left(m01, x):
    hi, lo = _split_bf16(x)
    return _dot(m01, hi) + _dot(m01, lo)


def _log1p_exp_neg_abs(x):
    return jnp.log1p(jnp.exp(-jnp.abs(x)))


def _log_sigmoid(x):
    return jnp.minimum(x, 0.0) - _log1p_exp_neg_abs(x)


def _sigmoid(x):
    return 1.0 / (1.0 + jnp.exp(-x))


def _rms(x, gain):
    return x * lax.rsqrt(jnp.mean(x * x, axis=-1, keepdims=True) + RMS_EPS) * gain


def _const_spec(shape):
    nd = len(shape)
    return pl.BlockSpec(shape, lambda *_: (0,) * nd)


def _inproj_kernel(x_ref, g_ref, w_ref, pa_ref, pb_ref):
    n = _rms(x_ref[...], g_ref[...]).astype(BF16)
    pa_ref[...] = _dot(n, w_ref[:, :D_RWKV_IN])
    pb_ref[...] = _dot(n, w_ref[:, D_RWKV_IN:])


def _inproj(x2d, gain, w_bf16):
    t = x2d.shape[0]
    return pl.pallas_call(
        _inproj_kernel,
        grid=(t // TM_PROJ,),
        in_specs=[
            pl.BlockSpec((TM_PROJ, D_MODEL), lambda i: (i, 0)),
            _const_spec((1, D_MODEL)),
            _const_spec((D_MODEL, D_RWKV_IN + D_HGRN_IN)),
        ],
        out_specs=[
            pl.BlockSpec((TM_PROJ, D_RWKV_IN), lambda i: (i, 0)),
            pl.BlockSpec((TM_PROJ, D_HGRN_IN), lambda i: (i, 0)),
        ],
        out_shape=[
            jax.ShapeDtypeStruct((t, D_RWKV_IN), F32),
            jax.ShapeDtypeStruct((t, D_HGRN_IN), F32),
        ],
        compiler_params=pltpu.CompilerParams(
            dimension_semantics=("parallel",), vmem_limit_bytes=V7X_VMEM_LIMIT_BYTES),
        name="inproj",
    )(x2d, gain, w_bf16)


_M_EYE, _M_BD, _M_STRICT, _M_INCL, _M_OFF0 = 0, 1, 2, 3, 4


def _rwkv_masks():
    i = np.arange(GROUP_LANES)[:, None]
    j = np.arange(GROUP_LANES)[None, :]
    bd = (i // CHUNK) == (j // CHUNK)
    masks = [i == j, bd, (j % CHUNK) < (i % CHUNK), (j % CHUNK) <= (i % CHUNK)]
    for lvl in range(N_LEVELS):
        s = 1 << lvl
        masks.append(((i // (2 * s)) == (j // (2 * s))) & ((i % (2 * s)) >= s) & ((j % (2 * s)) < s))
    return np.stack(masks).astype(np.float32)


def _tile_rows(x, n):
    return jnp.concatenate([x] * n, axis=0)


def _rwkv_kernel(p_ref, mu_ref, w0_ref, a0_ref, wwa_ref, gup_ref, kk_ref, ka_ref, rk_ref, lnw_ref,
                 lnb_ref, tri_ref, m_ref, mb_ref, o_ref, z_ref, prev_ref, st_ref):
    c = pl.program_id(1)

    @pl.when(c == 0)
    def _():
        prev_ref[...] = jnp.zeros_like(prev_ref)
        st_ref[...] = jnp.zeros_like(st_ref)

    rows = RWKV_BATCH * CHUNK
    p = p_ref[...].reshape(rows, D_RWKV_IN)
    row = lax.broadcasted_iota(jnp.int32, p.shape, 0)
    shifted = pltpu.roll(p, 1, 0)
    for bi in range(RWKV_BATCH):
        shifted = jnp.where(row == bi * CHUNK, prev_ref[bi, V7X_SUBLANES - 1:V7X_SUBLANES, :], shifted)
        prev_ref[bi] = p[(bi + 1) * CHUNK - V7X_SUBLANES:(bi + 1) * CHUNK, :]
    p = p + mu_ref[...] * (shifted - p)

    r = p[:, 0:D_RWKV]
    k = p[:, D_RWKV:2 * D_RWKV]
    v = p[:, 2 * D_RWKV:3 * D_RWKV]
    x_wa = p[:, 3 * D_RWKV:3 * D_RWKV + RANK_W + RANK_A]
    dg = p[:, 3 * D_RWKV + RANK_W + RANK_A:]

    lane = lax.broadcasted_iota(jnp.int32, x_wa.shape, 1)
    t_wa = jnp.where(lane < RANK_W, jnp.tanh(x_wa), x_wa).astype(BF16)
    wa = _dot(t_wa, wwa_ref[...])
    w_lin = w0_ref[...] + wa[:, :D_RWKV]
    w = _log_sigmoid(w_lin) - 0.5
    log_decay = -jnp.exp(w)
    a = _sigmoid(a0_ref[...] + wa[:, D_RWKV:])
    g = _dot(_sigmoid(dg).astype(BF16), gup_ref[...])

    bd_bf = mb_ref[_M_BD]

    def head_sum(x):
        n = N_RWKV_GROUPS
        parts = [half[:, gi * GROUP_LANES:(gi + 1) * GROUP_LANES] for half in _split_bf16(x) for gi in range(n)]
        s = _dot(jnp.concatenate(parts, axis=0), bd_bf)
        return jnp.concatenate(
            [s[gi * rows:(gi + 1) * rows] + s[(n + gi) * rows:(n + gi + 1) * rows] for gi in range(n)], axis=1)

    kk = k * kk_ref[...]
    kk = kk / jnp.maximum(jnp.sqrt(head_sum(kk * kk)), L2_EPS)
    k = k * (1.0 + (a - 1.0) * ka_ref[...])
    kb = kk * a

    sums = _dot01_left(tri_ref[...], log_decay)
    cum = sums[:rows]
    e_in = jnp.exp(cum)
    e_ex = jnp.exp(cum - log_decay)
    e_inv = jnp.exp(-cum)
    e_rem = jnp.exp(sums[rows:])

    r_hat = (r * e_in).astype(BF16)
    kk_hat = (kk * e_ex).astype(BF16)
    k_inv = (k * e_inv).astype(BF16)
    b_inv = (kb * e_inv).astype(BF16)
    k_rem = (k * e_rem).astype(BF16)
    b_rem = (kb * e_rem).astype(BF16)
    v_bf = v.astype(BF16)

    eye = m_ref[_M_EYE]
    bd = m_ref[_M_BD]
    strict_lc = m_ref[_M_STRICT][:CHUNK]
    incl_lc = m_ref[_M_INCL][:CHUNK]

    def block_diag(x_bf):
        return _tile_rows(x_bf, HEADS_PER_GROUP) * bd_bf

    chains = [(bi, gi) for bi in range(RWKV_BATCH) for gi in range(N_RWKV_GROUPS)]
    rs_of = lambda bi: slice(bi * CHUNK, (bi + 1) * CHUNK)
    sl_of = lambda gi: slice(gi * GROUP_LANES, (gi + 1) * GROUP_LANES)

    lhs, a_kk, a_rk, a_rb, a_kb_bd, inv = {}, {}, {}, {}, {}, {}
    for ch in chains:
        rs, sl = rs_of(ch[0]), sl_of(ch[1])
        lhs[ch] = jnp.concatenate([kk_hat[rs, sl], r_hat[rs, sl]], axis=0)
        a_k = _dot_nt(lhs[ch], block_diag(k_inv[rs, sl]))
        a_b = _dot_nt(lhs[ch], block_diag(b_inv[rs, sl]))
        a_kk[ch] = (a_k[:CHUNK] * strict_lc).astype(BF16)
        a_rk[ch] = (a_k[CHUNK:] * incl_lc).astype(BF16)
        a_rb[ch] = (a_b[CHUNK:] * incl_lc).astype(BF16)
        a_kb_bd[ch] = block_diag((a_b[:CHUNK] * strict_lc).astype(BF16))
        inv[ch] = eye - (a_kb_bd[ch] * mb_ref[_M_OFF0]).astype(F32)

    for lvl in range(1, N_LEVELS):
        inv_bf = {ch: inv[ch].astype(BF16) for ch in chains}
        mid = {ch: _dot(inv_bf[ch], a_kb_bd[ch] * mb_ref[_M_OFF0 + lvl]).astype(BF16) for ch in chains}
        for ch in chains:
            inv[ch] = inv[ch] - _dot(mid[ch], inv_bf[ch])

    ys = {}
    for ch in chains:
        bi, gi = ch
        rs, sl = rs_of(bi), sl_of(gi)
        inv_lc = inv[ch][0:CHUNK]
        for h in range(1, HEADS_PER_GROUP):
            inv_lc = inv_lc + inv[ch][h * CHUNK:(h + 1) * CHUNK]
        s_bd = st_ref[bi, gi]
        v_bd = block_diag(v_bf[rs, sl])
        from_state = _dot_nt(lhs[ch], s_bd.astype(BF16))
        rhs = from_state[:CHUNK] + _dot(a_kk[ch], v_bd)
        u_bf = _dot(inv_lc.astype(BF16), block_diag(rhs.astype(BF16))).astype(BF16)
        ys[ch] = from_state[CHUNK:] + _dot(a_rk[ch], v_bd) - _dot(a_rb[ch], block_diag(u_bf))

        vu = jnp.concatenate([v_bf[rs, sl], -u_bf], axis=0)
        kb_rem = jnp.concatenate([k_rem[rs, sl], b_rem[rs, sl]], axis=0)
        w_chunk = e_in[(bi + 1) * CHUNK - 1:(bi + 1) * CHUNK, sl]
        st_ref[bi, gi] = s_bd * w_chunk + _dot_tn(vu, kb_rem) * bd

    y = jnp.concatenate(
        [jnp.concatenate([ys[(bi, gi)] for gi in range(N_RWKV_GROUPS)], axis=1) for bi in range(RWKV_BATCH)],
        axis=0)
    inv_n = 1.0 / RWKV_HEAD
    mean = head_sum(y) * inv_n
    yc = y - mean
    var = head_sum(yc * yc) * inv_n
    yn = yc * lax.rsqrt(var + GN_EPS) * lnw_ref[...] + lnb_ref[...]
    bonus = head_sum(r * k * rk_ref[...])
    o_ref[...] = ((yn + bonus * v) * g).reshape(RWKV_BATCH, CHUNK, D_RWKV)
    z_ref[...] = jnp.zeros_like(z_ref)


def _rwkv(pa, mu, w0, a0, wwa, gup, k_k, k_a, r_k, ln_w, ln_b, zero_rows):
    b, s, _ = pa.shape
    n_steps = (b // RWKV_BATCH) * (s // CHUNK)
    z_block = -(-zero_rows // (n_steps * V7X_SUBLANES)) * V7X_SUBLANES
    tril = np.tril(np.ones((CHUNK, CHUNK), np.float32))
    eye_b = np.eye(RWKV_BATCH, dtype=np.float32)
    tri = jnp.asarray(np.concatenate([np.kron(eye_b, tril), np.kron(eye_b, 1.0 - tril)], axis=0), BF16)
    masks = jnp.asarray(_rwkv_masks())
    vec = _const_spec((1, D_RWKV))
    return pl.pallas_call(
        _rwkv_kernel,
        grid=(b // RWKV_BATCH, s // CHUNK),
        in_specs=[
            pl.BlockSpec((RWKV_BATCH, CHUNK, D_RWKV_IN), lambda i, j: (i, j, 0)),
            _const_spec((1, D_RWKV_IN)),
            vec, vec,
            _const_spec((RANK_W + RANK_A, 2 * D_RWKV)),
            _const_spec((RANK_G, D_RWKV)),
            vec, vec, vec, vec, vec,
            _const_spec(tri.shape),
            _const_spec(masks.shape),
            _const_spec(masks.shape),
        ],
        out_specs=[
            pl.BlockSpec((RWKV_BATCH, CHUNK, D_RWKV), lambda i, j: (i, j, 0)),
            pl.BlockSpec((z_block, V7X_LANES), lambda i, j: (i * (s // CHUNK) + j, 0)),
        ],
        out_shape=[
            jax.ShapeDtypeStruct((b, s, D_RWKV), F32),
            jax.ShapeDtypeStruct((z_block * n_steps, V7X_LANES), F32),
        ],
        scratch_shapes=[
            pltpu.VMEM((RWKV_BATCH, V7X_SUBLANES, D_RWKV_IN), F32),
            pltpu.VMEM((RWKV_BATCH, N_RWKV_GROUPS, GROUP_LANES, GROUP_LANES), F32),
        ],
        compiler_params=pltpu.CompilerParams(
            dimension_semantics=("parallel", "arbitrary"), vmem_limit_bytes=V7X_VMEM_LIMIT_BYTES),
        name="rwkv7",
    )(pa, mu, w0, a0, wwa, gup, k_k, k_a, r_k, ln_w, ln_b, tri, masks, masks.astype(BF16))


def _hgrn_sum_matrix():
    t = np.arange(CHUNK)[:, None]
    j = np.arange(CHUNK)[None, :]
    mats = [(j <= t), (j > t)]
    for lvl in range(N_LEVELS):
        s = 1 << lvl
        mid = (t // (2 * s)) * (2 * s) + s - 1
        mats.append(((j > mid) & (j <= t)).astype(np.float32) - ((j > t) & (j <= mid)).astype(np.float32))
    return np.concatenate([np.asarray(m, np.float32) for m in mats], axis=0)


def _hgrn_level_masks():
    t = np.arange(CHUNK)[:, None]
    s_ = np.arange(CHUNK)[None, :]
    masks = [t == s_]
    for lvl in range(N_LEVELS):
        s = 1 << lvl
        masks.append(((t // (2 * s)) == (s_ // (2 * s))) & ((t % (2 * s)) >= s) & ((s_ % (2 * s)) < s))
    return np.stack(masks).astype(np.float32)


def _hgrn_kernel(p_ref, cw_ref, lbl_ref, ng_ref, sm_ref, lm_ref, o_ref, prev_ref, st_ref):
    c = pl.program_id(1)

    @pl.when(c == 0)
    def _():
        prev_ref[...] = jnp.zeros_like(prev_ref)
        st_ref[...] = jnp.zeros_like(st_ref)

    l0 = lbl_ref[0:1, :]
    l1 = lbl_ref[1:2, :]
    lmax = jnp.maximum(l0, l1)
    e0 = jnp.exp(l0 - lmax)
    lb = e0 / (e0 + jnp.exp(l1 - lmax))
    ta = jnp.log(lb)

    q_lv, k_lv, q_inter, k_inter, w_chunk, val_bf = {}, {}, {}, {}, {}, {}
    for bi in range(HGRN_BATCH):
        x = p_ref[bi, :, :3 * D_HGRN]
        xe = jnp.concatenate([prev_ref[bi], x], axis=0)
        prev_ref[bi] = x[CHUNK - V7X_SUBLANES:, :]
        conv = x * cw_ref[CONV_W - 1:CONV_W, :]
        for s in range(1, CONV_W):
            conv = conv + pltpu.roll(xe, s, 0)[V7X_SUBLANES:, :] * cw_ref[CONV_W - 1 - s:CONV_W - s, :]

        qc = conv[:, :D_HGRN]
        f = conv[:, D_HGRN:2 * D_HGRN]
        val_bf[bi] = conv[:, 2 * D_HGRN:].astype(BF16)
        q = qc * _sigmoid(qc)
        tb = jnp.log1p(-lb) + _log_sigmoid(f)
        log_f = jnp.maximum(ta, tb) + _log1p_exp_neg_abs(ta - tb)
        kg = (1.0 - lb) * _sigmoid(-f)

        sums = _dot01_left(sm_ref[...], log_f)
        b_in = sums[0:CHUNK]
        q_inter[bi] = (q * jnp.exp(b_in)).astype(BF16)
        k_inter[bi] = (kg * jnp.exp(sums[CHUNK:2 * CHUNK])).astype(BF16)
        w_chunk[bi] = jnp.exp(b_in[CHUNK - 1:CHUNK, :])
        q_lv[bi] = [q.astype(BF16)]
        k_lv[bi] = [kg.astype(BF16)]
        for lvl in range(N_LEVELS):
            e = jnp.exp(-jnp.abs(sums[(2 + lvl) * CHUNK:(3 + lvl) * CHUNK])).astype(BF16)
            q_lv[bi].append(q_lv[bi][0] * e)
            k_lv[bi].append(k_lv[bi][0] * e)

    chains = [(bi, h) for bi in range(HGRN_BATCH) for h in range(HGRN_HEADS)]
    sl_of = lambda h: slice(h * HGRN_HEAD, (h + 1) * HGRN_HEAD)
    scores = {}
    for lvl in range(N_LEVELS + 1):
        for ch in chains:
            bi, sl = ch[0], sl_of(ch[1])
            term = _dot_nt(q_lv[bi][lvl][:, sl], k_lv[bi][lvl][:, sl]) * lm_ref[lvl]
            scores[ch] = term if lvl == 0 else scores[ch] + term
    outs = {}
    for ch in chains:
        bi, h = ch
        sl = sl_of(h)
        st = st_ref[bi, h]
        o = _dot(scores[ch].astype(BF16), val_bf[bi][:, sl]) + _dot_nt(q_inter[bi][:, sl], st.astype(BF16))
        st_ref[bi, h] = st * w_chunk[bi][:, sl] + _dot_tn(val_bf[bi][:, sl], k_inter[bi][:, sl])
        outs[ch] = o * lax.rsqrt(jnp.mean(o * o, axis=-1, keepdims=True) + RMS_EPS)
    for bi in range(HGRN_BATCH):
        gate = p_ref[bi, :, 3 * D_HGRN:]
        o = jnp.concatenate([outs[(bi, h)] for h in range(HGRN_HEADS)], axis=1)
        o_ref[bi] = o * ng_ref[...] * (gate * _sigmoid(gate))


def _hgrn(pb, conv_w, lb_logits, norm_g):
    b, s, _ = pb.shape
    sum_m = jnp.asarray(_hgrn_sum_matrix(), BF16)
    lvl_m = jnp.asarray(_hgrn_level_masks())
    return pl.pallas_call(
        _hgrn_kernel,
        grid=(b // HGRN_BATCH, s // CHUNK),
        in_specs=[
            pl.BlockSpec((HGRN_BATCH, CHUNK, D_HGRN_IN), lambda i, j: (i, j, 0)),
            _const_spec((CONV_W, 3 * D_HGRN)),
            _const_spec(lb_logits.shape),
            _const_spec((1, D_HGRN)),
            _const_spec(sum_m.shape),
            _const_spec(lvl_m.shape),
        ],
        out_specs=pl.BlockSpec((HGRN_BATCH, CHUNK, D_HGRN), lambda i, j: (i, j, 0)),
        out_shape=jax.ShapeDtypeStruct((b, s, D_HGRN), F32),
        scratch_shapes=[
            pltpu.VMEM((HGRN_BATCH, V7X_SUBLANES, 3 * D_HGRN), F32),
            pltpu.VMEM((HGRN_BATCH, HGRN_HEADS, HGRN_HEAD, HGRN_HEAD), F32),
        ],
        compiler_params=pltpu.CompilerParams(
            dimension_semantics=("parallel", "arbitrary"), vmem_limit_bytes=V7X_VMEM_LIMIT_BYTES),
        name="hgrn2",
    )(pb, conv_w, lb_logits, norm_g, sum_m, lvl_m)


ROUTER_LANES = V7X_LANES


TOKEN_ROWS = V7X_SUBLANES
assert D_MODEL == TOKEN_ROWS * V7X_LANES


def _tiles_shape(tokens):
    return (tokens * TOKEN_ROWS, V7X_LANES)


def _store_token_tiles(ref, x):
    rows = x.shape[0]
    for s in range(TOKEN_ROWS):
        ref[pl.ds(s, rows, stride=TOKEN_ROWS), :] = x[:, s * V7X_LANES:(s + 1) * V7X_LANES]


def _load_token_tiles(ref):
    rows = ref.shape[0] // TOKEN_ROWS
    return jnp.concatenate([ref[pl.ds(s, rows, stride=TOKEN_ROWS), :] for s in range(TOKEN_ROWS)], axis=1)


_R_E1, _R_E2, _R_G1, _R_G2, _R_RANK1, _R_RANK2 = 0, 1, 2, 3, 4, 5


def _outproj_kernel(x_ref, ya_ref, yb_ref, wo_ref, g2_ref, wr_ref, br_ref, tri_ref,
                    h_ref, n_ref, meta_ref, counts_ref, carry_ref):
    @pl.when(pl.program_id(0) == 0)
    def _():
        carry_ref[...] = jnp.zeros_like(carry_ref)

    h = (x_ref[...]
         + _dot(ya_ref[...].astype(BF16), wo_ref[:D_RWKV, :])
         + _dot(yb_ref[...].astype(BF16), wo_ref[D_RWKV:, :]))
    h_ref[...] = h
    n = _rms(h, g2_ref[...])
    _store_token_tiles(n_ref, n)

    n_hi, n_lo = _split_bf16(n)
    logits = (_dot(n_hi, wr_ref[0]) + _dot(n_lo, wr_ref[0]) + _dot(n_hi, wr_ref[1])) + br_ref[...]
    lane = lax.broadcasted_iota(jnp.int32, logits.shape, 1).astype(F32)
    neg = -jnp.inf
    gl = jnp.where((lane >= N_EXPERTS) & (lane < N_EXPERTS + N_GROUPS), logits, neg)
    gmax = jnp.max(gl, axis=-1, keepdims=True)
    gidx = jnp.min(jnp.where(gl == gmax, lane - N_EXPERTS, ROUTER_LANES), axis=-1, keepdims=True)
    g_w = 1.0 / jnp.sum(jnp.exp(gl - gmax), axis=-1, keepdims=True)
    lo = gidx * EXPERTS_PER_GROUP
    el = jnp.where((lane >= lo) & (lane < lo + EXPERTS_PER_GROUP), logits, neg)
    m1 = jnp.max(el, axis=-1, keepdims=True)
    i1 = jnp.min(jnp.where(el == m1, lane, ROUTER_LANES), axis=-1, keepdims=True)
    el2 = jnp.where(lane == i1, neg, el)
    m2 = jnp.max(el2, axis=-1, keepdims=True)
    i2 = jnp.min(jnp.where(el2 == m2, lane, ROUTER_LANES), axis=-1, keepdims=True)
    t = jnp.exp(m2 - m1)
    w1 = 1.0 / (1.0 + t)

    onehot1 = lane == i1
    onehot2 = lane == i2
    assigned = jnp.where(onehot1 | onehot2, 1.0, 0.0)
    before = _dot(tri_ref[...], assigned.astype(BF16)) + carry_ref[...]
    rank1 = jnp.sum(jnp.where(onehot1, before, 0.0), axis=-1, keepdims=True)
    rank2 = jnp.sum(jnp.where(onehot2, before, 0.0), axis=-1, keepdims=True)
    carry_ref[...] += jnp.sum(assigned, axis=0, keepdims=True)
    counts_ref[...] = jnp.broadcast_to(carry_ref[...], counts_ref.shape)

    meta = jnp.zeros_like(logits)
    for idx, val in ((_R_E1, i1), (_R_E2, i2), (_R_G1, g_w * w1), (_R_G2, g_w * (t * w1)),
                     (_R_RANK1, rank1), (_R_RANK2, rank2)):
        meta = jnp.where(lane == idx, val, meta)
    meta_ref[...] = meta


def _outproj(x2d, ya, yb, wo_bf16, g2, wr, br):
    t = x2d.shape[0]
    row = lambda w: pl.BlockSpec((TM_PROJ, w), lambda i: (i, 0))
    tri = jnp.asarray(np.tril(np.ones((TM_PROJ, TM_PROJ), np.float32), -1), BF16)
    return pl.pallas_call(
        _outproj_kernel,
        grid=(t // TM_PROJ,),
        in_specs=[
            row(D_MODEL), row(D_RWKV), row(D_HGRN),
            _const_spec((D_RWKV + D_HGRN, D_MODEL)),
            _const_spec((1, D_MODEL)),
            _const_spec((2, D_MODEL, ROUTER_LANES)),
            _const_spec((1, ROUTER_LANES)),
            _const_spec((TM_PROJ, TM_PROJ)),
        ],
        out_specs=[row(D_MODEL), pl.BlockSpec(_tiles_shape(TM_PROJ), lambda i: (i, 0)), row(ROUTER_LANES),
                   _const_spec((V7X_SUBLANES, ROUTER_LANES))],
        out_shape=[
            jax.ShapeDtypeStruct((t, D_MODEL), F32),
            jax.ShapeDtypeStruct(_tiles_shape(t), F32),
            jax.ShapeDtypeStruct((t, ROUTER_LANES), F32),
            jax.ShapeDtypeStruct((V7X_SUBLANES, ROUTER_LANES), F32),
        ],
        scratch_shapes=[pltpu.VMEM((1, ROUTER_LANES), F32)],
        compiler_params=pltpu.CompilerParams(
            dimension_semantics=("arbitrary",), vmem_limit_bytes=V7X_VMEM_LIMIT_BYTES),
        name="outproj_router",
    )(x2d, ya, yb, wo_bf16, g2, wr, br, tri)


def _slot_kernel(meta_ref, offs_ref, pos_ref):
    meta = meta_ref[...]
    lane = lax.broadcasted_iota(jnp.int32, meta.shape, 1).astype(F32)
    offs = offs_ref[...]

    def slot(e_lane, rank_lane):
        e = meta[:, e_lane:e_lane + 1]
        return jnp.sum(jnp.where(lane == e, offs, 0.0), axis=-1, keepdims=True) + meta[:, rank_lane:rank_lane + 1]

    pos = jnp.concatenate([slot(_R_E1, _R_RANK1), slot(_R_E2, _R_RANK2)], axis=1)
    pos_ref[...] = pos.astype(jnp.int32)


def _slots(meta, offs):
    t = meta.shape[0]
    return pl.pallas_call(
        _slot_kernel,
        grid=(t // TM_SLOTS,),
        in_specs=[pl.BlockSpec((TM_SLOTS, ROUTER_LANES), lambda i: (i, 0)), _const_spec((1, ROUTER_LANES))],
        out_specs=pl.BlockSpec((TM_SLOTS, 2), lambda i: (i, 0)),
        out_shape=jax.ShapeDtypeStruct((t, 2), jnp.int32),
        compiler_params=pltpu.CompilerParams(dimension_semantics=("parallel",)),
        name="moe_slots",
    )(meta, offs)


def _token_rows(token, count=1):
    return pl.ds(pl.multiple_of(token * TOKEN_ROWS, TOKEN_ROWS), count * TOKEN_ROWS)


def _row_copy(src_ref, src_token, dst_ref, dst_token, sem):
    return pltpu.make_async_copy(src_ref.at[_token_rows(src_token)], dst_ref.at[_token_rows(dst_token)], sem)


def _dispatch_kernel(pos_ref, n_ref, xs_in_ref, xs_ref, sem):
    del xs_in_ref

    def body(r, carry):
        _row_copy(n_ref, r, xs_ref, pos_ref[2 * r], sem).start(priority=0)
        _row_copy(n_ref, r, xs_ref, pos_ref[2 * r + 1], sem).start(priority=1)
        return carry

    lax.fori_loop(0, TM_DISPATCH, body, 0, unroll=8)
    for _ in range(2):
        pltpu.make_async_copy(n_ref, xs_ref.at[_token_rows(0, TM_DISPATCH)], sem).wait()


def _dispatch(pos_flat, n2, xs0):
    t = n2.shape[0] // TOKEN_ROWS
    return pl.pallas_call(
        _dispatch_kernel,
        grid=(t // TM_DISPATCH,),
        in_specs=[
            pl.BlockSpec((2 * TM_DISPATCH,), lambda i: (i,), memory_space=pltpu.SMEM),
            pl.BlockSpec(_tiles_shape(TM_DISPATCH), lambda i: (i, 0)),
            pl.BlockSpec(memory_space=pl.ANY),
        ],
        out_specs=pl.BlockSpec(memory_space=pl.ANY),
        out_shape=jax.ShapeDtypeStruct(xs0.shape, F32),
        scratch_shapes=[pltpu.SemaphoreType.DMA(())],
        input_output_aliases={2: 0},
        compiler_params=pltpu.CompilerParams(
            dimension_semantics=("arbitrary",), has_side_effects=True),
        name="moe_dispatch",
    )(pos_flat, n2, xs0)


def _gmm_kernel(be_ref, nu_ref, xs_ref, wg_ref, wu_ref, wd_ref, y_ref, wgu_s, wd_s):
    b = pl.program_id(0)

    @pl.when(b < nu_ref[0])
    def _():
        @pl.when((b == 0) | (be_ref[b] != be_ref[jnp.maximum(b - 1, 0)]))
        def _():
            wgu_s[:, :D_EXPERT] = wg_ref[...].astype(BF16)
            wgu_s[:, D_EXPERT:] = wu_ref[...].astype(BF16)
            wd_s[...] = wd_ref[...].astype(BF16)

        gu = _dot(_load_token_tiles(xs_ref).astype(BF16), wgu_s[...])
        hg = gu[:, :D_EXPERT]
        hid = (hg * _sigmoid(hg) * gu[:, D_EXPERT:]).astype(BF16)
        _store_token_tiles(y_ref, _dot(hid, wd_s[...]))

    @pl.when(b >= nu_ref[0])
    def _():
        y_ref[...] = jnp.zeros_like(y_ref)


def _gmm(block_expert, n_used, xs, wg, wu, wd):
    n_slots = block_expert.shape[0] * TB_MOE
    last = lambda b, nu: jnp.maximum(jnp.minimum(b, nu[0] - 1), 0)
    blk = lambda b, be, nu: (last(b, nu), 0)
    wsel = lambda b, be, nu: (be[last(b, nu)], 0, 0)
    return pl.pallas_call(
        _gmm_kernel,
        grid_spec=pltpu.PrefetchScalarGridSpec(
            num_scalar_prefetch=2,
            grid=(n_slots // TB_MOE,),
            in_specs=[
                pl.BlockSpec(_tiles_shape(TB_MOE), blk),
                pl.BlockSpec((None, D_MODEL, D_EXPERT), wsel),
                pl.BlockSpec((None, D_MODEL, D_EXPERT), wsel),
                pl.BlockSpec((None, D_EXPERT, D_MODEL), wsel),
            ],
            out_specs=pl.BlockSpec(_tiles_shape(TB_MOE), lambda b, be, nu: (b, 0)),
            scratch_shapes=[
                pltpu.VMEM((D_MODEL, 2 * D_EXPERT), BF16),
                pltpu.VMEM((D_EXPERT, D_MODEL), BF16),
            ],
        ),
        out_shape=jax.ShapeDtypeStruct(_tiles_shape(n_slots), F32),
        compiler_params=pltpu.CompilerParams(
            dimension_semantics=("arbitrary",), vmem_limit_bytes=V7X_VMEM_LIMIT_BYTES),
        name="moe_gmm",
    )(block_expert, n_used, xs, wg, wu, wd)


def _combine_kernel(pos_ref, pos_next_ref, meta_ref, h_ref, gf_ref, ys_ref, o_ref, buf_ref, sem):
    i = pl.program_id(0)
    slot = lax.rem(i, 2)

    def issue(p_ref, s):
        def body(r, carry):
            _row_copy(ys_ref, p_ref[2 * r], buf_ref.at[s, 0], r, sem.at[s]).start(priority=0)
            _row_copy(ys_ref, p_ref[2 * r + 1], buf_ref.at[s, 1], r, sem.at[s]).start(priority=1)
            return carry
        lax.fori_loop(0, TM_DISPATCH, body, 0, unroll=8)

    @pl.when(i == 0)
    def _():
        issue(pos_ref, 0)

    @pl.when(i + 1 < pl.num_programs(0))
    def _():
        issue(pos_next_ref, 1 - slot)

    for j in range(2):
        pltpu.make_async_copy(ys_ref.at[_token_rows(0, TM_DISPATCH)], buf_ref.at[slot, j], sem.at[slot]).wait()

    meta = meta_ref[...]
    g1 = meta[:, _R_G1:_R_G1 + 1]
    g2 = meta[:, _R_G2:_R_G2 + 1]
    moe = g1 * _load_token_tiles(buf_ref.at[slot, 0]) + g2 * _load_token_tiles(buf_ref.at[slot, 1])
    o_ref[...] = _rms(h_ref[...] + moe, gf_ref[...])


def _combine(pos_flat, meta, h, gf, ys):
    t = h.shape[0]
    n_tiles = t // TM_DISPATCH
    return pl.pallas_call(
        _combine_kernel,
        grid=(n_tiles,),
        in_specs=[
            pl.BlockSpec((2 * TM_DISPATCH,), lambda i: (i,), memory_space=pltpu.SMEM),
            pl.BlockSpec((2 * TM_DISPATCH,), lambda i: (jnp.minimum(i + 1, n_tiles - 1),),
                         memory_space=pltpu.SMEM),
            pl.BlockSpec((TM_DISPATCH, ROUTER_LANES), lambda i: (i, 0)),
            pl.BlockSpec((TM_DISPATCH, D_MODEL), lambda i: (i, 0)),
            _const_spec((1, D_MODEL)),
            pl.BlockSpec(memory_space=pl.ANY),
        ],
        out_specs=pl.BlockSpec((TM_DISPATCH, D_MODEL), lambda i: (i, 0)),
        out_shape=jax.ShapeDtypeStruct((t, D_MODEL), F32),
        scratch_shapes=[
            pltpu.VMEM((2, 2) + _tiles_shape(TM_DISPATCH), F32),
            pltpu.SemaphoreType.DMA((2,)),
        ],
        compiler_params=pltpu.CompilerParams(
            dimension_semantics=("arbitrary",), vmem_limit_bytes=V7X_VMEM_LIMIT_BYTES),
        name="moe_combine",
    )(pos_flat, pos_flat, meta, h, gf, ys)


def _moe_blocks(t):
    return (2 * t + N_EXPERTS * (TB_MOE - 1) + TB_MOE - 1) // TB_MOE


def _moe(n2, meta, counts, h, wg, wu, wd, gf, xs0):
    n_blocks = _moe_blocks(h.shape[0])
    cnt = counts[0, :N_EXPERTS].astype(jnp.int32)
    blocks = (cnt + TB_MOE - 1) // TB_MOE
    ends = jnp.cumsum(blocks)
    offs = ((ends - blocks) * TB_MOE).astype(F32)
    offs_row = jnp.zeros((1, ROUTER_LANES), F32).at[0, :N_EXPERTS].set(offs)
    block_ids = jnp.arange(n_blocks, dtype=jnp.int32)
    block_expert = jnp.minimum(
        jnp.sum((ends[None, :] <= block_ids[:, None]).astype(jnp.int32), axis=1), N_EXPERTS - 1)
    n_used = ends[-1:].astype(jnp.int32)

    pos = _slots(meta, offs_row)
    pos_flat = pos.reshape(-1)
    xs = _dispatch(pos_flat, n2, xs0)
    ys = _gmm(block_expert, n_used, xs, wg, wu, wd)
    return _combine(pos_flat, meta, h, gf, ys)


def kernel(x, norm1_g, w_in, rwkv_mu, rwkv_w0, rwkv_w_up, rwkv_a0, rwkv_a_up, rwkv_g_up, rwkv_k_k,
           rwkv_k_a, rwkv_r_k, rwkv_ln_w, rwkv_ln_b, hgrn_conv_w, hgrn_lb_logits, hgrn_norm_g, w_out,
           norm2_g, router_g_w, router_g_b, router_e_w, router_e_b, exp_w_gate, exp_w_up, exp_w_down,
           final_norm_g):
    b, s, d = x.shape
    t = b * s
    l = 0
    x2d = x.reshape(t, d)
    row = lambda a: a.reshape(1, -1)

    pa, pb = _inproj(x2d, row(norm1_g[l]), w_in[l].astype(BF16))

    wwa = jnp.zeros((RANK_W + RANK_A, 2 * D_RWKV), F32)
    wwa = wwa.at[:RANK_W, :D_RWKV].set(rwkv_w_up[l]).at[RANK_W:, D_RWKV:].set(rwkv_a_up[l]).astype(BF16)
    ya, xs0 = _rwkv(pa.reshape(b, s, D_RWKV_IN), row(rwkv_mu[l]), row(rwkv_w0[l]), row(rwkv_a0[l]), wwa,
                    rwkv_g_up[l].astype(BF16), row(rwkv_k_k[l]), row(rwkv_k_a[l]), row(rwkv_r_k[l]),
                    row(rwkv_ln_w[l]), row(rwkv_ln_b[l]), _moe_blocks(t) * TB_MOE * TOKEN_ROWS)
    yb = _hgrn(pb.reshape(b, s, D_HGRN_IN), hgrn_conv_w[l], hgrn_lb_logits, row(hgrn_norm_g[l]))

    wr = jnp.zeros((D_MODEL, ROUTER_LANES), F32)
    wr = wr.at[:, :N_EXPERTS].set(router_e_w[l]).at[:, N_EXPERTS:N_EXPERTS + N_GROUPS].set(router_g_w[l])
    br = jnp.zeros((1, ROUTER_LANES), F32)
    br = br.at[0, :N_EXPERTS].set(router_e_b[l]).at[0, N_EXPERTS:N_EXPERTS + N_GROUPS].set(router_g_b[l])
    wr_hi = wr.astype(BF16)
    wr_split = jnp.stack([wr_hi, (wr - wr_hi.astype(F32)).astype(BF16)])
    h, n2, meta, counts = _outproj(x2d, ya.reshape(t, D_RWKV), yb.reshape(t, D_HGRN),
                                   w_out[l].astype(BF16), row(norm2_g[l]), wr_split, br)

    out = _moe(n2, meta, counts, h, exp_w_gate[l], exp_w_up[l], exp_w_down[l], row(final_norm_g), xs0)
    return out.reshape(b, s, d)
```

```python
import functools

import numpy as np
import jax
import jax.numpy as jnp
from jax import lax
from jax.experimental import pallas as pl
from jax.experimental.pallas import tpu as pltpu

F32 = jnp.float32
BF16 = jnp.bfloat16

D_MODEL = 1024
D_RWKV = 512
D_HGRN = 512
RWKV_HEAD = 64
RANK_W = 64
RANK_A = 64
RANK_G = 128
HGRN_HEAD = 128
HGRN_HEADS = D_HGRN // HGRN_HEAD
CONV_W = 4
N_GROUPS = 4
EXPERTS_PER_GROUP = 8
N_EXPERTS = N_GROUPS * EXPERTS_PER_GROUP
D_EXPERT = 512
RMS_EPS = 1e-6
GN_EPS = 64e-5
L2_EPS = 1e-12
D_RWKV_IN = 3 * D_RWKV + RANK_W + RANK_A + RANK_G
D_HGRN_IN = 4 * D_HGRN

V7X_LANES = 128
V7X_SUBLANES = 8
V7X_MXU_DIM = 256
V7X_VMEM_LIMIT_BYTES = 56 * 1024 * 1024

CHUNK = 64
GROUP_LANES = V7X_MXU_DIM
HEADS_PER_GROUP = GROUP_LANES // RWKV_HEAD
N_RWKV_GROUPS = D_RWKV // GROUP_LANES
N_LEVELS = 6
HGRN_BATCH = 4
RWKV_BATCH = 4
TM_PROJ = 512
TM_SLOTS = 2048
TM_DISPATCH = 512
TB_MOE = 512


def _dot(a, b):
    return jnp.dot(a, b, preferred_element_type=F32)


def _dot_nt(a, b):
    return lax.dot_general(a, b, (((1,), (1,)), ((), ())), preferred_element_type=F32)


def _dot_tn(a, b):
    return lax.dot_general(a, b, (((0,), (0,)), ((), ())), preferred_element_type=F32)


def _split_bf16(x):
    hi = x.astype(BF16)
    lo = (x - hi.astype(F32)).astype(BF16)
    return hi, lo


def _dot01_left(m01, x):
    hi, lo = _split_bf16(x)
    return _dot(m01, hi) + _dot(m01, lo)


def _log1p_exp_neg_abs(x):
    return jnp.log1p(jnp.exp(-jnp.abs(x)))


def _log_sigmoid(x):
    return jnp.minimum(x, 0.0) - _log1p_exp_neg_abs(x)


def _sigmoid(x):
    return 1.0 / (1.0 + jnp.exp(-x))


def _rms(x, gain):
    return x * lax.rsqrt(jnp.mean(x * x, axis=-1, keepdims=True) + RMS_EPS) * gain


def _const_spec(shape):
    nd = len(shape)
    return pl.BlockSpec(shape, lambda *_: (0,) * nd)


def _inproj_kernel(x_ref, g_ref, w_ref, pa_ref, pb_ref):
    n = _rms(x_ref[...], g_ref[...]).astype(BF16)
    pa_ref[...] = _dot(n, w_ref[:, :D_RWKV_IN])
    pb_ref[...] = _dot(n, w_ref[:, D_RWKV_IN:])


def _inproj(x2d, gain, w_bf16):
    t = x2d.shape[0]
    return pl.pallas_call(
        _inproj_kernel,
        grid=(t // TM_PROJ,),
        in_specs=[
            pl.BlockSpec((TM_PROJ, D_MODEL), lambda i: (i, 0)),
            _const_spec((1, D_MODEL)),
            _const_spec((D_MODEL, D_RWKV_IN + D_HGRN_IN)),
        ],
        out_specs=[
            pl.BlockSpec((TM_PROJ, D_RWKV_IN), lambda i: (i, 0)),
            pl.BlockSpec((TM_PROJ, D_HGRN_IN), lambda i: (i, 0)),
        ],
        out_shape=[
            jax.ShapeDtypeStruct((t, D_RWKV_IN), F32),
            jax.ShapeDtypeStruct((t, D_HGRN_IN), F32),
        ],
        compiler_params=pltpu.CompilerParams(
            dimension_semantics=("parallel",), vmem_limit_bytes=V7X_VMEM_LIMIT_BYTES),
        name="inproj",
    )(x2d, gain, w_bf16)


_M_EYE, _M_BD, _M_STRICT, _M_INCL, _M_OFF0 = 0, 1, 2, 3, 4


def _rwkv_masks():
    i = np.arange(GROUP_LANES)[:, None]
    j = np.arange(GROUP_LANES)[None, :]
    bd = (i // CHUNK) == (j // CHUNK)
    masks = [i == j, bd, (j % CHUNK) < (i % CHUNK), (j % CHUNK) <= (i % CHUNK)]
    for lvl in range(N_LEVELS):
        s = 1 << lvl
        masks.append(((i // (2 * s)) == (j // (2 * s))) & ((i % (2 * s)) >= s) & ((j % (2 * s)) < s))
    return np.stack(masks).astype(np.float32)


def _tile_rows(x, n):
    return jnp.concatenate([x] * n, axis=0)


def _rwkv_kernel(p_ref, mu_ref, w0_ref, a0_ref, wwa_ref, gup_ref, kk_ref, ka_ref, rk_ref, lnw_ref,
                 lnb_ref, tri_ref, m_ref, mb_ref, o_ref, z_ref, prev_ref, st_ref):
    c = pl.program_id(1)

    @pl.when(c == 0)
    def _():
        prev_ref[...] = jnp.zeros_like(prev_ref)
        st_ref[...] = jnp.zeros_like(st_ref)

    rows = RWKV_BATCH * CHUNK
    p = p_ref[...].reshape(rows, D_RWKV_IN)
    row = lax.broadcasted_iota(jnp.int32, p.shape, 0)
    shifted = pltpu.roll(p, 1, 0)
    for bi in range(RWKV_BATCH):
        shifted = jnp.where(row == bi * CHUNK, prev_ref[bi, V7X_SUBLANES - 1:V7X_SUBLANES, :], shifted)
        prev_ref[bi] = p[(bi + 1) * CHUNK - V7X_SUBLANES:(bi + 1) * CHUNK, :]
    p = p + mu_ref[...] * (shifted - p)

    r = p[:, 0:D_RWKV]
    k = p[:, D_RWKV:2 * D_RWKV]
    v = p[:, 2 * D_RWKV:3 * D_RWKV]
    x_wa = p[:, 3 * D_RWKV:3 * D_RWKV + RANK_W + RANK_A]
    dg = p[:, 3 * D_RWKV + RANK_W + RANK_A:]

    lane = lax.broadcasted_iota(jnp.int32, x_wa.shape, 1)
    t_wa = jnp.where(lane < RANK_W, jnp.tanh(x_wa), x_wa).astype(BF16)
    wa = _dot(t_wa, wwa_ref[...])
    w_lin = w0_ref[...] + wa[:, :D_RWKV]
    w = _log_sigmoid(w_lin) - 0.5
    log_decay = -jnp.exp(w)
    a = _sigmoid(a0_ref[...] + wa[:, D_RWKV:])
    g = _dot(_sigmoid(dg).astype(BF16), gup_ref[...])

    bd_bf = mb_ref[_M_BD]

    def head_sum(x):
        n = N_RWKV_GROUPS
        parts = [half[:, gi * GROUP_LANES:(gi + 1) * GROUP_LANES] for half in _split_bf16(x) for gi in range(n)]
        s = _dot(jnp.concatenate(parts, axis=0), bd_bf)
        return jnp.concatenate(
            [s[gi * rows:(gi + 1) * rows] + s[(n + gi) * rows:(n + gi + 1) * rows] for gi in range(n)], axis=1)

    kk = k * kk_ref[...]
    kk = kk / jnp.maximum(jnp.sqrt(head_sum(kk * kk)), L2_EPS)
    k = k * (1.0 + (a - 1.0) * ka_ref[...])
    kb = kk * a

    sums = _dot01_left(tri_ref[...], log_decay)
    cum = sums[:rows]
    e_in = jnp.exp(cum)
    e_ex = jnp.exp(cum - log_decay)
    e_inv = jnp.exp(-cum)
    e_rem = jnp.exp(sums[rows:])

    r_hat = (r * e_in).astype(BF16)
    kk_hat = (kk * e_ex).astype(BF16)
    k_inv = (k * e_inv).astype(BF16)
    b_inv = (kb * e_inv).astype(BF16)
    k_rem = (k * e_rem).astype(BF16)
    b_rem = (kb * e_rem).astype(BF16)
    v_bf = v.astype(BF16)

    eye = m_ref[_M_EYE]
    bd = m_ref[_M_BD]
    strict_lc = m_ref[_M_STRICT][:CHUNK]
    incl_lc = m_ref[_M_INCL][:CHUNK]

    def block_diag(x_bf):
        return _tile_rows(x_bf, HEADS_PER_GROUP) * bd_bf

    chains = [(bi, gi) for bi in range(RWKV_BATCH) for gi in range(N_RWKV_GROUPS)]
    rs_of = lambda bi: slice(bi * CHUNK, (bi + 1) * CHUNK)
    sl_of = lambda gi: slice(gi * GROUP_LANES, (gi + 1) * GROUP_LANES)

    lhs, a_kk, a_rk, a_rb, a_kb_bd, inv = {}, {}, {}, {}, {}, {}
    for ch in chains:
        rs, sl = rs_of(ch[0]), sl_of(ch[1])
        lhs[ch] = jnp.concatenate([kk_hat[rs, sl], r_hat[rs, sl]], axis=0)
        a_k = _dot_nt(lhs[ch], block_diag(k_inv[rs, sl]))
        a_b = _dot_nt(lhs[ch], block_diag(b_inv[rs, sl]))
        a_kk[ch] = (a_k[:CHUNK] * strict_lc).astype(BF16)
        a_rk[ch] = (a_k[CHUNK:] * incl_lc).astype(BF16)
        a_rb[ch] = (a_b[CHUNK:] * incl_lc).astype(BF16)
        a_kb_bd[ch] = block_diag((a_b[:CHUNK] * strict_lc).astype(BF16))
        inv[ch] = eye - (a_kb_bd[ch] * mb_ref[_M_OFF0]).astype(F32)

    for lvl in range(1, N_LEVELS):
        inv_bf = {ch: inv[ch].astype(BF16) for ch in chains}
        mid = {ch: _dot(inv_bf[ch], a_kb_bd[ch] * mb_ref[_M_OFF0 + lvl]).astype(BF16) for ch in chains}
        for ch in chains:
            inv[ch] = inv[ch] - _dot(mid[ch], inv_bf[ch])

    ys = {}
    for ch in chains:
        bi, gi = ch
        rs, sl = rs_of(bi), sl_of(gi)
        inv_lc = inv[ch][0:CHUNK]
        for h in range(1, HEADS_PER_GROUP):
            inv_lc = inv_lc + inv[ch][h * CHUNK:(h + 1) * CHUNK]
        s_bd = st_ref[bi, gi]
        v_bd = block_diag(v_bf[rs, sl])
        from_state = _dot_nt(lhs[ch], s_bd.astype(BF16))
        rhs = from_state[:CHUNK] + _dot(a_kk[ch], v_bd)
        u_bf = _dot(inv_lc.astype(BF16), block_diag(rhs.astype(BF16))).astype(BF16)
        ys[ch] = from_state[CHUNK:] + _dot(a_rk[ch], v_bd) - _dot(a_rb[ch], block_diag(u_bf))

        vu = jnp.concatenate([v_bf[rs, sl], -u_bf], axis=0)
        kb_rem = jnp.concatenate([k_rem[rs, sl], b_rem[rs, sl]], axis=0)
        w_chunk = e_in[(bi + 1) * CHUNK - 1:(bi + 1) * CHUNK, sl]
        st_ref[bi, gi] = s_bd * w_chunk + _dot_tn(vu, kb_rem) * bd

    y = jnp.concatenate(
        [jnp.concatenate([ys[(bi, gi)] for gi in range(N_RWKV_GROUPS)], axis=1) for bi in range(RWKV_BATCH)],
        axis=0)
    inv_n = 1.0 / RWKV_HEAD
    mean = head_sum(y) * inv_n
    yc = y - mean
    var = head_sum(yc * yc) * inv_n
    yn = yc * lax.rsqrt(var + GN_EPS) * lnw_ref[...] + lnb_ref[...]
    bonus = head_sum(r * k * rk_ref[...])
    o_ref[...] = ((yn + bonus * v) * g).astype(BF16).reshape(RWKV_BATCH, CHUNK, D_RWKV)
    z_ref[...] = jnp.zeros_like(z_ref)


def _rwkv(pa, mu, w0, a0, wwa, gup, k_k, k_a, r_k, ln_w, ln_b, zero_rows):
    b, s, _ = pa.shape
    n_steps = (b // RWKV_BATCH) * (s // CHUNK)
    z_block = -(-zero_rows // (n_steps * V7X_SUBLANES)) * V7X_SUBLANES
    tril = np.tril(np.ones((CHUNK, CHUNK), np.float32))
    eye_b = np.eye(RWKV_BATCH, dtype=np.float32)
    tri = jnp.asarray(np.concatenate([np.kron(eye_b, tril), np.kron(eye_b, 1.0 - tril)], axis=0), BF16)
    masks = jnp.asarray(_rwkv_masks())
    vec = _const_spec((1, D_RWKV))
    return pl.pallas_call(
        _rwkv_kernel,
        grid=(b // RWKV_BATCH, s // CHUNK),
        in_specs=[
            pl.BlockSpec((RWKV_BATCH, CHUNK, D_RWKV_IN), lambda i, j: (i, j, 0)),
            _const_spec((1, D_RWKV_IN)),
            vec, vec,
            _const_spec((RANK_W + RANK_A, 2 * D_RWKV)),
            _const_spec((RANK_G, D_RWKV)),
            vec, vec, vec, vec, vec,
            _const_spec(tri.shape),
            _const_spec(masks.shape),
            _const_spec(masks.shape),
        ],
        out_specs=[
            pl.BlockSpec((RWKV_BATCH, CHUNK, D_RWKV), lambda i, j: (i, j, 0)),
            pl.BlockSpec((z_block, V7X_LANES), lambda i, j: (i * (s // CHUNK) + j, 0)),
        ],
        out_shape=[
            jax.ShapeDtypeStruct((b, s, D_RWKV), BF16),
            jax.ShapeDtypeStruct((z_block * n_steps, V7X_LANES), F32),
        ],
        scratch_shapes=[
            pltpu.VMEM((RWKV_BATCH, V7X_SUBLANES, D_RWKV_IN), F32),
            pltpu.VMEM((RWKV_BATCH, N_RWKV_GROUPS, GROUP_LANES, GROUP_LANES), F32),
        ],
        compiler_params=pltpu.CompilerParams(
            dimension_semantics=("parallel", "arbitrary"), vmem_limit_bytes=V7X_VMEM_LIMIT_BYTES),
        name="rwkv7",
    )(pa, mu, w0, a0, wwa, gup, k_k, k_a, r_k, ln_w, ln_b, tri, masks, masks.astype(BF16))


def _hgrn_sum_matrix():
    t = np.arange(CHUNK)[:, None]
    j = np.arange(CHUNK)[None, :]
    mats = [(j <= t), (j > t)]
    for lvl in range(N_LEVELS):
        s = 1 << lvl
        mid = (t // (2 * s)) * (2 * s) + s - 1
        mats.append(((j > mid) & (j <= t)).astype(np.float32) - ((j > t) & (j <= mid)).astype(np.float32))
    return np.concatenate([np.asarray(m, np.float32) for m in mats], axis=0)


def _hgrn_level_masks():
    t = np.arange(CHUNK)[:, None]
    s_ = np.arange(CHUNK)[None, :]
    masks = [t == s_]
    for lvl in range(N_LEVELS):
        s = 1 << lvl
        masks.append(((t // (2 * s)) == (s_ // (2 * s))) & ((t % (2 * s)) >= s) & ((s_ % (2 * s)) < s))
    return np.stack(masks).astype(np.float32)


def _hgrn_kernel(p_ref, cw_ref, lbl_ref, ng_ref, sm_ref, lm_ref, o_ref, prev_ref, st_ref):
    c = pl.program_id(1)

    @pl.when(c == 0)
    def _():
        prev_ref[...] = jnp.zeros_like(prev_ref)
        st_ref[...] = jnp.zeros_like(st_ref)

    l0 = lbl_ref[0:1, :]
    l1 = lbl_ref[1:2, :]
    lmax = jnp.maximum(l0, l1)
    e0 = jnp.exp(l0 - lmax)
    lb = e0 / (e0 + jnp.exp(l1 - lmax))
    ta = jnp.log(lb)

    q_lv, k_lv, q_inter, k_inter, w_chunk, val_bf = {}, {}, {}, {}, {}, {}
    for bi in range(HGRN_BATCH):
        x = p_ref[bi, :, :3 * D_HGRN]
        xe = jnp.concatenate([prev_ref[bi], x], axis=0)
        prev_ref[bi] = x[CHUNK - V7X_SUBLANES:, :]
        conv = x * cw_ref[CONV_W - 1:CONV_W, :]
        for s in range(1, CONV_W):
            conv = conv + pltpu.roll(xe, s, 0)[V7X_SUBLANES:, :] * cw_ref[CONV_W - 1 - s:CONV_W - s, :]

        qc = conv[:, :D_HGRN]
        f = conv[:, D_HGRN:2 * D_HGRN]
        val_bf[bi] = conv[:, 2 * D_HGRN:].astype(BF16)
        q = qc * _sigmoid(qc)
        tb = jnp.log1p(-lb) + _log_sigmoid(f)
        log_f = jnp.maximum(ta, tb) + _log1p_exp_neg_abs(ta - tb)
        kg = (1.0 - lb) * _sigmoid(-f)

        sums = _dot01_left(sm_ref[...], log_f)
        b_in = sums[0:CHUNK]
        q_inter[bi] = (q * jnp.exp(b_in)).astype(BF16)
        k_inter[bi] = (kg * jnp.exp(sums[CHUNK:2 * CHUNK])).astype(BF16)
        w_chunk[bi] = jnp.exp(b_in[CHUNK - 1:CHUNK, :])
        q_lv[bi] = [q.astype(BF16)]
        k_lv[bi] = [kg.astype(BF16)]
        for lvl in range(N_LEVELS):
            e = jnp.exp(-jnp.abs(sums[(2 + lvl) * CHUNK:(3 + lvl) * CHUNK])).astype(BF16)
            q_lv[bi].append(q_lv[bi][0] * e)
            k_lv[bi].append(k_lv[bi][0] * e)

    chains = [(bi, h) for bi in range(HGRN_BATCH) for h in range(HGRN_HEADS)]
    sl_of = lambda h: slice(h * HGRN_HEAD, (h + 1) * HGRN_HEAD)
    scores = {}
    for lvl in range(N_LEVELS + 1):
        for ch in chains:
            bi, sl = ch[0], sl_of(ch[1])
            term = _dot_nt(q_lv[bi][lvl][:, sl], k_lv[bi][lvl][:, sl]) * lm_ref[lvl]
            scores[ch] = term if lvl == 0 else scores[ch] + term
    outs = {}
    for ch in chains:
        bi, h = ch
        sl = sl_of(h)
        st = st_ref[bi, h]
        o = _dot(scores[ch].astype(BF16), val_bf[bi][:, sl]) + _dot_nt(q_inter[bi][:, sl], st.astype(BF16))
        st_ref[bi, h] = st * w_chunk[bi][:, sl] + _dot_tn(val_bf[bi][:, sl], k_inter[bi][:, sl])
        outs[ch] = o * lax.rsqrt(jnp.mean(o * o, axis=-1, keepdims=True) + RMS_EPS)
    for bi in range(HGRN_BATCH):
        gate = p_ref[bi, :, 3 * D_HGRN:]
        o = jnp.concatenate([outs[(bi, h)] for h in range(HGRN_HEADS)], axis=1)
        o_ref[bi] = (o * ng_ref[...] * (gate * _sigmoid(gate))).astype(BF16)


def _hgrn(pb, conv_w, lb_logits, norm_g):
    b, s, _ = pb.shape
    sum_m = jnp.asarray(_hgrn_sum_matrix(), BF16)
    lvl_m = jnp.asarray(_hgrn_level_masks())
    return pl.pallas_call(
        _hgrn_kernel,
        grid=(b // HGRN_BATCH, s // CHUNK),
        in_specs=[
            pl.BlockSpec((HGRN_BATCH, CHUNK, D_HGRN_IN), lambda i, j: (i, j, 0)),
            _const_spec((CONV_W, 3 * D_HGRN)),
            _const_spec(lb_logits.shape),
            _const_spec((1, D_HGRN)),
            _const_spec(sum_m.shape),
            _const_spec(lvl_m.shape),
        ],
        out_specs=pl.BlockSpec((HGRN_BATCH, CHUNK, D_HGRN), lambda i, j: (i, j, 0)),
        out_shape=jax.ShapeDtypeStruct((b, s, D_HGRN), BF16),
        scratch_shapes=[
            pltpu.VMEM((HGRN_BATCH, V7X_SUBLANES, 3 * D_HGRN), F32),
            pltpu.VMEM((HGRN_BATCH, HGRN_HEADS, HGRN_HEAD, HGRN_HEAD), F32),
        ],
        compiler_params=pltpu.CompilerParams(
            dimension_semantics=("parallel", "arbitrary"), vmem_limit_bytes=V7X_VMEM_LIMIT_BYTES),
        name="hgrn2",
    )(pb, conv_w, lb_logits, norm_g, sum_m, lvl_m)


ROUTER_LANES = V7X_LANES


TOKEN_ROWS = V7X_SUBLANES
assert D_MODEL == TOKEN_ROWS * V7X_LANES


def _tiles_shape(tokens):
    return (tokens * TOKEN_ROWS, V7X_LANES)


def _store_token_tiles(ref, x):
    rows = x.shape[0]
    for s in range(TOKEN_ROWS):
        ref[pl.ds(s, rows, stride=TOKEN_ROWS), :] = x[:, s * V7X_LANES:(s + 1) * V7X_LANES]


def _load_token_tiles(ref):
    rows = ref.shape[0] // TOKEN_ROWS
    return jnp.concatenate([ref[pl.ds(s, rows, stride=TOKEN_ROWS), :] for s in range(TOKEN_ROWS)], axis=1)


_R_E1, _R_E2, _R_G1, _R_G2, _R_RANK1, _R_RANK2 = 0, 1, 2, 3, 4, 5


def _outproj_kernel(x_ref, ya_ref, yb_ref, wo_ref, g2_ref, wr_ref, br_ref, tri_ref,
                    h_ref, n_ref, meta_ref, counts_ref, carry_ref):
    @pl.when(pl.program_id(0) == 0)
    def _():
        carry_ref[...] = jnp.zeros_like(carry_ref)

    h = (x_ref[...]
         + _dot(ya_ref[...], wo_ref[:D_RWKV, :])
         + _dot(yb_ref[...], wo_ref[D_RWKV:, :]))
    h_ref[...] = h
    n = _rms(h, g2_ref[...])
    _store_token_tiles(n_ref, n)

    n_hi, n_lo = _split_bf16(n)
    logits = (_dot(n_hi, wr_ref[0]) + _dot(n_lo, wr_ref[0]) + _dot(n_hi, wr_ref[1])) + br_ref[...]
    lane = lax.broadcasted_iota(jnp.int32, logits.shape, 1).astype(F32)
    neg = -jnp.inf
    gl = jnp.where((lane >= N_EXPERTS) & (lane < N_EXPERTS + N_GROUPS), logits, neg)
    gmax = jnp.max(gl, axis=-1, keepdims=True)
    gidx = jnp.min(jnp.where(gl == gmax, lane - N_EXPERTS, ROUTER_LANES), axis=-1, keepdims=True)
    g_w = 1.0 / jnp.sum(jnp.exp(gl - gmax), axis=-1, keepdims=True)
    lo = gidx * EXPERTS_PER_GROUP
    el = jnp.where((lane >= lo) & (lane < lo + EXPERTS_PER_GROUP), logits, neg)
    m1 = jnp.max(el, axis=-1, keepdims=True)
    i1 = jnp.min(jnp.where(el == m1, lane, ROUTER_LANES), axis=-1, keepdims=True)
    el2 = jnp.where(lane == i1, neg, el)
    m2 = jnp.max(el2, axis=-1, keepdims=True)
    i2 = jnp.min(jnp.where(el2 == m2, lane, ROUTER_LANES), axis=-1, keepdims=True)
    t = jnp.exp(m2 - m1)
    w1 = 1.0 / (1.0 + t)

    onehot1 = lane == i1
    onehot2 = lane == i2
    assigned = jnp.where(onehot1 | onehot2, 1.0, 0.0)
    before = _dot(tri_ref[...], assigned.astype(BF16)) + carry_ref[...]
    rank1 = jnp.sum(jnp.where(onehot1, before, 0.0), axis=-1, keepdims=True)
    rank2 = jnp.sum(jnp.where(onehot2, before, 0.0), axis=-1, keepdims=True)
    carry_ref[...] += jnp.sum(assigned, axis=0, keepdims=True)
    counts_ref[...] = jnp.broadcast_to(carry_ref[...], counts_ref.shape)

    meta = jnp.zeros_like(logits)
    for idx, val in ((_R_E1, i1), (_R_E2, i2), (_R_G1, g_w * w1), (_R_G2, g_w * (t * w1)),
                     (_R_RANK1, rank1), (_R_RANK2, rank2)):
        meta = jnp.where(lane == idx, val, meta)
    meta_ref[...] = meta


def _outproj(x2d, ya, yb, wo_bf16, g2, wr, br):
    t = x2d.shape[0]
    row = lambda w: pl.BlockSpec((TM_PROJ, w), lambda i: (i, 0))
    tri = jnp.asarray(np.tril(np.ones((TM_PROJ, TM_PROJ), np.float32), -1), BF16)
    return pl.pallas_call(
        _outproj_kernel,
        grid=(t // TM_PROJ,),
        in_specs=[
            row(D_MODEL), row(D_RWKV), row(D_HGRN),
            _const_spec((D_RWKV + D_HGRN, D_MODEL)),
            _const_spec((1, D_MODEL)),
            _const_spec((2, D_MODEL, ROUTER_LANES)),
            _const_spec((1, ROUTER_LANES)),
            _const_spec((TM_PROJ, TM_PROJ)),
        ],
        out_specs=[row(D_MODEL), pl.BlockSpec(_tiles_shape(TM_PROJ), lambda i: (i, 0)), row(ROUTER_LANES),
                   _const_spec((V7X_SUBLANES, ROUTER_LANES))],
        out_shape=[
            jax.ShapeDtypeStruct((t, D_MODEL), F32),
            jax.ShapeDtypeStruct(_tiles_shape(t), F32),
            jax.ShapeDtypeStruct((t, ROUTER_LANES), F32),
            jax.ShapeDtypeStruct((V7X_SUBLANES, ROUTER_LANES), F32),
        ],
        scratch_shapes=[pltpu.VMEM((1, ROUTER_LANES), F32)],
        compiler_params=pltpu.CompilerParams(
            dimension_semantics=("arbitrary",), vmem_limit_bytes=V7X_VMEM_LIMIT_BYTES),
        name="outproj_router",
    )(x2d, ya, yb, wo_bf16, g2, wr, br, tri)


def _slot_kernel(meta_ref, offs_ref, pos_ref):
    meta = meta_ref[...]
    lane = lax.broadcasted_iota(jnp.int32, meta.shape, 1).astype(F32)
    offs = offs_ref[...]

    def slot(e_lane, rank_lane):
        e = meta[:, e_lane:e_lane + 1]
        return jnp.sum(jnp.where(lane == e, offs, 0.0), axis=-1, keepdims=True) + meta[:, rank_lane:rank_lane + 1]

    pos = jnp.concatenate([slot(_R_E1, _R_RANK1), slot(_R_E2, _R_RANK2)], axis=1)
    pos_ref[...] = pos.astype(jnp.int32)


def _slots(meta, offs):
    t = meta.shape[0]
    return pl.pallas_call(
        _slot_kernel,
        grid=(t // TM_SLOTS,),
        in_specs=[pl.BlockSpec((TM_SLOTS, ROUTER_LANES), lambda i: (i, 0)), _const_spec((1, ROUTER_LANES))],
        out_specs=pl.BlockSpec((TM_SLOTS, 2), lambda i: (i, 0)),
        out_shape=jax.ShapeDtypeStruct((t, 2), jnp.int32),
        compiler_params=pltpu.CompilerParams(dimension_semantics=("parallel",)),
        name="moe_slots",
    )(meta, offs)


def _token_rows(token, count=1):
    return pl.ds(pl.multiple_of(token * TOKEN_ROWS, TOKEN_ROWS), count * TOKEN_ROWS)


def _row_copy(src_ref, src_token, dst_ref, dst_token, sem):
    return pltpu.make_async_copy(src_ref.at[_token_rows(src_token)], dst_ref.at[_token_rows(dst_token)], sem)


def _dispatch_kernel(pos_ref, n_ref, xs_in_ref, xs_ref, sem):
    del xs_in_ref

    def body(r, carry):
        _row_copy(n_ref, r, xs_ref, pos_ref[2 * r], sem).start(priority=0)
        _row_copy(n_ref, r, xs_ref, pos_ref[2 * r + 1], sem).start(priority=1)
        return carry

    lax.fori_loop(0, TM_DISPATCH, body, 0, unroll=8)
    for _ in range(2):
        pltpu.make_async_copy(n_ref, xs_ref.at[_token_rows(0, TM_DISPATCH)], sem).wait()


def _dispatch(pos_flat, n2, xs0):
    t = n2.shape[0] // TOKEN_ROWS
    return pl.pallas_call(
        _dispatch_kernel,
        grid=(t // TM_DISPATCH,),
        in_specs=[
            pl.BlockSpec((2 * TM_DISPATCH,), lambda i: (i,), memory_space=pltpu.SMEM),
            pl.BlockSpec(_tiles_shape(TM_DISPATCH), lambda i: (i, 0)),
            pl.BlockSpec(memory_space=pl.ANY),
        ],
        out_specs=pl.BlockSpec(memory_space=pl.ANY),
        out_shape=jax.ShapeDtypeStruct(xs0.shape, F32),
        scratch_shapes=[pltpu.SemaphoreType.DMA(())],
        input_output_aliases={2: 0},
        compiler_params=pltpu.CompilerParams(
            dimension_semantics=("arbitrary",), has_side_effects=True),
        name="moe_dispatch",
    )(pos_flat, n2, xs0)


def _gmm_kernel(be_ref, nu_ref, xs_ref, wg_ref, wu_ref, wd_ref, y_ref, wgu_s, wd_s):
    b = pl.program_id(0)

    @pl.when(b < nu_ref[0])
    def _():
        @pl.when((b == 0) | (be_ref[b] != be_ref[jnp.maximum(b - 1, 0)]))
        def _():
            wgu_s[:, :D_EXPERT] = wg_ref[...].astype(BF16)
            wgu_s[:, D_EXPERT:] = wu_ref[...].astype(BF16)
            wd_s[...] = wd_ref[...].astype(BF16)

        gu = _dot(_load_token_tiles(xs_ref).astype(BF16), wgu_s[...])
        hg = gu[:, :D_EXPERT]
        hid = (hg * _sigmoid(hg) * gu[:, D_EXPERT:]).astype(BF16)
        _store_token_tiles(y_ref, _dot(hid, wd_s[...]))

    @pl.when(b >= nu_ref[0])
    def _():
        y_ref[...] = jnp.zeros_like(y_ref)


def _gmm(block_expert, n_used, xs, wg, wu, wd):
    n_slots = block_expert.shape[0] * TB_MOE
    last = lambda b, nu: jnp.maximum(jnp.minimum(b, nu[0] - 1), 0)
    blk = lambda b, be, nu: (last(b, nu), 0)
    wsel = lambda b, be, nu: (be[last(b, nu)], 0, 0)
    return pl.pallas_call(
        _gmm_kernel,
        grid_spec=pltpu.PrefetchScalarGridSpec(
            num_scalar_prefetch=2,
            grid=(n_slots // TB_MOE,),
            in_specs=[
                pl.BlockSpec(_tiles_shape(TB_MOE), blk),
                pl.BlockSpec((None, D_MODEL, D_EXPERT), wsel),
                pl.BlockSpec((None, D_MODEL, D_EXPERT), wsel),
                pl.BlockSpec((None, D_EXPERT, D_MODEL), wsel),
            ],
            out_specs=pl.BlockSpec(_tiles_shape(TB_MOE), lambda b, be, nu: (b, 0)),
            scratch_shapes=[
                pltpu.VMEM((D_MODEL, 2 * D_EXPERT), BF16),
                pltpu.VMEM((D_EXPERT, D_MODEL), BF16),
            ],
        ),
        out_shape=jax.ShapeDtypeStruct(_tiles_shape(n_slots), F32),
        compiler_params=pltpu.CompilerParams(
            dimension_semantics=("arbitrary",), vmem_limit_bytes=V7X_VMEM_LIMIT_BYTES),
        name="moe_gmm",
    )(block_expert, n_used, xs, wg, wu, wd)


def _combine_kernel(pos_ref, pos_next_ref, meta_ref, h_ref, gf_ref, ys_ref, o_ref, buf_ref, sem):
    i = pl.program_id(0)
    slot = lax.rem(i, 2)

    def issue(p_ref, s):
        def body(r, carry):
            _row_copy(ys_ref, p_ref[2 * r], buf_ref.at[s, 0], r, sem.at[s]).start(priority=0)
            _row_copy(ys_ref, p_ref[2 * r + 1], buf_ref.at[s, 1], r, sem.at[s]).start(priority=1)
            return carry
        lax.fori_loop(0, TM_DISPATCH, body, 0, unroll=8)

    @pl.when(i == 0)
    def _():
        issue(pos_ref, 0)

    @pl.when(i + 1 < pl.num_programs(0))
    def _():
        issue(pos_next_ref, 1 - slot)

    for j in range(2):
        pltpu.make_async_copy(ys_ref.at[_token_rows(0, TM_DISPATCH)], buf_ref.at[slot, j], sem.at[slot]).wait()

    meta = meta_ref[...]
    g1 = meta[:, _R_G1:_R_G1 + 1]
    g2 = meta[:, _R_G2:_R_G2 + 1]
    moe = g1 * _load_token_tiles(buf_ref.at[slot, 0]) + g2 * _load_token_tiles(buf_ref.at[slot, 1])
    o_ref[...] = _rms(h_ref[...] + moe, gf_ref[...])


def _combine(pos_flat, meta, h, gf, ys):
    t = h.shape[0]
    n_tiles = t // TM_DISPATCH
    return pl.pallas_call(
        _combine_kernel,
        grid=(n_tiles,),
        in_specs=[
            pl.BlockSpec((2 * TM_DISPATCH,), lambda i: (i,), memory_space=pltpu.SMEM),
            pl.BlockSpec((2 * TM_DISPATCH,), lambda i: (jnp.minimum(i + 1, n_tiles - 1),),
                         memory_space=pltpu.SMEM),
            pl.BlockSpec((TM_DISPATCH, ROUTER_LANES), lambda i: (i, 0)),
            pl.BlockSpec((TM_DISPATCH, D_MODEL), lambda i: (i, 0)),
            _const_spec((1, D_MODEL)),
            pl.BlockSpec(memory_space=pl.ANY),
        ],
        out_specs=pl.BlockSpec((TM_DISPATCH, D_MODEL), lambda i: (i, 0)),
        out_shape=jax.ShapeDtypeStruct((t, D_MODEL), F32),
        scratch_shapes=[
            pltpu.VMEM((2, 2) + _tiles_shape(TM_DISPATCH), F32),
            pltpu.SemaphoreType.DMA((2,)),
        ],
        compiler_params=pltpu.CompilerParams(
            dimension_semantics=("arbitrary",), vmem_limit_bytes=V7X_VMEM_LIMIT_BYTES),
        name="moe_combine",
    )(pos_flat, pos_flat, meta, h, gf, ys)


def _moe_blocks(t):
    return (2 * t + N_EXPERTS * (TB_MOE - 1) + TB_MOE - 1) // TB_MOE


def _moe(n2, meta, counts, h, wg, wu, wd, gf, xs0):
    n_blocks = _moe_blocks(h.shape[0])
    cnt = counts[0, :N_EXPERTS].astype(jnp.int32)
    blocks = (cnt + TB_MOE - 1) // TB_MOE
    ends = jnp.cumsum(blocks)
    offs = ((ends - blocks) * TB_MOE).astype(F32)
    offs_row = jnp.zeros((1, ROUTER_LANES), F32).at[0, :N_EXPERTS].set(offs)
    block_ids = jnp.arange(n_blocks, dtype=jnp.int32)
    block_expert = jnp.minimum(
        jnp.sum((ends[None, :] <= block_ids[:, None]).astype(jnp.int32), axis=1), N_EXPERTS - 1)
    n_used = ends[-1:].astype(jnp.int32)

    pos = _slots(meta, offs_row)
    pos_flat = pos.reshape(-1)
    xs = _dispatch(pos_flat, n2, xs0)
    ys = _gmm(block_expert, n_used, xs, wg, wu, wd)
    return _combine(pos_flat, meta, h, gf, ys)


def kernel(x, norm1_g, w_in, rwkv_mu, rwkv_w0, rwkv_w_up, rwkv_a0, rwkv_a_up, rwkv_g_up, rwkv_k_k,
           rwkv_k_a, rwkv_r_k, rwkv_ln_w, rwkv_ln_b, hgrn_conv_w, hgrn_lb_logits, hgrn_norm_g, w_out,
           norm2_g, router_g_w, router_g_b, router_e_w, router_e_b, exp_w_gate, exp_w_up, exp_w_down,
           final_norm_g):
    b, s, d = x.shape
    t = b * s
    l = 0
    x2d = x.reshape(t, d)
    row = lambda a: a.reshape(1, -1)

    pa, pb = _inproj(x2d, row(norm1_g[l]), w_in[l].astype(BF16))

    wwa = jnp.zeros((RANK_W + RANK_A, 2 * D_RWKV), F32)
    wwa = wwa.at[:RANK_W, :D_RWKV].set(rwkv_w_up[l]).at[RANK_W:, D_RWKV:].set(rwkv_a_up[l]).astype(BF16)
    ya, xs0 = _rwkv(pa.reshape(b, s, D_RWKV_IN), row(rwkv_mu[l]), row(rwkv_w0[l]), row(rwkv_a0[l]), wwa,
                    rwkv_g_up[l].astype(BF16), row(rwkv_k_k[l]), row(rwkv_k_a[l]), row(rwkv_r_k[l]),
                    row(rwkv_ln_w[l]), row(rwkv_ln_b[l]), _moe_blocks(t) * TB_MOE * TOKEN_ROWS)
    yb = _hgrn(pb.reshape(b, s, D_HGRN_IN), hgrn_conv_w[l], hgrn_lb_logits, row(hgrn_norm_g[l]))

    wr = jnp.zeros((D_MODEL, ROUTER_LANES), F32)
    wr = wr.at[:, :N_EXPERTS].set(router_e_w[l]).at[:, N_EXPERTS:N_EXPERTS + N_GROUPS].set(router_g_w[l])
    br = jnp.zeros((1, ROUTER_LANES), F32)
    br = br.at[0, :N_EXPERTS].set(router_e_b[l]).at[0, N_EXPERTS:N_EXPERTS + N_GROUPS].set(router_g_b[l])
    wr_hi = wr.astype(BF16)
    wr_split = jnp.stack([wr_hi, (wr - wr_hi.astype(F32)).astype(BF16)])
    h, n2, meta, counts = _outproj(x2d, ya.reshape(t, D_RWKV), yb.reshape(t, D_HGRN),
                                   w_out[l].astype(BF16), row(norm2_g[l]), wr_split, br)

    out = _moe(n2, meta, counts, h, exp_w_gate[l], exp_w_up[l], exp_w_down[l], row(final_norm_g), xs0)
    return out.reshape(b, s, d)
```

```python
import numpy as np
import jax
import jax.numpy as jnp
from jax import lax
from jax.experimental import pallas as pl
from jax.experimental.pallas import tpu as pltpu

F32 = jnp.float32
BF16 = jnp.bfloat16

D_MODEL = 1024
D_RWKV = 512
D_HGRN = 512
RWKV_HEAD = 64
RANK_W = 64
RANK_A = 64
RANK_G = 128
HGRN_HEAD = 128
HGRN_HEADS = D_HGRN // HGRN_HEAD
CONV_W = 4
N_GROUPS = 4
EXPERTS_PER_GROUP = 8
N_EXPERTS = N_GROUPS * EXPERTS_PER_GROUP
D_EXPERT = 512
RMS_EPS = 1e-6
GN_EPS = 64e-5
L2_EPS = 1e-12
D_RWKV_IN = 3 * D_RWKV + RANK_W + RANK_A + RANK_G
D_HGRN_IN = 4 * D_HGRN

V7X_LANES = 128
V7X_SUBLANES = 8
V7X_MXU_DIM = 256
V7X_VMEM_LIMIT_BYTES = 56 * 1024 * 1024

CHUNK = 64
GROUP_LANES = V7X_MXU_DIM
HEADS_PER_GROUP = GROUP_LANES // RWKV_HEAD
N_RWKV_GROUPS = D_RWKV // GROUP_LANES
N_LEVELS = 6
HGRN_BATCH = 4
RWKV_BATCH = 4
TM_PROJ = 512
TM_SLOTS = 2048
TM_DISPATCH = 512
TB_MOE = 512


def _dot(a, b):
    return jnp.dot(a, b, preferred_element_type=F32)


def _dot_nt(a, b):
    return lax.dot_general(a, b, (((1,), (1,)), ((), ())), preferred_element_type=F32)


def _dot_tn(a, b):
    return lax.dot_general(a, b, (((0,), (0,)), ((), ())), preferred_element_type=F32)


def _split_bf16(x):
    hi = x.astype(BF16)
    lo = (x - hi.astype(F32)).astype(BF16)
    return hi, lo


def _dot01_left(m01, x):
    hi, lo = _split_bf16(x)
    return _dot(m01, hi) + _dot(m01, lo)


def _log1p_exp_neg_abs(x):
    return jnp.log1p(jnp.exp(-jnp.abs(x)))


def _log_sigmoid(x):
    return jnp.minimum(x, 0.0) - _log1p_exp_neg_abs(x)


def _sigmoid(x):
    return 1.0 / (1.0 + jnp.exp(-x))


def _rms(x, gain):
    return x * lax.rsqrt(jnp.mean(x * x, axis=-1, keepdims=True) + RMS_EPS) * gain


def _const_spec(shape):
    nd = len(shape)
    return pl.BlockSpec(shape, lambda *_: (0,) * nd)


def _inproj_kernel(x_ref, g_ref, w_ref, pa_ref, pb_ref):
    n = _rms(x_ref[...], g_ref[...]).astype(BF16)
    pa_ref[...] = _dot(n, w_ref[:, :D_RWKV_IN])
    pb_ref[...] = _dot(n, w_ref[:, D_RWKV_IN:])


def _inproj(x2d, gain, w_bf16):
    t = x2d.shape[0]
    return pl.pallas_call(
        _inproj_kernel,
        grid=(t // TM_PROJ,),
        in_specs=[
            pl.BlockSpec((TM_PROJ, D_MODEL), lambda i: (i, 0)),
            _const_spec((1, D_MODEL)),
            _const_spec((D_MODEL, D_RWKV_IN + D_HGRN_IN)),
        ],
        out_specs=[
            pl.BlockSpec((TM_PROJ, D_RWKV_IN), lambda i: (i, 0)),
            pl.BlockSpec((TM_PROJ, D_HGRN_IN), lambda i: (i, 0)),
        ],
        out_shape=[
            jax.ShapeDtypeStruct((t, D_RWKV_IN), F32),
            jax.ShapeDtypeStruct((t, D_HGRN_IN), F32),
        ],
        compiler_params=pltpu.CompilerParams(
            dimension_semantics=("parallel",), vmem_limit_bytes=V7X_VMEM_LIMIT_BYTES),
        name="inproj",
    )(x2d, gain, w_bf16)


_M_EYE, _M_BD, _M_STRICT, _M_INCL, _M_OFF0 = 0, 1, 2, 3, 4


def _rwkv_masks():
    i = np.arange(GROUP_LANES)[:, None]
    j = np.arange(GROUP_LANES)[None, :]
    bd = (i // CHUNK) == (j // CHUNK)
    masks = [i == j, bd, (j % CHUNK) < (i % CHUNK), (j % CHUNK) <= (i % CHUNK)]
    for lvl in range(N_LEVELS):
        s = 1 << lvl
        masks.append(((i // (2 * s)) == (j // (2 * s))) & ((i % (2 * s)) >= s) & ((j % (2 * s)) < s))
    return np.stack(masks).astype(np.float32)


def _tile_rows(x, n):
    return jnp.concatenate([x] * n, axis=0)


def _rwkv_kernel(p_ref, mu_ref, w0_ref, a0_ref, wwa_ref, gup_ref, kk_ref, ka_ref, rk_ref, lnw_ref,
                 lnb_ref, tri_ref, m_ref, mb_ref, o_ref, z_ref, prev_ref, st_ref):
    c = pl.program_id(1)

    @pl.when(c == 0)
    def _():
        prev_ref[...] = jnp.zeros_like(prev_ref)
        st_ref[...] = jnp.zeros_like(st_ref)

    rows = RWKV_BATCH * CHUNK
    p = p_ref[...].reshape(rows, D_RWKV_IN)
    row = lax.broadcasted_iota(jnp.int32, p.shape, 0)
    shifted = pltpu.roll(p, 1, 0)
    for bi in range(RWKV_BATCH):
        shifted = jnp.where(row == bi * CHUNK, prev_ref[bi, V7X_SUBLANES - 1:V7X_SUBLANES, :], shifted)
        prev_ref[bi] = p[(bi + 1) * CHUNK - V7X_SUBLANES:(bi + 1) * CHUNK, :]
    p = p + mu_ref[...] * (shifted - p)

    r = p[:, 0:D_RWKV]
    k = p[:, D_RWKV:2 * D_RWKV]
    v = p[:, 2 * D_RWKV:3 * D_RWKV]
    x_wa = p[:, 3 * D_RWKV:3 * D_RWKV + RANK_W + RANK_A]
    dg = p[:, 3 * D_RWKV + RANK_W + RANK_A:]

    lane = lax.broadcasted_iota(jnp.int32, x_wa.shape, 1)
    t_wa = jnp.where(lane < RANK_W, jnp.tanh(x_wa), x_wa).astype(BF16)
    wa = _dot(t_wa, wwa_ref[...])
    w_lin = w0_ref[...] + wa[:, :D_RWKV]
    w = _log_sigmoid(w_lin) - 0.5
    log_decay = -jnp.exp(w)
    a = _sigmoid(a0_ref[...] + wa[:, D_RWKV:])
    g = _dot(_sigmoid(dg).astype(BF16), gup_ref[...])

    bd_bf = mb_ref[_M_BD]

    def head_sum(x):
        n = N_RWKV_GROUPS
        parts = [half[:, gi * GROUP_LANES:(gi + 1) * GROUP_LANES] for half in _split_bf16(x) for gi in range(n)]
        s = _dot(jnp.concatenate(parts, axis=0), bd_bf)
        return jnp.concatenate(
            [s[gi * rows:(gi + 1) * rows] + s[(n + gi) * rows:(n + gi + 1) * rows] for gi in range(n)], axis=1)

    kk = k * kk_ref[...]
    kk = kk / jnp.maximum(jnp.sqrt(head_sum(kk * kk)), L2_EPS)
    k = k * (1.0 + (a - 1.0) * ka_ref[...])
    kb = kk * a

    sums = _dot01_left(tri_ref[...], log_decay)
    cum = sums[:rows]
    e_in = jnp.exp(cum)
    e_ex = jnp.exp(cum - log_decay)
    e_inv = jnp.exp(-cum)
    e_rem = jnp.exp(sums[rows:])

    r_hat = (r * e_in).astype(BF16)
    kk_hat = (kk * e_ex).astype(BF16)
    k_inv = (k * e_inv).astype(BF16)
    b_inv = (kb * e_inv).astype(BF16)
    k_rem = (k * e_rem).astype(BF16)
    b_rem = (kb * e_rem).astype(BF16)
    v_bf = v.astype(BF16)

    eye = m_ref[_M_EYE]
    bd = m_ref[_M_BD]
    strict_lc = m_ref[_M_STRICT][:CHUNK]
    incl_lc = m_ref[_M_INCL][:CHUNK]

    def block_diag(x_bf):
        return _tile_rows(x_bf, HEADS_PER_GROUP) * bd_bf

    chains = [(bi, gi) for bi in range(RWKV_BATCH) for gi in range(N_RWKV_GROUPS)]
    rs_of = lambda bi: slice(bi * CHUNK, (bi + 1) * CHUNK)
    sl_of = lambda gi: slice(gi * GROUP_LANES, (gi + 1) * GROUP_LANES)

    lhs, a_kk, a_rk, a_rb, a_kb_bd, inv = {}, {}, {}, {}, {}, {}
    for ch in chains:
        rs, sl = rs_of(ch[0]), sl_of(ch[1])
        lhs[ch] = jnp.concatenate([kk_hat[rs, sl], r_hat[rs, sl]], axis=0)
        a_k = _dot_nt(lhs[ch], block_diag(k_inv[rs, sl]))
        a_b = _dot_nt(lhs[ch], block_diag(b_inv[rs, sl]))
        a_kk[ch] = (a_k[:CHUNK] * strict_lc).astype(BF16)
        a_rk[ch] = (a_k[CHUNK:] * incl_lc).astype(BF16)
        a_rb[ch] = (a_b[CHUNK:] * incl_lc).astype(BF16)
        a_kb_bd[ch] = block_diag((a_b[:CHUNK] * strict_lc).astype(BF16))
        inv[ch] = eye - (a_kb_bd[ch] * mb_ref[_M_OFF0]).astype(F32)

    for lvl in range(1, N_LEVELS):
        inv_bf = {ch: inv[ch].astype(BF16) for ch in chains}
        mid = {ch: _dot(inv_bf[ch], a_kb_bd[ch] * mb_ref[_M_OFF0 + lvl]).astype(BF16) for ch in chains}
        for ch in chains:
            inv[ch] = inv[ch] - _dot(mid[ch], inv_bf[ch])

    ys = {}
    for ch in chains:
        bi, gi = ch
        rs, sl = rs_of(bi), sl_of(gi)
        inv_lc = inv[ch][0:CHUNK]
        for h in range(1, HEADS_PER_GROUP):
            inv_lc = inv_lc + inv[ch][h * CHUNK:(h + 1) * CHUNK]
        s_bd = st_ref[bi, gi]
        v_bd = block_diag(v_bf[rs, sl])
        from_state = _dot_nt(lhs[ch], s_bd.astype(BF16))
        rhs = from_state[:CHUNK] + _dot(a_kk[ch], v_bd)
        u_bf = _dot(inv_lc.astype(BF16), block_diag(rhs.astype(BF16))).astype(BF16)
        ys[ch] = from_state[CHUNK:] + _dot(a_rk[ch], v_bd) - _dot(a_rb[ch], block_diag(u_bf))

        vu = jnp.concatenate([v_bf[rs, sl], -u_bf], axis=0)
        kb_rem = jnp.concatenate([k_rem[rs, sl], b_rem[rs, sl]], axis=0)
        w_chunk = e_in[(bi + 1) * CHUNK - 1:(bi + 1) * CHUNK, sl]
        st_ref[bi, gi] = s_bd * w_chunk + _dot_tn(vu, kb_rem) * bd

    y = jnp.concatenate(
        [jnp.concatenate([ys[(bi, gi)] for gi in range(N_RWKV_GROUPS)], axis=1) for bi in range(RWKV_BATCH)],
        axis=0)
    inv_n = 1.0 / RWKV_HEAD
    mean = head_sum(y) * inv_n
    yc = y - mean
    var = head_sum(yc * yc) * inv_n
    yn = yc * lax.rsqrt(var + GN_EPS) * lnw_ref[...] + lnb_ref[...]
    bonus = head_sum(r * k * rk_ref[...])
    o_ref[...] = ((yn + bonus * v) * g).astype(BF16).reshape(RWKV_BATCH, CHUNK, D_RWKV)
    z_ref[...] = jnp.zeros_like(z_ref)


def _rwkv(pa, mu, w0, a0, wwa, gup, k_k, k_a, r_k, ln_w, ln_b, zero_rows):
    b, s, _ = pa.shape
    n_steps = (b // RWKV_BATCH) * (s // CHUNK)
    z_block = -(-zero_rows // (n_steps * V7X_SUBLANES)) * V7X_SUBLANES
    tril = np.tril(np.ones((CHUNK, CHUNK), np.float32))
    eye_b = np.eye(RWKV_BATCH, dtype=np.float32)
    tri = jnp.asarray(np.concatenate([np.kron(eye_b, tril), np.kron(eye_b, 1.0 - tril)], axis=0), BF16)
    masks = jnp.asarray(_rwkv_masks())
    vec = _const_spec((1, D_RWKV))
    return pl.pallas_call(
        _rwkv_kernel,
        grid=(b // RWKV_BATCH, s // CHUNK),
        in_specs=[
            pl.BlockSpec((RWKV_BATCH, CHUNK, D_RWKV_IN), lambda i, j: (i, j, 0)),
            _const_spec((1, D_RWKV_IN)),
            vec, vec,
            _const_spec((RANK_W + RANK_A, 2 * D_RWKV)),
            _const_spec((RANK_G, D_RWKV)),
            vec, vec, vec, vec, vec,
            _const_spec(tri.shape),
            _const_spec(masks.shape),
            _const_spec(masks.shape),
        ],
        out_specs=[
            pl.BlockSpec((RWKV_BATCH, CHUNK, D_RWKV), lambda i, j: (i, j, 0)),
            pl.BlockSpec((z_block, V7X_LANES), lambda i, j: (i * (s // CHUNK) + j, 0)),
        ],
        out_shape=[
            jax.ShapeDtypeStruct((b, s, D_RWKV), BF16),
            jax.ShapeDtypeStruct((z_block * n_steps, V7X_LANES), F32),
        ],
        scratch_shapes=[
            pltpu.VMEM((RWKV_BATCH, V7X_SUBLANES, D_RWKV_IN), F32),
            pltpu.VMEM((RWKV_BATCH, N_RWKV_GROUPS, GROUP_LANES, GROUP_LANES), F32),
        ],
        compiler_params=pltpu.CompilerParams(
            dimension_semantics=("parallel", "arbitrary"), vmem_limit_bytes=V7X_VMEM_LIMIT_BYTES),
        name="rwkv7",
    )(pa, mu, w0, a0, wwa, gup, k_k, k_a, r_k, ln_w, ln_b, tri, masks, masks.astype(BF16))


def _hgrn_sum_matrix():
    t = np.arange(CHUNK)[:, None]
    j = np.arange(CHUNK)[None, :]
    mats = [(j <= t), (j > t)]
    for lvl in range(N_LEVELS):
        s = 1 << lvl
        mid = (t // (2 * s)) * (2 * s) + s - 1
        mats.append(((j > mid) & (j <= t)).astype(np.float32) - ((j > t) & (j <= mid)).astype(np.float32))
    return np.concatenate([np.asarray(m, np.float32) for m in mats], axis=0)


def _hgrn_level_masks():
    t = np.arange(CHUNK)[:, None]
    s_ = np.arange(CHUNK)[None, :]
    masks = [t == s_]
    for lvl in range(N_LEVELS):
        s = 1 << lvl
        masks.append(((t // (2 * s)) == (s_ // (2 * s))) & ((t % (2 * s)) >= s) & ((s_ % (2 * s)) < s))
    return np.stack(masks).astype(np.float32)


def _hgrn_kernel(p_ref, cw_ref, lbl_ref, ng_ref, sm_ref, lm_ref, o_ref, prev_ref, st_ref):
    c = pl.program_id(1)

    @pl.when(c == 0)
    def _():
        prev_ref[...] = jnp.zeros_like(prev_ref)
        st_ref[...] = jnp.zeros_like(st_ref)

    l0 = lbl_ref[0:1, :]
    l1 = lbl_ref[1:2, :]
    lmax = jnp.maximum(l0, l1)
    e0 = jnp.exp(l0 - lmax)
    lb = e0 / (e0 + jnp.exp(l1 - lmax))
    ta = jnp.log(lb)

    q_lv, k_lv, q_inter, k_inter, w_chunk, val_bf = {}, {}, {}, {}, {}, {}
    for bi in range(HGRN_BATCH):
        x = p_ref[bi, :, :3 * D_HGRN]
        xe = jnp.concatenate([prev_ref[bi], x], axis=0)
        prev_ref[bi] = x[CHUNK - V7X_SUBLANES:, :]
        conv = x * cw_ref[CONV_W - 1:CONV_W, :]
        for s in range(1, CONV_W):
            conv = conv + pltpu.roll(xe, s, 0)[V7X_SUBLANES:, :] * cw_ref[CONV_W - 1 - s:CONV_W - s, :]

        qc = conv[:, :D_HGRN]
        f = conv[:, D_HGRN:2 * D_HGRN]
        val_bf[bi] = conv[:, 2 * D_HGRN:].astype(BF16)
        q = qc * _sigmoid(qc)
        tb = jnp.log1p(-lb) + _log_sigmoid(f)
        log_f = jnp.maximum(ta, tb) + _log1p_exp_neg_abs(ta - tb)
        kg = (1.0 - lb) * _sigmoid(-f)

        sums = _dot01_left(sm_ref[...], log_f)
        b_in = sums[0:CHUNK]
        q_inter[bi] = (q * jnp.exp(b_in)).astype(BF16)
        k_inter[bi] = (kg * jnp.exp(sums[CHUNK:2 * CHUNK])).astype(BF16)
        w_chunk[bi] = jnp.exp(b_in[CHUNK - 1:CHUNK, :])
        q_lv[bi] = [q.astype(BF16)]
        k_lv[bi] = [kg.astype(BF16)]
        for lvl in range(N_LEVELS):
            e = jnp.exp(-jnp.abs(sums[(2 + lvl) * CHUNK:(3 + lvl) * CHUNK])).astype(BF16)
            q_lv[bi].append(q_lv[bi][0] * e)
            k_lv[bi].append(k_lv[bi][0] * e)

    chains = [(bi, h) for bi in range(HGRN_BATCH) for h in range(HGRN_HEADS)]
    sl_of = lambda h: slice(h * HGRN_HEAD, (h + 1) * HGRN_HEAD)
    scores = {}
    for lvl in range(N_LEVELS + 1):
        for ch in chains:
            bi, sl = ch[0], sl_of(ch[1])
            term = _dot_nt(q_lv[bi][lvl][:, sl], k_lv[bi][lvl][:, sl]) * lm_ref[lvl]
            scores[ch] = term if lvl == 0 else scores[ch] + term
    outs = {}
    for ch in chains:
        bi, h = ch
        sl = sl_of(h)
        st = st_ref[bi, h]
        o = _dot(scores[ch].astype(BF16), val_bf[bi][:, sl]) + _dot_nt(q_inter[bi][:, sl], st.astype(BF16))
        st_ref[bi, h] = st * w_chunk[bi][:, sl] + _dot_tn(val_bf[bi][:, sl], k_inter[bi][:, sl])
        outs[ch] = o * lax.rsqrt(jnp.mean(o * o, axis=-1, keepdims=True) + RMS_EPS)
    for bi in range(HGRN_BATCH):
        gate = p_ref[bi, :, 3 * D_HGRN:]
        o = jnp.concatenate([outs[(bi, h)] for h in range(HGRN_HEADS)], axis=1)
        o_ref[bi] = (o * ng_ref[...] * (gate * _sigmoid(gate))).astype(BF16)


def _hgrn(pb, conv_w, lb_logits, norm_g):
    b, s, _ = pb.shape
    sum_m = jnp.asarray(_hgrn_sum_matrix(), BF16)
    lvl_m = jnp.asarray(_hgrn_level_masks())
    return pl.pallas_call(
        _hgrn_kernel,
        grid=(b // HGRN_BATCH, s // CHUNK),
        in_specs=[
            pl.BlockSpec((HGRN_BATCH, CHUNK, D_HGRN_IN), lambda i, j: (i, j, 0)),
            _const_spec((CONV_W, 3 * D_HGRN)),
            _const_spec(lb_logits.shape),
            _const_spec((1, D_HGRN)),
            _const_spec(sum_m.shape),
            _const_spec(lvl_m.shape),
        ],
        out_specs=pl.BlockSpec((HGRN_BATCH, CHUNK, D_HGRN), lambda i, j: (i, j, 0)),
        out_shape=jax.ShapeDtypeStruct((b, s, D_HGRN), BF16),
        scratch_shapes=[
            pltpu.VMEM((HGRN_BATCH, V7X_SUBLANES, 3 * D_HGRN), F32),
            pltpu.VMEM((HGRN_BATCH, HGRN_HEADS, HGRN_HEAD, HGRN_HEAD), F32),
        ],
        compiler_params=pltpu.CompilerParams(
            dimension_semantics=("parallel", "arbitrary"), vmem_limit_bytes=V7X_VMEM_LIMIT_BYTES),
        name="hgrn2",
    )(pb, conv_w, lb_logits, norm_g, sum_m, lvl_m)


ROUTER_LANES = V7X_LANES


TOKEN_ROWS = V7X_SUBLANES
assert D_MODEL == TOKEN_ROWS * V7X_LANES


def _tiles_shape(tokens):
    return (tokens * TOKEN_ROWS, V7X_LANES)


def _store_token_tiles(ref, x):
    rows = x.shape[0]
    for s in range(TOKEN_ROWS):
        ref[pl.ds(s, rows, stride=TOKEN_ROWS), :] = x[:, s * V7X_LANES:(s + 1) * V7X_LANES]


def _load_token_tiles(ref):
    rows = ref.shape[0] // TOKEN_ROWS
    return jnp.concatenate([ref[pl.ds(s, rows, stride=TOKEN_ROWS), :] for s in range(TOKEN_ROWS)], axis=1)


_R_E1, _R_E2, _R_G1, _R_G2, _R_RANK1, _R_RANK2 = 0, 1, 2, 3, 4, 5


def _outproj_kernel(x_ref, ya_ref, yb_ref, wo_ref, g2_ref, wr_ref, br_ref, tri_ref,
                    h_ref, n_ref, meta_ref, counts_ref, carry_ref):
    @pl.when(pl.program_id(0) == 0)
    def _():
        carry_ref[...] = jnp.zeros_like(carry_ref)

    h = (x_ref[...]
         + _dot(ya_ref[...], wo_ref[:D_RWKV, :])
         + _dot(yb_ref[...], wo_ref[D_RWKV:, :]))
    h_ref[...] = h
    n = _rms(h, g2_ref[...])
    _store_token_tiles(n_ref, n)

    n_hi, n_lo = _split_bf16(n)
    logits = (_dot(n_hi, wr_ref[0]) + _dot(n_lo, wr_ref[0]) + _dot(n_hi, wr_ref[1])) + br_ref[...]
    lane = lax.broadcasted_iota(jnp.int32, logits.shape, 1).astype(F32)
    neg = -jnp.inf
    gl = jnp.where((lane >= N_EXPERTS) & (lane < N_EXPERTS + N_GROUPS), logits, neg)
    gmax = jnp.max(gl, axis=-1, keepdims=True)
    gidx = jnp.min(jnp.where(gl == gmax, lane - N_EXPERTS, ROUTER_LANES), axis=-1, keepdims=True)
    g_w = 1.0 / jnp.sum(jnp.exp(gl - gmax), axis=-1, keepdims=True)
    lo = gidx * EXPERTS_PER_GROUP
    el = jnp.where((lane >= lo) & (lane < lo + EXPERTS_PER_GROUP), logits, neg)
    m1 = jnp.max(el, axis=-1, keepdims=True)
    i1 = jnp.min(jnp.where(el == m1, lane, ROUTER_LANES), axis=-1, keepdims=True)
    el2 = jnp.where(lane == i1, neg, el)
    m2 = jnp.max(el2, axis=-1, keepdims=True)
    i2 = jnp.min(jnp.where(el2 == m2, lane, ROUTER_LANES), axis=-1, keepdims=True)
    t = jnp.exp(m2 - m1)
    w1 = 1.0 / (1.0 + t)

    onehot1 = lane == i1
    onehot2 = lane == i2
    assigned = jnp.where(onehot1 | onehot2, 1.0, 0.0)
    before = _dot(tri_ref[...], assigned.astype(BF16)) + carry_ref[...]
    rank1 = jnp.sum(jnp.where(onehot1, before, 0.0), axis=-1, keepdims=True)
    rank2 = jnp.sum(jnp.where(onehot2, before, 0.0), axis=-1, keepdims=True)
    carry_ref[...] += jnp.sum(assigned, axis=0, keepdims=True)
    counts_ref[...] = jnp.broadcast_to(carry_ref[...], counts_ref.shape)

    meta = jnp.zeros_like(logits)
    for idx, val in ((_R_E1, i1), (_R_E2, i2), (_R_G1, g_w * w1), (_R_G2, g_w * (t * w1)),
                     (_R_RANK1, rank1), (_R_RANK2, rank2)):
        meta = jnp.where(lane == idx, val, meta)
    meta_ref[...] = meta


def _outproj(x2d, ya, yb, wo_bf16, g2, wr, br):
    t = x2d.shape[0]
    row = lambda w: pl.BlockSpec((TM_PROJ, w), lambda i: (i, 0))
    tri = jnp.asarray(np.tril(np.ones((TM_PROJ, TM_PROJ), np.float32), -1), BF16)
    return pl.pallas_call(
        _outproj_kernel,
        grid=(t // TM_PROJ,),
        in_specs=[
            row(D_MODEL), row(D_RWKV), row(D_HGRN),
            _const_spec((D_RWKV + D_HGRN, D_MODEL)),
            _const_spec((1, D_MODEL)),
            _const_spec((2, D_MODEL, ROUTER_LANES)),
            _const_spec((1, ROUTER_LANES)),
            _const_spec((TM_PROJ, TM_PROJ)),
        ],
        out_specs=[row(D_MODEL), pl.BlockSpec(_tiles_shape(TM_PROJ), lambda i: (i, 0)), row(ROUTER_LANES),
                   _const_spec((V7X_SUBLANES, ROUTER_LANES))],
        out_shape=[
            jax.ShapeDtypeStruct((t, D_MODEL), F32),
            jax.ShapeDtypeStruct(_tiles_shape(t), F32),
            jax.ShapeDtypeStruct((t, ROUTER_LANES), F32),
            jax.ShapeDtypeStruct((V7X_SUBLANES, ROUTER_LANES), F32),
        ],
        scratch_shapes=[pltpu.VMEM((1, ROUTER_LANES), F32)],
        compiler_params=pltpu.CompilerParams(
            dimension_semantics=("arbitrary",), vmem_limit_bytes=V7X_VMEM_LIMIT_BYTES),
        name="outproj_router",
    )(x2d, ya, yb, wo_bf16, g2, wr, br, tri)


def _slot_kernel(meta_ref, offs_ref, pos_ref):
    meta = meta_ref[...]
    lane = lax.broadcasted_iota(jnp.int32, meta.shape, 1).astype(F32)
    offs = offs_ref[...]

    def slot(e_lane, rank_lane):
        e = meta[:, e_lane:e_lane + 1]
        return jnp.sum(jnp.where(lane == e, offs, 0.0), axis=-1, keepdims=True) + meta[:, rank_lane:rank_lane + 1]

    pos = jnp.concatenate([slot(_R_E1, _R_RANK1), slot(_R_E2, _R_RANK2)], axis=1)
    pos_ref[...] = pos.astype(jnp.int32)


def _slots(meta, offs):
    t = meta.shape[0]
    return pl.pallas_call(
        _slot_kernel,
        grid=(t // TM_SLOTS,),
        in_specs=[pl.BlockSpec((TM_SLOTS, ROUTER_LANES), lambda i: (i, 0)), _const_spec((1, ROUTER_LANES))],
        out_specs=pl.BlockSpec((TM_SLOTS, 2), lambda i: (i, 0)),
        out_shape=jax.ShapeDtypeStruct((t, 2), jnp.int32),
        compiler_params=pltpu.CompilerParams(dimension_semantics=("parallel",)),
        name="moe_slots",
    )(meta, offs)


def _token_rows(token, count=1):
    return pl.ds(pl.multiple_of(token * TOKEN_ROWS, TOKEN_ROWS), count * TOKEN_ROWS)


def _row_copy(src_ref, src_token, dst_ref, dst_token, sem):
    return pltpu.make_async_copy(src_ref.at[_token_rows(src_token)], dst_ref.at[_token_rows(dst_token)], sem)


def _dispatch_kernel(pos_ref, n_ref, xs_in_ref, xs_ref, sem):
    del xs_in_ref

    def body(r, carry):
        _row_copy(n_ref, r, xs_ref, pos_ref[2 * r], sem).start(priority=0)
        _row_copy(n_ref, r, xs_ref, pos_ref[2 * r + 1], sem).start(priority=1)
        return carry

    lax.fori_loop(0, TM_DISPATCH, body, 0, unroll=8)
    for _ in range(2):
        pltpu.make_async_copy(n_ref, xs_ref.at[_token_rows(0, TM_DISPATCH)], sem).wait()


def _dispatch(pos_flat, n2, xs0):
    t = n2.shape[0] // TOKEN_ROWS
    return pl.pallas_call(
        _dispatch_kernel,
        grid=(t // TM_DISPATCH,),
        in_specs=[
            pl.BlockSpec((2 * TM_DISPATCH,), lambda i: (i,), memory_space=pltpu.SMEM),
            pl.BlockSpec(_tiles_shape(TM_DISPATCH), lambda i: (i, 0)),
            pl.BlockSpec(memory_space=pl.ANY),
        ],
        out_specs=pl.BlockSpec(memory_space=pl.ANY),
        out_shape=jax.ShapeDtypeStruct(xs0.shape, F32),
        scratch_shapes=[pltpu.SemaphoreType.DMA(())],
        input_output_aliases={2: 0},
        compiler_params=pltpu.CompilerParams(
            dimension_semantics=("arbitrary",), has_side_effects=True),
        name="moe_dispatch",
    )(pos_flat, n2, xs0)


def _gmm_kernel(be_ref, nu_ref, xs_ref, wg_ref, wu_ref, wd_ref, y_ref, wgu_s, wd_s):
    b = pl.program_id(0)

    @pl.when(b < nu_ref[0])
    def _():
        @pl.when((b == 0) | (be_ref[b] != be_ref[jnp.maximum(b - 1, 0)]))
        def _():
            wgu_s[:, :D_EXPERT] = wg_ref[...].astype(BF16)
            wgu_s[:, D_EXPERT:] = wu_ref[...].astype(BF16)
            wd_s[...] = wd_ref[...].astype(BF16)

        gu = _dot(_load_token_tiles(xs_ref).astype(BF16), wgu_s[...])
        hg = gu[:, :D_EXPERT]
        hid = (hg * _sigmoid(hg) * gu[:, D_EXPERT:]).astype(BF16)
        _store_token_tiles(y_ref, _dot(hid, wd_s[...]))

    @pl.when(b >= nu_ref[0])
    def _():
        y_ref[...] = jnp.zeros_like(y_ref)


def _gmm(block_expert, n_used, xs, wg, wu, wd):
    n_slots = block_expert.shape[0] * TB_MOE
    last = lambda b, nu: jnp.maximum(jnp.minimum(b, nu[0] - 1), 0)
    blk = lambda b, be, nu: (last(b, nu), 0)
    wsel = lambda b, be, nu: (be[last(b, nu)], 0, 0)
    return pl.pallas_call(
        _gmm_kernel,
        grid_spec=pltpu.PrefetchScalarGridSpec(
            num_scalar_prefetch=2,
            grid=(n_slots // TB_MOE,),
            in_specs=[
                pl.BlockSpec(_tiles_shape(TB_MOE), blk),
                pl.BlockSpec((None, D_MODEL, D_EXPERT), wsel),
                pl.BlockSpec((None, D_MODEL, D_EXPERT), wsel),
                pl.BlockSpec((None, D_EXPERT, D_MODEL), wsel),
            ],
            out_specs=pl.BlockSpec(_tiles_shape(TB_MOE), lambda b, be, nu: (b, 0)),
            scratch_shapes=[
                pltpu.VMEM((D_MODEL, 2 * D_EXPERT), BF16),
                pltpu.VMEM((D_EXPERT, D_MODEL), BF16),
            ],
        ),
        out_shape=jax.ShapeDtypeStruct(_tiles_shape(n_slots), F32),
        compiler_params=pltpu.CompilerParams(
            dimension_semantics=("arbitrary",), vmem_limit_bytes=V7X_VMEM_LIMIT_BYTES),
        name="moe_gmm",
    )(block_expert, n_used, xs, wg, wu, wd)


def _combine_kernel(pos_ref, pos_next_ref, meta_ref, h_ref, gf_ref, ys_ref, o_ref, buf_ref, sem):
    i = pl.program_id(0)
    slot = lax.rem(i, 2)

    def issue(p_ref, s):
        def body(r, carry):
            _row_copy(ys_ref, p_ref[2 * r], buf_ref.at[s, 0], r, sem.at[s]).start(priority=0)
            _row_copy(ys_ref, p_ref[2 * r + 1], buf_ref.at[s, 1], r, sem.at[s]).start(priority=1)
            return carry
        lax.fori_loop(0, TM_DISPATCH, body, 0, unroll=8)

    @pl.when(i == 0)
    def _():
        issue(pos_ref, 0)

    @pl.when(i + 1 < pl.num_programs(0))
    def _():
        issue(pos_next_ref, 1 - slot)

    for j in range(2):
        pltpu.make_async_copy(ys_ref.at[_token_rows(0, TM_DISPATCH)], buf_ref.at[slot, j], sem.at[slot]).wait()

    meta = meta_ref[...]
    g1 = meta[:, _R_G1:_R_G1 + 1]
    g2 = meta[:, _R_G2:_R_G2 + 1]
    moe = g1 * _load_token_tiles(buf_ref.at[slot, 0]) + g2 * _load_token_tiles(buf_ref.at[slot, 1])
    o_ref[...] = _rms(h_ref[...] + moe, gf_ref[...])


def _combine(pos_flat, meta, h, gf, ys):
    t = h.shape[0]
    n_tiles = t // TM_DISPATCH
    return pl.pallas_call(
        _combine_kernel,
        grid=(n_tiles,),
        in_specs=[
            pl.BlockSpec((2 * TM_DISPATCH,), lambda i: (i,), memory_space=pltpu.SMEM),
            pl.BlockSpec((2 * TM_DISPATCH,), lambda i: (jnp.minimum(i + 1, n_tiles - 1),),
                         memory_space=pltpu.SMEM),
            pl.BlockSpec((TM_DISPATCH, ROUTER_LANES), lambda i: (i, 0)),
            pl.BlockSpec((TM_DISPATCH, D_MODEL), lambda i: (i, 0)),
            _const_spec((1, D_MODEL)),
            pl.BlockSpec(memory_space=pl.ANY),
        ],
        out_specs=pl.BlockSpec((TM_DISPATCH, D_MODEL), lambda i: (i, 0)),
        out_shape=jax.ShapeDtypeStruct((t, D_MODEL), F32),
        scratch_shapes=[
            pltpu.VMEM((2, 2) + _tiles_shape(TM_DISPATCH), F32),
            pltpu.SemaphoreType.DMA((2,)),
        ],
        compiler_params=pltpu.CompilerParams(
            dimension_semantics=("arbitrary",), vmem_limit_bytes=V7X_VMEM_LIMIT_BYTES),
        name="moe_combine",
    )(pos_flat, pos_flat, meta, h, gf, ys)


def _moe_blocks(t):
    return (2 * t + N_EXPERTS * (TB_MOE - 1) + TB_MOE - 1) // TB_MOE


def _moe(n2, meta, counts, h, wg, wu, wd, gf, xs0):
    n_blocks = _moe_blocks(h.shape[0])
    cnt = counts[0, :N_EXPERTS].astype(jnp.int32)
    blocks = (cnt + TB_MOE - 1) // TB_MOE
    ends = jnp.cumsum(blocks)
    offs = ((ends - blocks) * TB_MOE).astype(F32)
    offs_row = jnp.zeros((1, ROUTER_LANES), F32).at[0, :N_EXPERTS].set(offs)
    block_ids = jnp.arange(n_blocks, dtype=jnp.int32)
    block_expert = jnp.minimum(
        jnp.sum((ends[None, :] <= block_ids[:, None]).astype(jnp.int32), axis=1), N_EXPERTS - 1)
    n_used = ends[-1:].astype(jnp.int32)

    pos = _slots(meta, offs_row)
    pos_flat = pos.reshape(-1)
    xs = _dispatch(pos_flat, n2, xs0)
    ys = _gmm(block_expert, n_used, xs, wg, wu, wd)
    return _combine(pos_flat, meta, h, gf, ys)


def kernel(x, norm1_g, w_in, rwkv_mu, rwkv_w0, rwkv_w_up, rwkv_a0, rwkv_a_up, rwkv_g_up, rwkv_k_k,
           rwkv_k_a, rwkv_r_k, rwkv_ln_w, rwkv_ln_b, hgrn_conv_w, hgrn_lb_logits, hgrn_norm_g, w_out,
           norm2_g, router_g_w, router_g_b, router_e_w, router_e_b, exp_w_gate, exp_w_up, exp_w_down,
           final_norm_g):
    b, s, d = x.shape
    t = b * s
    l = 0
    x2d = x.reshape(t, d)
    row = lambda a: a.reshape(1, -1)

    pa, pb = _inproj(x2d, row(norm1_g[l]), w_in[l].astype(BF16))

    wwa = jnp.zeros((RANK_W + RANK_A, 2 * D_RWKV), F32)
    wwa = wwa.at[:RANK_W, :D_RWKV].set(rwkv_w_up[l]).at[RANK_W:, D_RWKV:].set(rwkv_a_up[l]).astype(BF16)
    ya, xs0 = _rwkv(pa.reshape(b, s, D_RWKV_IN), row(rwkv_mu[l]), row(rwkv_w0[l]), row(rwkv_a0[l]), wwa,
                    rwkv_g_up[l].astype(BF16), row(rwkv_k_k[l]), row(rwkv_k_a[l]), row(rwkv_r_k[l]),
                    row(rwkv_ln_w[l]), row(rwkv_ln_b[l]), _moe_blocks(t) * TB_MOE * TOKEN_ROWS)
    yb = _hgrn(pb.reshape(b, s, D_HGRN_IN), hgrn_conv_w[l], hgrn_lb_logits, row(hgrn_norm_g[l]))

    wr = jnp.zeros((D_MODEL, ROUTER_LANES), F32)
    wr = wr.at[:, :N_EXPERTS].set(router_e_w[l]).at[:, N_EXPERTS:N_EXPERTS + N_GROUPS].set(router_g_w[l])
    br = jnp.zeros((1, ROUTER_LANES), F32)
    br = br.at[0, :N_EXPERTS].set(router_e_b[l]).at[0, N_EXPERTS:N_EXPERTS + N_GROUPS].set(router_g_b[l])
    wr_hi = wr.astype(BF16)
    wr_split = jnp.stack([wr_hi, (wr - wr_hi.astype(F32)).astype(BF16)])
    h, n2, meta, counts = _outproj(x2d, ya.reshape(t, D_RWKV), yb.reshape(t, D_HGRN),
                                   w_out[l].astype(BF16), row(norm2_g[l]), wr_split, br)

    out = _moe(n2, meta, counts, h, exp_w_gate[l], exp_w_up[l], exp_w_down[l], row(final_norm_g), xs0)
    return out.reshape(b, s, d)
```

```python
import numpy as np
import jax
import jax.numpy as jnp
from jax import lax
from jax.experimental import pallas as pl
from jax.experimental.pallas import tpu as pltpu

F32 = jnp.float32
BF16 = jnp.bfloat16

D_MODEL = 1024
D_RWKV = 512
D_HGRN = 512
RWKV_HEAD = 64
RANK_W = 64
RANK_A = 64
RANK_G = 128
HGRN_HEAD = 128
HGRN_HEADS = D_HGRN // HGRN_HEAD
CONV_W = 4
N_GROUPS = 4
EXPERTS_PER_GROUP = 8
N_EXPERTS = N_GROUPS * EXPERTS_PER_GROUP
D_EXPERT = 512
RMS_EPS = 1e-6
GN_EPS = 64e-5
L2_EPS = 1e-12
D_RWKV_IN = 3 * D_RWKV + RANK_W + RANK_A + RANK_G
D_HGRN_IN = 4 * D_HGRN

V7X_LANES = 128
V7X_SUBLANES = 8
V7X_MXU_DIM = 256
V7X_VMEM_LIMIT_BYTES = 56 * 1024 * 1024

CHUNK = 64
GROUP_LANES = V7X_MXU_DIM
HEADS_PER_GROUP = GROUP_LANES // RWKV_HEAD
N_RWKV_GROUPS = D_RWKV // GROUP_LANES
N_LEVELS = 6
HGRN_BATCH = 4
RWKV_BATCH = 4
TM_PROJ = 512
TM_SLOTS = 2048
TM_DISPATCH = 512
TB_MOE = 512


def _dot(a, b):
    return jnp.dot(a, b, preferred_element_type=F32)


def _dot_nt(a, b):
    return lax.dot_general(a, b, (((1,), (1,)), ((), ())), preferred_element_type=F32)


def _dot_tn(a, b):
    return lax.dot_general(a, b, (((0,), (0,)), ((), ())), preferred_element_type=F32)


def _split_bf16(x):
    hi = x.astype(BF16)
    lo = (x - hi.astype(F32)).astype(BF16)
    return hi, lo


def _dot01_left(m01, x):
    hi, lo = _split_bf16(x)
    return _dot(m01, hi) + _dot(m01, lo)


def _log_sigmoid(x):
    return jnp.minimum(x, 0.0) - jnp.log(1.0 + jnp.exp(-jnp.abs(x)))


def _sigmoid(x):
    return 1.0 / (1.0 + jnp.exp(-x))


def _rms(x, gain):
    return x * lax.rsqrt(jnp.mean(x * x, axis=-1, keepdims=True) + RMS_EPS) * gain


def _const_spec(shape):
    nd = len(shape)
    return pl.BlockSpec(shape, lambda *_: (0,) * nd)


def _inproj_kernel(x_ref, g_ref, w_ref, pa_ref, pb_ref):
    n = _rms(x_ref[...], g_ref[...]).astype(BF16)
    pa_ref[...] = _dot(n, w_ref[:, :D_RWKV_IN])
    pb_ref[...] = _dot(n, w_ref[:, D_RWKV_IN:])


def _inproj(x2d, gain, w_bf16):
    t = x2d.shape[0]
    return pl.pallas_call(
        _inproj_kernel,
        grid=(t // TM_PROJ,),
        in_specs=[
            pl.BlockSpec((TM_PROJ, D_MODEL), lambda i: (i, 0)),
            _const_spec((1, D_MODEL)),
            _const_spec((D_MODEL, D_RWKV_IN + D_HGRN_IN)),
        ],
        out_specs=[
            pl.BlockSpec((TM_PROJ, D_RWKV_IN), lambda i: (i, 0)),
            pl.BlockSpec((TM_PROJ, D_HGRN_IN), lambda i: (i, 0)),
        ],
        out_shape=[
            jax.ShapeDtypeStruct((t, D_RWKV_IN), F32),
            jax.ShapeDtypeStruct((t, D_HGRN_IN), F32),
        ],
        compiler_params=pltpu.CompilerParams(
            dimension_semantics=("parallel",), vmem_limit_bytes=V7X_VMEM_LIMIT_BYTES),
        name="inproj",
    )(x2d, gain, w_bf16)


_M_EYE, _M_BD, _M_STRICT, _M_INCL, _M_OFF0 = 0, 1, 2, 3, 4


def _rwkv_masks():
    i = np.arange(GROUP_LANES)[:, None]
    j = np.arange(GROUP_LANES)[None, :]
    bd = (i // CHUNK) == (j // CHUNK)
    masks = [i == j, bd, (j % CHUNK) < (i % CHUNK), (j % CHUNK) <= (i % CHUNK)]
    for lvl in range(N_LEVELS):
        s = 1 << lvl
        masks.append(((i // (2 * s)) == (j // (2 * s))) & ((i % (2 * s)) >= s) & ((j % (2 * s)) < s))
    return np.stack(masks).astype(np.float32)


def _tile_rows(x, n):
    return jnp.concatenate([x] * n, axis=0)


def _rwkv_kernel(p_ref, mu_ref, w0_ref, a0_ref, wwa_ref, gup_ref, kk_ref, ka_ref, rk_ref, lnw_ref,
                 lnb_ref, tri_ref, m_ref, mb_ref, o_ref, z_ref, prev_ref, st_ref):
    c = pl.program_id(1)

    @pl.when(c == 0)
    def _():
        prev_ref[...] = jnp.zeros_like(prev_ref)
        st_ref[...] = jnp.zeros_like(st_ref)

    rows = RWKV_BATCH * CHUNK
    p = p_ref[...].reshape(rows, D_RWKV_IN)
    row = lax.broadcasted_iota(jnp.int32, p.shape, 0)
    shifted = pltpu.roll(p, 1, 0)
    for bi in range(RWKV_BATCH):
        shifted = jnp.where(row == bi * CHUNK, prev_ref[bi, V7X_SUBLANES - 1:V7X_SUBLANES, :], shifted)
        prev_ref[bi] = p[(bi + 1) * CHUNK - V7X_SUBLANES:(bi + 1) * CHUNK, :]
    p = p + mu_ref[...] * (shifted - p)

    r = p[:, 0:D_RWKV]
    k = p[:, D_RWKV:2 * D_RWKV]
    v = p[:, 2 * D_RWKV:3 * D_RWKV]
    x_wa = p[:, 3 * D_RWKV:3 * D_RWKV + RANK_W + RANK_A]
    dg = p[:, 3 * D_RWKV + RANK_W + RANK_A:]

    lane = lax.broadcasted_iota(jnp.int32, x_wa.shape, 1)
    t_wa = jnp.where(lane < RANK_W, jnp.tanh(x_wa), x_wa).astype(BF16)
    wa = _dot(t_wa, wwa_ref[...])
    w_lin = w0_ref[...] + wa[:, :D_RWKV]
    w = _log_sigmoid(w_lin) - 0.5
    log_decay = -jnp.exp(w)
    a = _sigmoid(a0_ref[...] + wa[:, D_RWKV:])
    g = _dot(_sigmoid(dg).astype(BF16), gup_ref[...])

    bd_bf = mb_ref[_M_BD]

    def head_sum(x):
        n = N_RWKV_GROUPS
        parts = [half[:, gi * GROUP_LANES:(gi + 1) * GROUP_LANES] for half in _split_bf16(x) for gi in range(n)]
        s = _dot(jnp.concatenate(parts, axis=0), bd_bf)
        return jnp.concatenate(
            [s[gi * rows:(gi + 1) * rows] + s[(n + gi) * rows:(n + gi + 1) * rows] for gi in range(n)], axis=1)

    kk = k * kk_ref[...]
    kk = kk / jnp.maximum(jnp.sqrt(head_sum(kk * kk)), L2_EPS)
    k = k * (1.0 + (a - 1.0) * ka_ref[...])
    kb = kk * a

    sums = _dot01_left(tri_ref[...], log_decay)
    cum = sums[:rows]
    e_in = jnp.exp(cum)
    e_ex = jnp.exp(cum - log_decay)
    e_inv = jnp.exp(-cum)
    e_rem = jnp.exp(sums[rows:])

    r_hat = (r * e_in).astype(BF16)
    kk_hat = (kk * e_ex).astype(BF16)
    k_inv = (k * e_inv).astype(BF16)
    b_inv = (kb * e_inv).astype(BF16)
    k_rem = (k * e_rem).astype(BF16)
    b_rem = (kb * e_rem).astype(BF16)
    v_bf = v.astype(BF16)

    eye = m_ref[_M_EYE]
    bd = m_ref[_M_BD]
    strict_lc = m_ref[_M_STRICT][:CHUNK]
    incl_lc = m_ref[_M_INCL][:CHUNK]

    def block_diag(x_bf):
        return _tile_rows(x_bf, HEADS_PER_GROUP) * bd_bf

    chains = [(bi, gi) for bi in range(RWKV_BATCH) for gi in range(N_RWKV_GROUPS)]
    rs_of = lambda bi: slice(bi * CHUNK, (bi + 1) * CHUNK)
    sl_of = lambda gi: slice(gi * GROUP_LANES, (gi + 1) * GROUP_LANES)

    lhs, a_kk, a_rk, a_rb, a_kb_bd, inv = {}, {}, {}, {}, {}, {}
    for ch in chains:
        rs, sl = rs_of(ch[0]), sl_of(ch[1])
        lhs[ch] = jnp.concatenate([kk_hat[rs, sl], r_hat[rs, sl]], axis=0)
        a_k = _dot_nt(lhs[ch], block_diag(k_inv[rs, sl]))
        a_b = _dot_nt(lhs[ch], block_diag(b_inv[rs, sl]))
        a_kk[ch] = (a_k[:CHUNK] * strict_lc).astype(BF16)
        a_rk[ch] = (a_k[CHUNK:] * incl_lc).astype(BF16)
        a_rb[ch] = (a_b[CHUNK:] * incl_lc).astype(BF16)
        a_kb_bd[ch] = block_diag((a_b[:CHUNK] * strict_lc).astype(BF16))
        inv[ch] = eye - (a_kb_bd[ch] * mb_ref[_M_OFF0]).astype(F32)

    for lvl in range(1, N_LEVELS):
        inv_bf = {ch: inv[ch].astype(BF16) for ch in chains}
        mid = {ch: _dot(inv_bf[ch], a_kb_bd[ch] * mb_ref[_M_OFF0 + lvl]).astype(BF16) for ch in chains}
        for ch in chains:
            inv[ch] = inv[ch] - _dot(mid[ch], inv_bf[ch])

    ys = {}
    for ch in chains:
        bi, gi = ch
        rs, sl = rs_of(bi), sl_of(gi)
        inv_lc = inv[ch][0:CHUNK]
        for h in range(1, HEADS_PER_GROUP):
            inv_lc = inv_lc + inv[ch][h * CHUNK:(h + 1) * CHUNK]
        s_bd = st_ref[bi, gi]
        v_bd = block_diag(v_bf[rs, sl])
        from_state = _dot_nt(lhs[ch], s_bd.astype(BF16))
        rhs = from_state[:CHUNK] + _dot(a_kk[ch], v_bd)
        u_bf = _dot(inv_lc.astype(BF16), block_diag(rhs.astype(BF16))).astype(BF16)
        ys[ch] = from_state[CHUNK:] + _dot(a_rk[ch], v_bd) - _dot(a_rb[ch], block_diag(u_bf))

        vu = jnp.concatenate([v_bf[rs, sl], -u_bf], axis=0)
        kb_rem = jnp.concatenate([k_rem[rs, sl], b_rem[rs, sl]], axis=0)
        w_chunk = e_in[(bi + 1) * CHUNK - 1:(bi + 1) * CHUNK, sl]
        st_ref[bi, gi] = s_bd * w_chunk + _dot_tn(vu, kb_rem) * bd

    y = jnp.concatenate(
        [jnp.concatenate([ys[(bi, gi)] for gi in range(N_RWKV_GROUPS)], axis=1) for bi in range(RWKV_BATCH)],
        axis=0)
    inv_n = 1.0 / RWKV_HEAD
    mean = head_sum(y) * inv_n
    yc = y - mean
    var = head_sum(yc * yc) * inv_n
    yn = yc * lax.rsqrt(var + GN_EPS) * lnw_ref[...] + lnb_ref[...]
    bonus = head_sum(r * k * rk_ref[...])
    o_ref[...] = ((yn + bonus * v) * g).astype(BF16).reshape(RWKV_BATCH, CHUNK, D_RWKV)
    z_ref[...] = jnp.zeros_like(z_ref)


def _rwkv(pa, mu, w0, a0, wwa, gup, k_k, k_a, r_k, ln_w, ln_b, zero_rows):
    b, s, _ = pa.shape
    n_steps = (b // RWKV_BATCH) * (s // CHUNK)
    z_block = -(-zero_rows // (n_steps * V7X_SUBLANES)) * V7X_SUBLANES
    tril = np.tril(np.ones((CHUNK, CHUNK), np.float32))
    eye_b = np.eye(RWKV_BATCH, dtype=np.float32)
    tri = jnp.asarray(np.concatenate([np.kron(eye_b, tril), np.kron(eye_b, 1.0 - tril)], axis=0), BF16)
    masks = jnp.asarray(_rwkv_masks())
    vec = _const_spec((1, D_RWKV))
    return pl.pallas_call(
        _rwkv_kernel,
        grid=(b // RWKV_BATCH, s // CHUNK),
        in_specs=[
            pl.BlockSpec((RWKV_BATCH, CHUNK, D_RWKV_IN), lambda i, j: (i, j, 0)),
            _const_spec((1, D_RWKV_IN)),
            vec, vec,
            _const_spec((RANK_W + RANK_A, 2 * D_RWKV)),
            _const_spec((RANK_G, D_RWKV)),
            vec, vec, vec, vec, vec,
            _const_spec(tri.shape),
            _const_spec(masks.shape),
            _const_spec(masks.shape),
        ],
        out_specs=[
            pl.BlockSpec((RWKV_BATCH, CHUNK, D_RWKV), lambda i, j: (i, j, 0)),
            pl.BlockSpec((z_block, V7X_LANES), lambda i, j: (i * (s // CHUNK) + j, 0)),
        ],
        out_shape=[
            jax.ShapeDtypeStruct((b, s, D_RWKV), BF16),
            jax.ShapeDtypeStruct((z_block * n_steps, V7X_LANES), F32),
        ],
        scratch_shapes=[
            pltpu.VMEM((RWKV_BATCH, V7X_SUBLANES, D_RWKV_IN), F32),
            pltpu.VMEM((RWKV_BATCH, N_RWKV_GROUPS, GROUP_LANES, GROUP_LANES), F32),
        ],
        compiler_params=pltpu.CompilerParams(
            dimension_semantics=("parallel", "arbitrary"), vmem_limit_bytes=V7X_VMEM_LIMIT_BYTES),
        name="rwkv7",
    )(pa, mu, w0, a0, wwa, gup, k_k, k_a, r_k, ln_w, ln_b, tri, masks, masks.astype(BF16))


def _hgrn_sum_matrix():
    t = np.arange(CHUNK)[:, None]
    j = np.arange(CHUNK)[None, :]
    mats = [(j <= t), (j > t)]
    for lvl in range(N_LEVELS):
        s = 1 << lvl
        mid = (t // (2 * s)) * (2 * s) + s - 1
        mats.append(((j > mid) & (j <= t)).astype(np.float32) - ((j > t) & (j <= mid)).astype(np.float32))
    return np.concatenate([np.asarray(m, np.float32) for m in mats], axis=0)


def _hgrn_level_masks():
    t = np.arange(CHUNK)[:, None]
    s_ = np.arange(CHUNK)[None, :]
    masks = [t == s_]
    for lvl in range(N_LEVELS):
        s = 1 << lvl
        masks.append(((t // (2 * s)) == (s_ // (2 * s))) & ((t % (2 * s)) >= s) & ((s_ % (2 * s)) < s))
    return np.stack(masks).astype(np.float32)


def _hgrn_kernel(p_ref, cw_ref, lbl_ref, ng_ref, sm_ref, lm_ref, o_ref, prev_ref, st_ref):
    c = pl.program_id(1)

    @pl.when(c == 0)
    def _():
        prev_ref[...] = jnp.zeros_like(prev_ref)
        st_ref[...] = jnp.zeros_like(st_ref)

    l0 = lbl_ref[0:1, :]
    l1 = lbl_ref[1:2, :]
    lmax = jnp.maximum(l0, l1)
    e0 = jnp.exp(l0 - lmax)
    lb = e0 / (e0 + jnp.exp(l1 - lmax))
    ta = jnp.log(lb)

    q_lv, k_lv, q_inter, k_inter, w_chunk, val_bf = {}, {}, {}, {}, {}, {}
    for bi in range(HGRN_BATCH):
        x = p_ref[bi, :, :3 * D_HGRN]
        xe = jnp.concatenate([prev_ref[bi], x], axis=0)
        prev_ref[bi] = x[CHUNK - V7X_SUBLANES:, :]
        conv = x * cw_ref[CONV_W - 1:CONV_W, :]
        for s in range(1, CONV_W):
            conv = conv + pltpu.roll(xe, s, 0)[V7X_SUBLANES:, :] * cw_ref[CONV_W - 1 - s:CONV_W - s, :]

        qc = conv[:, :D_HGRN]
        f = conv[:, D_HGRN:2 * D_HGRN]
        val_bf[bi] = conv[:, 2 * D_HGRN:].astype(BF16)
        q = qc * _sigmoid(qc)
        e_f = jnp.exp(-jnp.abs(f))
        t_f = 1.0 + e_f
        tb = jnp.log1p(-lb) + (jnp.minimum(f, 0.0) - jnp.log(t_f))
        log_f = jnp.maximum(ta, tb) + jnp.log(1.0 + jnp.exp(-jnp.abs(ta - tb)))
        kg = (1.0 - lb) * (jnp.where(f >= 0.0, e_f, 1.0) / t_f)

        sums = _dot01_left(sm_ref[...], log_f)
        b_in = sums[0:CHUNK]
        q_inter[bi] = (q * jnp.exp(b_in)).astype(BF16)
        k_inter[bi] = (kg * jnp.exp(sums[CHUNK:2 * CHUNK])).astype(BF16)
        w_chunk[bi] = jnp.exp(b_in[CHUNK - 1:CHUNK, :])
        q_lv[bi] = [q.astype(BF16)]
        k_lv[bi] = [kg.astype(BF16)]
        for lvl in range(N_LEVELS):
            e = jnp.exp(-jnp.abs(sums[(2 + lvl) * CHUNK:(3 + lvl) * CHUNK])).astype(BF16)
            q_lv[bi].append(q_lv[bi][0] * e)
            k_lv[bi].append(k_lv[bi][0] * e)

    chains = [(bi, h) for bi in range(HGRN_BATCH) for h in range(HGRN_HEADS)]
    sl_of = lambda h: slice(h * HGRN_HEAD, (h + 1) * HGRN_HEAD)
    scores = {}
    for lvl in range(N_LEVELS + 1):
        for ch in chains:
            bi, sl = ch[0], sl_of(ch[1])
            term = _dot_nt(q_lv[bi][lvl][:, sl], k_lv[bi][lvl][:, sl]) * lm_ref[lvl]
            scores[ch] = term if lvl == 0 else scores[ch] + term
    outs = {}
    for ch in chains:
        bi, h = ch
        sl = sl_of(h)
        st = st_ref[bi, h]
        o = _dot(scores[ch].astype(BF16), val_bf[bi][:, sl]) + _dot_nt(q_inter[bi][:, sl], st.astype(BF16))
        st_ref[bi, h] = st * w_chunk[bi][:, sl] + _dot_tn(val_bf[bi][:, sl], k_inter[bi][:, sl])
        outs[ch] = o * lax.rsqrt(jnp.mean(o * o, axis=-1, keepdims=True) + RMS_EPS)
    for bi in range(HGRN_BATCH):
        gate = p_ref[bi, :, 3 * D_HGRN:]
        o = jnp.concatenate([outs[(bi, h)] for h in range(HGRN_HEADS)], axis=1)
        o_ref[bi] = (o * ng_ref[...] * (gate * _sigmoid(gate))).astype(BF16)


def _hgrn(pb, conv_w, lb_logits, norm_g):
    b, s, _ = pb.shape
    sum_m = jnp.asarray(_hgrn_sum_matrix(), BF16)
    lvl_m = jnp.asarray(_hgrn_level_masks())
    return pl.pallas_call(
        _hgrn_kernel,
        grid=(b // HGRN_BATCH, s // CHUNK),
        in_specs=[
            pl.BlockSpec((HGRN_BATCH, CHUNK, D_HGRN_IN), lambda i, j: (i, j, 0)),
            _const_spec((CONV_W, 3 * D_HGRN)),
            _const_spec(lb_logits.shape),
            _const_spec((1, D_HGRN)),
            _const_spec(sum_m.shape),
            _const_spec(lvl_m.shape),
        ],
        out_specs=pl.BlockSpec((HGRN_BATCH, CHUNK, D_HGRN), lambda i, j: (i, j, 0)),
        out_shape=jax.ShapeDtypeStruct((b, s, D_HGRN), BF16),
        scratch_shapes=[
            pltpu.VMEM((HGRN_BATCH, V7X_SUBLANES, 3 * D_HGRN), F32),
            pltpu.VMEM((HGRN_BATCH, HGRN_HEADS, HGRN_HEAD, HGRN_HEAD), F32),
        ],
        compiler_params=pltpu.CompilerParams(
            dimension_semantics=("parallel", "arbitrary"), vmem_limit_bytes=V7X_VMEM_LIMIT_BYTES),
        name="hgrn2",
    )(pb, conv_w, lb_logits, norm_g, sum_m, lvl_m)


ROUTER_LANES = V7X_LANES


TOKEN_ROWS = V7X_SUBLANES
assert D_MODEL == TOKEN_ROWS * V7X_LANES


def _tiles_shape(tokens):
    return (tokens * TOKEN_ROWS, V7X_LANES)


def _store_token_tiles(ref, x):
    rows = x.shape[0]
    for s in range(TOKEN_ROWS):
        ref[pl.ds(s, rows, stride=TOKEN_ROWS), :] = x[:, s * V7X_LANES:(s + 1) * V7X_LANES]


def _load_token_tiles(ref):
    rows = ref.shape[0] // TOKEN_ROWS
    return jnp.concatenate([ref[pl.ds(s, rows, stride=TOKEN_ROWS), :] for s in range(TOKEN_ROWS)], axis=1)


_R_E1, _R_E2, _R_G1, _R_G2, _R_RANK1, _R_RANK2 = 0, 1, 2, 3, 4, 5


def _outproj_kernel(x_ref, ya_ref, yb_ref, wo_ref, g2_ref, wr_ref, br_ref, tri_ref,
                    h_ref, n_ref, meta_ref, counts_ref, carry_ref):
    @pl.when(pl.program_id(0) == 0)
    def _():
        carry_ref[...] = jnp.zeros_like(carry_ref)

    h = (x_ref[...]
         + _dot(ya_ref[...], wo_ref[:D_RWKV, :])
         + _dot(yb_ref[...], wo_ref[D_RWKV:, :]))
    h_ref[...] = h
    n = _rms(h, g2_ref[...])
    _store_token_tiles(n_ref, n)

    n_hi, n_lo = _split_bf16(n)
    logits = (_dot(n_hi, wr_ref[0]) + _dot(n_lo, wr_ref[0]) + _dot(n_hi, wr_ref[1])) + br_ref[...]
    lane = lax.broadcasted_iota(jnp.int32, logits.shape, 1).astype(F32)
    neg = -jnp.inf
    gl = jnp.where((lane >= N_EXPERTS) & (lane < N_EXPERTS + N_GROUPS), logits, neg)
    gmax = jnp.max(gl, axis=-1, keepdims=True)
    gidx = jnp.min(jnp.where(gl == gmax, lane - N_EXPERTS, ROUTER_LANES), axis=-1, keepdims=True)
    g_w = 1.0 / jnp.sum(jnp.exp(gl - gmax), axis=-1, keepdims=True)
    lo = gidx * EXPERTS_PER_GROUP
    el = jnp.where((lane >= lo) & (lane < lo + EXPERTS_PER_GROUP), logits, neg)
    m1 = jnp.max(el, axis=-1, keepdims=True)
    i1 = jnp.min(jnp.where(el == m1, lane, ROUTER_LANES), axis=-1, keepdims=True)
    el2 = jnp.where(lane == i1, neg, el)
    m2 = jnp.max(el2, axis=-1, keepdims=True)
    i2 = jnp.min(jnp.where(el2 == m2, lane, ROUTER_LANES), axis=-1, keepdims=True)
    t = jnp.exp(m2 - m1)
    w1 = 1.0 / (1.0 + t)

    onehot1 = lane == i1
    onehot2 = lane == i2
    assigned = jnp.where(onehot1 | onehot2, 1.0, 0.0)
    before = _dot(tri_ref[...], assigned.astype(BF16)) + carry_ref[...]
    rank1 = jnp.sum(jnp.where(onehot1, before, 0.0), axis=-1, keepdims=True)
    rank2 = jnp.sum(jnp.where(onehot2, before, 0.0), axis=-1, keepdims=True)
    carry_ref[...] += jnp.sum(assigned, axis=0, keepdims=True)
    counts_ref[...] = jnp.broadcast_to(carry_ref[...], counts_ref.shape)

    meta = jnp.zeros_like(logits)
    for idx, val in ((_R_E1, i1), (_R_E2, i2), (_R_G1, g_w * w1), (_R_G2, g_w * (t * w1)),
                     (_R_RANK1, rank1), (_R_RANK2, rank2)):
        meta = jnp.where(lane == idx, val, meta)
    meta_ref[...] = meta


def _outproj(x2d, ya, yb, wo_bf16, g2, wr, br):
    t = x2d.shape[0]
    row = lambda w: pl.BlockSpec((TM_PROJ, w), lambda i: (i, 0))
    tri = jnp.asarray(np.tril(np.ones((TM_PROJ, TM_PROJ), np.float32), -1), BF16)
    return pl.pallas_call(
        _outproj_kernel,
        grid=(t // TM_PROJ,),
        in_specs=[
            row(D_MODEL), row(D_RWKV), row(D_HGRN),
            _const_spec((D_RWKV + D_HGRN, D_MODEL)),
            _const_spec((1, D_MODEL)),
            _const_spec((2, D_MODEL, ROUTER_LANES)),
            _const_spec((1, ROUTER_LANES)),
            _const_spec((TM_PROJ, TM_PROJ)),
        ],
        out_specs=[row(D_MODEL), pl.BlockSpec(_tiles_shape(TM_PROJ), lambda i: (i, 0)), row(ROUTER_LANES),
                   _const_spec((V7X_SUBLANES, ROUTER_LANES))],
        out_shape=[
            jax.ShapeDtypeStruct((t, D_MODEL), F32),
            jax.ShapeDtypeStruct(_tiles_shape(t), F32),
            jax.ShapeDtypeStruct((t, ROUTER_LANES), F32),
            jax.ShapeDtypeStruct((V7X_SUBLANES, ROUTER_LANES), F32),
        ],
        scratch_shapes=[pltpu.VMEM((1, ROUTER_LANES), F32)],
        compiler_params=pltpu.CompilerParams(
            dimension_semantics=("arbitrary",), vmem_limit_bytes=V7X_VMEM_LIMIT_BYTES),
        name="outproj_router",
    )(x2d, ya, yb, wo_bf16, g2, wr, br, tri)


def _slot_kernel(meta_ref, offs_ref, pos_ref):
    meta = meta_ref[...]
    lane = lax.broadcasted_iota(jnp.int32, meta.shape, 1).astype(F32)
    offs = offs_ref[...]

    def slot(e_lane, rank_lane):
        e = meta[:, e_lane:e_lane + 1]
        return jnp.sum(jnp.where(lane == e, offs, 0.0), axis=-1, keepdims=True) + meta[:, rank_lane:rank_lane + 1]

    pos = jnp.concatenate([slot(_R_E1, _R_RANK1), slot(_R_E2, _R_RANK2)], axis=1)
    pos_ref[...] = pos.astype(jnp.int32)


def _slots(meta, offs):
    t = meta.shape[0]
    return pl.pallas_call(
        _slot_kernel,
        grid=(t // TM_SLOTS,),
        in_specs=[pl.BlockSpec((TM_SLOTS, ROUTER_LANES), lambda i: (i, 0)), _const_spec((1, ROUTER_LANES))],
        out_specs=pl.BlockSpec((TM_SLOTS, 2), lambda i: (i, 0)),
        out_shape=jax.ShapeDtypeStruct((t, 2), jnp.int32),
        compiler_params=pltpu.CompilerParams(dimension_semantics=("parallel",)),
        name="moe_slots",
    )(meta, offs)


def _token_rows(token, count=1):
    return pl.ds(pl.multiple_of(token * TOKEN_ROWS, TOKEN_ROWS), count * TOKEN_ROWS)


def _row_copy(src_ref, src_token, dst_ref, dst_token, sem):
    return pltpu.make_async_copy(src_ref.at[_token_rows(src_token)], dst_ref.at[_token_rows(dst_token)], sem)


def _dispatch_kernel(pos_ref, n_ref, xs_in_ref, xs_ref, sem):
    del xs_in_ref

    def body(r, carry):
        _row_copy(n_ref, r, xs_ref, pos_ref[2 * r], sem).start(priority=0)
        _row_copy(n_ref, r, xs_ref, pos_ref[2 * r + 1], sem).start(priority=1)
        return carry

    lax.fori_loop(0, TM_DISPATCH, body, 0, unroll=8)
    for _ in range(2):
        pltpu.make_async_copy(n_ref, xs_ref.at[_token_rows(0, TM_DISPATCH)], sem).wait()


def _dispatch(pos_flat, n2, xs0):
    t = n2.shape[0] // TOKEN_ROWS
    return pl.pallas_call(
        _dispatch_kernel,
        grid=(t // TM_DISPATCH,),
        in_specs=[
            pl.BlockSpec((2 * TM_DISPATCH,), lambda i: (i,), memory_space=pltpu.SMEM),
            pl.BlockSpec(_tiles_shape(TM_DISPATCH), lambda i: (i, 0)),
            pl.BlockSpec(memory_space=pl.ANY),
        ],
        out_specs=pl.BlockSpec(memory_space=pl.ANY),
        out_shape=jax.ShapeDtypeStruct(xs0.shape, F32),
        scratch_shapes=[pltpu.SemaphoreType.DMA(())],
        input_output_aliases={2: 0},
        compiler_params=pltpu.CompilerParams(
            dimension_semantics=("arbitrary",), has_side_effects=True),
        name="moe_dispatch",
    )(pos_flat, n2, xs0)


def _gmm_kernel(be_ref, nu_ref, xs_ref, wg_ref, wu_ref, wd_ref, y_ref, wgu_s, wd_s):
    b = pl.program_id(0)

    @pl.when(b < nu_ref[0])
    def _():
        @pl.when((b == 0) | (be_ref[b] != be_ref[jnp.maximum(b - 1, 0)]))
        def _():
            wgu_s[:, :D_EXPERT] = wg_ref[...].astype(BF16)
            wgu_s[:, D_EXPERT:] = wu_ref[...].astype(BF16)
            wd_s[...] = wd_ref[...].astype(BF16)

        gu = _dot(_load_token_tiles(xs_ref).astype(BF16), wgu_s[...])
        hg = gu[:, :D_EXPERT]
        hid = (hg * _sigmoid(hg) * gu[:, D_EXPERT:]).astype(BF16)
        _store_token_tiles(y_ref, _dot(hid, wd_s[...]))

    @pl.when(b >= nu_ref[0])
    def _():
        y_ref[...] = jnp.zeros_like(y_ref)


def _gmm(block_expert, n_used, xs, wg, wu, wd):
    n_slots = block_expert.shape[0] * TB_MOE
    last = lambda b, nu: jnp.maximum(jnp.minimum(b, nu[0] - 1), 0)
    blk = lambda b, be, nu: (last(b, nu), 0)
    wsel = lambda b, be, nu: (be[last(b, nu)], 0, 0)
    return pl.pallas_call(
        _gmm_kernel,
        grid_spec=pltpu.PrefetchScalarGridSpec(
            num_scalar_prefetch=2,
            grid=(n_slots // TB_MOE,),
            in_specs=[
                pl.BlockSpec(_tiles_shape(TB_MOE), blk),
                pl.BlockSpec((None, D_MODEL, D_EXPERT), wsel),
                pl.BlockSpec((None, D_MODEL, D_EXPERT), wsel),
                pl.BlockSpec((None, D_EXPERT, D_MODEL), wsel),
            ],
            out_specs=pl.BlockSpec(_tiles_shape(TB_MOE), lambda b, be, nu: (b, 0)),
            scratch_shapes=[
                pltpu.VMEM((D_MODEL, 2 * D_EXPERT), BF16),
                pltpu.VMEM((D_EXPERT, D_MODEL), BF16),
            ],
        ),
        out_shape=jax.ShapeDtypeStruct(_tiles_shape(n_slots), F32),
        compiler_params=pltpu.CompilerParams(
            dimension_semantics=("arbitrary",), vmem_limit_bytes=V7X_VMEM_LIMIT_BYTES),
        name="moe_gmm",
    )(block_expert, n_used, xs, wg, wu, wd)


def _combine_kernel(pos_ref, pos_next_ref, meta_ref, h_ref, gf_ref, ys_ref, o_ref, buf_ref, sem):
    i = pl.program_id(0)
    slot = lax.rem(i, 2)

    def issue(p_ref, s):
        def body(r, carry):
            _row_copy(ys_ref, p_ref[2 * r], buf_ref.at[s, 0], r, sem.at[s]).start(priority=0)
            _row_copy(ys_ref, p_ref[2 * r + 1], buf_ref.at[s, 1], r, sem.at[s]).start(priority=1)
            return carry
        lax.fori_loop(0, TM_DISPATCH, body, 0, unroll=8)

    @pl.when(i == 0)
    def _():
        issue(pos_ref, 0)

    @pl.when(i + 1 < pl.num_programs(0))
    def _():
        issue(pos_next_ref, 1 - slot)

    for j in range(2):
        pltpu.make_async_copy(ys_ref.at[_token_rows(0, TM_DISPATCH)], buf_ref.at[slot, j], sem.at[slot]).wait()

    meta = meta_ref[...]
    g1 = meta[:, _R_G1:_R_G1 + 1]
    g2 = meta[:, _R_G2:_R_G2 + 1]
    moe = g1 * _load_token_tiles(buf_ref.at[slot, 0]) + g2 * _load_token_tiles(buf_ref.at[slot, 1])
    o_ref[...] = _rms(h_ref[...] + moe, gf_ref[...])


def _combine(pos_flat, meta, h, gf, ys):
    t = h.shape[0]
    n_tiles = t // TM_DISPATCH
    return pl.pallas_call(
        _combine_kernel,
        grid=(n_tiles,),
        in_specs=[
            pl.BlockSpec((2 * TM_DISPATCH,), lambda i: (i,), memory_space=pltpu.SMEM),
            pl.BlockSpec((2 * TM_DISPATCH,), lambda i: (jnp.minimum(i + 1, n_tiles - 1),),
                         memory_space=pltpu.SMEM),
            pl.BlockSpec((TM_DISPATCH, ROUTER_LANES), lambda i: (i, 0)),
            pl.BlockSpec((TM_DISPATCH, D_MODEL), lambda i: (i, 0)),
            _const_spec((1, D_MODEL)),
            pl.BlockSpec(memory_space=pl.ANY),
        ],
        out_specs=pl.BlockSpec((TM_DISPATCH, D_MODEL), lambda i: (i, 0)),
        out_shape=jax.ShapeDtypeStruct((t, D_MODEL), F32),
        scratch_shapes=[
            pltpu.VMEM((2, 2) + _tiles_shape(TM_DISPATCH), F32),
            pltpu.SemaphoreType.DMA((2,)),
        ],
        compiler_params=pltpu.CompilerParams(
            dimension_semantics=("arbitrary",), vmem_limit_bytes=V7X_VMEM_LIMIT_BYTES),
        name="moe_combine",
    )(pos_flat, pos_flat, meta, h, gf, ys)


def _moe_blocks(t):
    return (2 * t + N_EXPERTS * (TB_MOE - 1) + TB_MOE - 1) // TB_MOE


def _moe(n2, meta, counts, h, wg, wu, wd, gf, xs0):
    n_blocks = _moe_blocks(h.shape[0])
    cnt = counts[0, :N_EXPERTS].astype(jnp.int32)
    blocks = (cnt + TB_MOE - 1) // TB_MOE
    ends = jnp.cumsum(blocks)
    offs = ((ends - blocks) * TB_MOE).astype(F32)
    offs_row = jnp.zeros((1, ROUTER_LANES), F32).at[0, :N_EXPERTS].set(offs)
    block_ids = jnp.arange(n_blocks, dtype=jnp.int32)
    block_expert = jnp.minimum(
        jnp.sum((ends[None, :] <= block_ids[:, None]).astype(jnp.int32), axis=1), N_EXPERTS - 1)
    n_used = ends[-1:].astype(jnp.int32)

    pos = _slots(meta, offs_row)
    pos_flat = pos.reshape(-1)
    xs = _dispatch(pos_flat, n2, xs0)
    ys = _gmm(block_expert, n_used, xs, wg, wu, wd)
    return _combine(pos_flat, meta, h, gf, ys)


def kernel(x, norm1_g, w_in, rwkv_mu, rwkv_w0, rwkv_w_up, rwkv_a0, rwkv_a_up, rwkv_g_up, rwkv_k_k,
           rwkv_k_a, rwkv_r_k, rwkv_ln_w, rwkv_ln_b, hgrn_conv_w, hgrn_lb_logits, hgrn_norm_g, w_out,
           norm2_g, router_g_w, router_g_b, router_e_w, router_e_b, exp_w_gate, exp_w_up, exp_w_down,
           final_norm_g):
    b, s, d = x.shape
    t = b * s
    l = 0
    x2d = x.reshape(t, d)
    row = lambda a: a.reshape(1, -1)

    pa, pb = _inproj(x2d, row(norm1_g[l]), w_in[l].astype(BF16))

    wwa = jnp.zeros((RANK_W + RANK_A, 2 * D_RWKV), F32)
    wwa = wwa.at[:RANK_W, :D_RWKV].set(rwkv_w_up[l]).at[RANK_W:, D_RWKV:].set(rwkv_a_up[l]).astype(BF16)
    ya, xs0 = _rwkv(pa.reshape(b, s, D_RWKV_IN), row(rwkv_mu[l]), row(rwkv_w0[l]), row(rwkv_a0[l]), wwa,
                    rwkv_g_up[l].astype(BF16), row(rwkv_k_k[l]), row(rwkv_k_a[l]), row(rwkv_r_k[l]),
                    row(rwkv_ln_w[l]), row(rwkv_ln_b[l]), _moe_blocks(t) * TB_MOE * TOKEN_ROWS)
    yb = _hgrn(pb.reshape(b, s, D_HGRN_IN), hgrn_conv_w[l], hgrn_lb_logits, row(hgrn_norm_g[l]))

    wr = jnp.zeros((D_MODEL, ROUTER_LANES), F32)
    wr = wr.at[:, :N_EXPERTS].set(router_e_w[l]).at[:, N_EXPERTS:N_EXPERTS + N_GROUPS].set(router_g_w[l])
    br = jnp.zeros((1, ROUTER_LANES), F32)
    br = br.at[0, :N_EXPERTS].set(router_e_b[l]).at[0, N_EXPERTS:N_EXPERTS + N_GROUPS].set(router_g_b[l])
    wr_hi = wr.astype(BF16)
    wr_split = jnp.stack([wr_hi, (wr - wr_hi.astype(F32)).astype(BF16)])
    h, n2, meta, counts = _outproj(x2d, ya.reshape(t, D_RWKV), yb.reshape(t, D_HGRN),
                                   w_out[l].astype(BF16), row(norm2_g[l]), wr_split, br)

    out = _moe(n2, meta, counts, h, exp_w_gate[l], exp_w_up[l], exp_w_down[l], row(final_norm_g), xs0)
    return out.reshape(b, s, d)
```

```python
import numpy as np
import jax
import jax.numpy as jnp
from jax import lax
from jax.experimental import pallas as pl
from jax.experimental.pallas import tpu as pltpu

F32 = jnp.float32
BF16 = jnp.bfloat16

D_MODEL = 1024
D_RWKV = 512
D_HGRN = 512
RWKV_HEAD = 64
RANK_W = 64
RANK_A = 64
RANK_G = 128
HGRN_HEAD = 128
HGRN_HEADS = D_HGRN // HGRN_HEAD
CONV_W = 4
N_GROUPS = 4
EXPERTS_PER_GROUP = 8
N_EXPERTS = N_GROUPS * EXPERTS_PER_GROUP
D_EXPERT = 512
RMS_EPS = 1e-6
GN_EPS = 64e-5
L2_EPS = 1e-12
D_RWKV_IN = 3 * D_RWKV + RANK_W + RANK_A + RANK_G
D_HGRN_IN = 4 * D_HGRN

V7X_LANES = 128
V7X_SUBLANES = 8
V7X_MXU_DIM = 256
V7X_VMEM_LIMIT_BYTES = 56 * 1024 * 1024

CHUNK = 64
GROUP_LANES = V7X_MXU_DIM
HEADS_PER_GROUP = GROUP_LANES // RWKV_HEAD
N_RWKV_GROUPS = D_RWKV // GROUP_LANES
N_LEVELS = 6
HGRN_BATCH = 4
RWKV_BATCH = 4
TM_PROJ = 512
TM_SLOTS = 2048
TM_DISPATCH = 512
TB_MOE = 512


def _dot(a, b):
    return jnp.dot(a, b, preferred_element_type=F32)


def _dot_nt(a, b):
    return lax.dot_general(a, b, (((1,), (1,)), ((), ())), preferred_element_type=F32)


def _dot_tn(a, b):
    return lax.dot_general(a, b, (((0,), (0,)), ((), ())), preferred_element_type=F32)


def _split_bf16(x):
    hi = x.astype(BF16)
    lo = (x - hi.astype(F32)).astype(BF16)
    return hi, lo


def _dot01_left(m01, x):
    hi, lo = _split_bf16(x)
    return _dot(m01, hi) + _dot(m01, lo)


def _log_sigmoid(x):
    return jnp.minimum(x, 0.0) - jnp.log(1.0 + jnp.exp(-jnp.abs(x)))


def _sigmoid(x):
    return 1.0 / (1.0 + jnp.exp(-x))


def _rms(x, gain):
    return x * lax.rsqrt(jnp.mean(x * x, axis=-1, keepdims=True) + RMS_EPS) * gain


def _const_spec(shape):
    nd = len(shape)
    return pl.BlockSpec(shape, lambda *_: (0,) * nd)


def _inproj_kernel(x_ref, g_ref, w_ref, pa_ref, pb_ref):
    n = _rms(x_ref[...], g_ref[...]).astype(BF16)
    pa_ref[...] = _dot(n, w_ref[:, :D_RWKV_IN])
    pb_ref[...] = _dot(n, w_ref[:, D_RWKV_IN:])


def _inproj(x2d, gain, w_bf16):
    t = x2d.shape[0]
    return pl.pallas_call(
        _inproj_kernel,
        grid=(t // TM_PROJ,),
        in_specs=[
            pl.BlockSpec((TM_PROJ, D_MODEL), lambda i: (i, 0)),
            _const_spec((1, D_MODEL)),
            _const_spec((D_MODEL, D_RWKV_IN + D_HGRN_IN)),
        ],
        out_specs=[
            pl.BlockSpec((TM_PROJ, D_RWKV_IN), lambda i: (i, 0)),
            pl.BlockSpec((TM_PROJ, D_HGRN_IN), lambda i: (i, 0)),
        ],
        out_shape=[
            jax.ShapeDtypeStruct((t, D_RWKV_IN), F32),
            jax.ShapeDtypeStruct((t, D_HGRN_IN), F32),
        ],
        compiler_params=pltpu.CompilerParams(
            dimension_semantics=("parallel",), vmem_limit_bytes=V7X_VMEM_LIMIT_BYTES),
        name="inproj",
    )(x2d, gain, w_bf16)


_M_EYE, _M_BD, _M_STRICT, _M_INCL, _M_OFF0 = 0, 1, 2, 3, 4


def _rwkv_masks():
    i = np.arange(GROUP_LANES)[:, None]
    j = np.arange(GROUP_LANES)[None, :]
    bd = (i // CHUNK) == (j // CHUNK)
    masks = [i == j, bd, (j % CHUNK) < (i % CHUNK), (j % CHUNK) <= (i % CHUNK)]
    for lvl in range(N_LEVELS):
        s = 1 << lvl
        masks.append(((i // (2 * s)) == (j // (2 * s))) & ((i % (2 * s)) >= s) & ((j % (2 * s)) < s))
    return np.stack(masks).astype(np.float32)


def _tile_rows(x, n):
    return jnp.concatenate([x] * n, axis=0)


def _rwkv_kernel(p_ref, mu_ref, w0_ref, a0_ref, wwa_ref, gup_ref, kk_ref, ka_ref, rk_ref, lnw_ref,
                 lnb_ref, tri_ref, m_ref, mb_ref, o_ref, z_ref, prev_ref, st_ref):
    c = pl.program_id(1)

    @pl.when(c == 0)
    def _():
        prev_ref[...] = jnp.zeros_like(prev_ref)
        st_ref[...] = jnp.zeros_like(st_ref)

    rows = RWKV_BATCH * CHUNK
    p = p_ref[...].reshape(rows, D_RWKV_IN)
    row = lax.broadcasted_iota(jnp.int32, p.shape, 0)
    shifted = pltpu.roll(p, 1, 0)
    for bi in range(RWKV_BATCH):
        shifted = jnp.where(row == bi * CHUNK, prev_ref[bi, V7X_SUBLANES - 1:V7X_SUBLANES, :], shifted)
        prev_ref[bi] = p[(bi + 1) * CHUNK - V7X_SUBLANES:(bi + 1) * CHUNK, :]
    p = p + mu_ref[...] * (shifted - p)

    r = p[:, 0:D_RWKV]
    k = p[:, D_RWKV:2 * D_RWKV]
    v = p[:, 2 * D_RWKV:3 * D_RWKV]
    x_wa = p[:, 3 * D_RWKV:3 * D_RWKV + RANK_W + RANK_A]
    dg = p[:, 3 * D_RWKV + RANK_W + RANK_A:]

    lane = lax.broadcasted_iota(jnp.int32, x_wa.shape, 1)
    t_wa = jnp.where(lane < RANK_W, jnp.tanh(x_wa), x_wa).astype(BF16)
    wa = _dot(t_wa, wwa_ref[...])
    w_lin = w0_ref[...] + wa[:, :D_RWKV]
    w = _log_sigmoid(w_lin) - 0.5
    log_decay = -jnp.exp(w)
    a = _sigmoid(a0_ref[...] + wa[:, D_RWKV:])
    g = _dot(_sigmoid(dg).astype(BF16), gup_ref[...])

    bd_bf = mb_ref[_M_BD]

    def head_sum(x):
        n = N_RWKV_GROUPS
        parts = [half[:, gi * GROUP_LANES:(gi + 1) * GROUP_LANES] for half in _split_bf16(x) for gi in range(n)]
        s = _dot(jnp.concatenate(parts, axis=0), bd_bf)
        return jnp.concatenate(
            [s[gi * rows:(gi + 1) * rows] + s[(n + gi) * rows:(n + gi + 1) * rows] for gi in range(n)], axis=1)

    kk = k * kk_ref[...]
    kk = kk / jnp.maximum(jnp.sqrt(head_sum(kk * kk)), L2_EPS)
    k = k * (1.0 + (a - 1.0) * ka_ref[...])
    kb = kk * a

    sums = _dot01_left(tri_ref[...], log_decay)
    cum = sums[:rows]
    e_in = jnp.exp(cum)
    e_ex = jnp.exp(cum - log_decay)
    e_inv = jnp.exp(-cum)
    e_rem = jnp.exp(sums[rows:])

    r_hat = (r * e_in).astype(BF16)
    kk_hat = (kk * e_ex).astype(BF16)
    k_inv = (k * e_inv).astype(BF16)
    b_inv = (kb * e_inv).astype(BF16)
    k_rem = (k * e_rem).astype(BF16)
    b_rem = (kb * e_rem).astype(BF16)
    v_bf = v.astype(BF16)

    eye = m_ref[_M_EYE]
    bd = m_ref[_M_BD]
    strict_lc = m_ref[_M_STRICT][:CHUNK]
    incl_lc = m_ref[_M_INCL][:CHUNK]

    def block_diag(x_bf):
        return _tile_rows(x_bf, HEADS_PER_GROUP) * bd_bf

    chains = [(bi, gi) for bi in range(RWKV_BATCH) for gi in range(N_RWKV_GROUPS)]
    rs_of = lambda bi: slice(bi * CHUNK, (bi + 1) * CHUNK)
    sl_of = lambda gi: slice(gi * GROUP_LANES, (gi + 1) * GROUP_LANES)

    lhs, a_kk, a_rk, a_rb, a_kb_bd, inv = {}, {}, {}, {}, {}, {}
    for ch in chains:
        rs, sl = rs_of(ch[0]), sl_of(ch[1])
        lhs[ch] = jnp.concatenate([kk_hat[rs, sl], r_hat[rs, sl]], axis=0)
        a_k = _dot_nt(lhs[ch], block_diag(k_inv[rs, sl]))
        a_b = _dot_nt(lhs[ch], block_diag(b_inv[rs, sl]))
        a_kk[ch] = (a_k[:CHUNK] * strict_lc).astype(BF16)
        a_rk[ch] = (a_k[CHUNK:] * incl_lc).astype(BF16)
        a_rb[ch] = (a_b[CHUNK:] * incl_lc).astype(BF16)
        a_kb_bd[ch] = block_diag((a_b[:CHUNK] * strict_lc).astype(BF16))
        inv[ch] = eye - (a_kb_bd[ch] * mb_ref[_M_OFF0]).astype(F32)

    for lvl in range(1, N_LEVELS):
        inv_bf = {ch: inv[ch].astype(BF16) for ch in chains}
        mid = {ch: _dot(inv_bf[ch], a_kb_bd[ch] * mb_ref[_M_OFF0 + lvl]).astype(BF16) for ch in chains}
        for ch in chains:
            inv[ch] = inv[ch] - _dot(mid[ch], inv_bf[ch])

    ys = {}
    for ch in chains:
        bi, gi = ch
        rs, sl = rs_of(bi), sl_of(gi)
        inv_lc = inv[ch][0:CHUNK]
        for h in range(1, HEADS_PER_GROUP):
            inv_lc = inv_lc + inv[ch][h * CHUNK:(h + 1) * CHUNK]
        s_bd = st_ref[bi, gi]
        v_bd = block_diag(v_bf[rs, sl])
        from_state = _dot_nt(lhs[ch], s_bd.astype(BF16))
        rhs = from_state[:CHUNK] + _dot(a_kk[ch], v_bd)
        u_bf = _dot(inv_lc.astype(BF16), block_diag(rhs.astype(BF16))).astype(BF16)
        ys[ch] = from_state[CHUNK:] + _dot(a_rk[ch], v_bd) - _dot(a_rb[ch], block_diag(u_bf))

        vu = jnp.concatenate([v_bf[rs, sl], -u_bf], axis=0)
        kb_rem = jnp.concatenate([k_rem[rs, sl], b_rem[rs, sl]], axis=0)
        w_chunk = e_in[(bi + 1) * CHUNK - 1:(bi + 1) * CHUNK, sl]
        st_ref[bi, gi] = s_bd * w_chunk + _dot_tn(vu, kb_rem) * bd

    y = jnp.concatenate(
        [jnp.concatenate([ys[(bi, gi)] for gi in range(N_RWKV_GROUPS)], axis=1) for bi in range(RWKV_BATCH)],
        axis=0)
    inv_n = 1.0 / RWKV_HEAD
    mean = head_sum(y) * inv_n
    yc = y - mean
    var = head_sum(yc * yc) * inv_n
    yn = yc * lax.rsqrt(var + GN_EPS) * lnw_ref[...] + lnb_ref[...]
    bonus = head_sum(r * k * rk_ref[...])
    o_ref[...] = ((yn + bonus * v) * g).astype(BF16).reshape(RWKV_BATCH, CHUNK, D_RWKV)
    z_ref[...] = jnp.zeros_like(z_ref)


def _rwkv(pa, mu, w0, a0, wwa, gup, k_k, k_a, r_k, ln_w, ln_b, zero_rows):
    b, s, _ = pa.shape
    n_steps = (b // RWKV_BATCH) * (s // CHUNK)
    z_block = -(-zero_rows // (n_steps * V7X_SUBLANES)) * V7X_SUBLANES
    tril = np.tril(np.ones((CHUNK, CHUNK), np.float32))
    eye_b = np.eye(RWKV_BATCH, dtype=np.float32)
    tri = jnp.asarray(np.concatenate([np.kron(eye_b, tril), np.kron(eye_b, 1.0 - tril)], axis=0), BF16)
    masks = jnp.asarray(_rwkv_masks())
    vec = _const_spec((1, D_RWKV))
    return pl.pallas_call(
        _rwkv_kernel,
        grid=(b // RWKV_BATCH, s // CHUNK),
        in_specs=[
            pl.BlockSpec((RWKV_BATCH, CHUNK, D_RWKV_IN), lambda i, j: (i, j, 0)),
            _const_spec((1, D_RWKV_IN)),
            vec, vec,
            _const_spec((RANK_W + RANK_A, 2 * D_RWKV)),
            _const_spec((RANK_G, D_RWKV)),
            vec, vec, vec, vec, vec,
            _const_spec(tri.shape),
            _const_spec(masks.shape),
            _const_spec(masks.shape),
        ],
        out_specs=[
            pl.BlockSpec((RWKV_BATCH, CHUNK, D_RWKV), lambda i, j: (i, j, 0)),
            pl.BlockSpec((z_block, V7X_LANES), lambda i, j: (i * (s // CHUNK) + j, 0)),
        ],
        out_shape=[
            jax.ShapeDtypeStruct((b, s, D_RWKV), BF16),
            jax.ShapeDtypeStruct((z_block * n_steps, V7X_LANES), F32),
        ],
        scratch_shapes=[
            pltpu.VMEM((RWKV_BATCH, V7X_SUBLANES, D_RWKV_IN), F32),
            pltpu.VMEM((RWKV_BATCH, N_RWKV_GROUPS, GROUP_LANES, GROUP_LANES), F32),
        ],
        compiler_params=pltpu.CompilerParams(
            dimension_semantics=("parallel", "arbitrary"), vmem_limit_bytes=V7X_VMEM_LIMIT_BYTES),
        name="rwkv7",
    )(pa, mu, w0, a0, wwa, gup, k_k, k_a, r_k, ln_w, ln_b, tri, masks, masks.astype(BF16))


def _hgrn_sum_matrix():
    t = np.arange(CHUNK)[:, None]
    j = np.arange(CHUNK)[None, :]
    mats = [(j <= t), (j > t)]
    for lvl in range(N_LEVELS):
        s = 1 << lvl
        mid = (t // (2 * s)) * (2 * s) + s - 1
        mats.append(((j > mid) & (j <= t)).astype(np.float32) - ((j > t) & (j <= mid)).astype(np.float32))
    return np.concatenate([np.asarray(m, np.float32) for m in mats], axis=0)


def _hgrn_level_masks():
    t = np.arange(CHUNK)[:, None]
    s_ = np.arange(CHUNK)[None, :]
    masks = [t == s_]
    for lvl in range(N_LEVELS):
        s = 1 << lvl
        masks.append(((t // (2 * s)) == (s_ // (2 * s))) & ((t % (2 * s)) >= s) & ((s_ % (2 * s)) < s))
    return np.stack(masks).astype(np.float32)


def _hgrn_kernel(p_ref, cw_ref, lbl_ref, ng_ref, sm_ref, lm_ref, o_ref, prev_ref, st_ref):
    c = pl.program_id(1)

    @pl.when(c == 0)
    def _():
        prev_ref[...] = jnp.zeros_like(prev_ref)
        st_ref[...] = jnp.zeros_like(st_ref)

    l0 = lbl_ref[0:1, :]
    l1 = lbl_ref[1:2, :]
    lmax = jnp.maximum(l0, l1)
    e0 = jnp.exp(l0 - lmax)
    lb = e0 / (e0 + jnp.exp(l1 - lmax))
    ta = jnp.log(lb)

    q_lv, k_lv, q_inter, k_inter, w_chunk, val_bf = {}, {}, {}, {}, {}, {}
    for bi in range(HGRN_BATCH):
        x = p_ref[bi, :, :3 * D_HGRN]
        xe = jnp.concatenate([prev_ref[bi], x], axis=0)
        prev_ref[bi] = x[CHUNK - V7X_SUBLANES:, :]
        conv = x * cw_ref[CONV_W - 1:CONV_W, :]
        for s in range(1, CONV_W):
            conv = conv + pltpu.roll(xe, s, 0)[V7X_SUBLANES:, :] * cw_ref[CONV_W - 1 - s:CONV_W - s, :]

        qc = conv[:, :D_HGRN]
        f = conv[:, D_HGRN:2 * D_HGRN]
        val_bf[bi] = conv[:, 2 * D_HGRN:].astype(BF16)
        q = qc * _sigmoid(qc)
        e_f = jnp.exp(-jnp.abs(f))
        t_f = 1.0 + e_f
        tb = jnp.log1p(-lb) + (jnp.minimum(f, 0.0) - jnp.log(t_f))
        log_f = jnp.maximum(ta, tb) + jnp.log(1.0 + jnp.exp(-jnp.abs(ta - tb)))
        kg = (1.0 - lb) * (jnp.where(f >= 0.0, e_f, 1.0) / t_f)

        sums = _dot01_left(sm_ref[...], log_f)
        b_in = sums[0:CHUNK]
        q_inter[bi] = (q * jnp.exp(b_in)).astype(BF16)
        k_inter[bi] = (kg * jnp.exp(sums[CHUNK:2 * CHUNK])).astype(BF16)
        w_chunk[bi] = jnp.exp(b_in[CHUNK - 1:CHUNK, :])
        q_lv[bi] = [q.astype(BF16)]
        k_lv[bi] = [kg.astype(BF16)]
        for lvl in range(N_LEVELS):
            e = jnp.exp(-jnp.abs(sums[(2 + lvl) * CHUNK:(3 + lvl) * CHUNK])).astype(BF16)
            q_lv[bi].append(q_lv[bi][0] * e)
            k_lv[bi].append(k_lv[bi][0] * e)

    chains = [(bi, h) for bi in range(HGRN_BATCH) for h in range(HGRN_HEADS)]
    sl_of = lambda h: slice(h * HGRN_HEAD, (h + 1) * HGRN_HEAD)
    scores = {}
    for lvl in range(N_LEVELS + 1):
        for ch in chains:
            bi, sl = ch[0], sl_of(ch[1])
            term = _dot_nt(q_lv[bi][lvl][:, sl], k_lv[bi][lvl][:, sl]) * lm_ref[lvl]
            scores[ch] = term if lvl == 0 else scores[ch] + term
    outs = {}
    for ch in chains:
        bi, h = ch
        sl = sl_of(h)
        st = st_ref[bi, h]
        o = _dot(scores[ch].astype(BF16), val_bf[bi][:, sl]) + _dot_nt(q_inter[bi][:, sl], st.astype(BF16))
        st_ref[bi, h] = st * w_chunk[bi][:, sl] + _dot_tn(val_bf[bi][:, sl], k_inter[bi][:, sl])
        outs[ch] = o * lax.rsqrt(jnp.mean(o * o, axis=-1, keepdims=True) + RMS_EPS)
    for bi in range(HGRN_BATCH):
        gate = p_ref[bi, :, 3 * D_HGRN:]
        o = jnp.concatenate([outs[(bi, h)] for h in range(HGRN_HEADS)], axis=1)
        o_ref[bi] = (o * ng_ref[...] * (gate * _sigmoid(gate))).astype(BF16)


def _hgrn(pb, conv_w, lb_logits, norm_g):
    b, s, _ = pb.shape
    sum_m = jnp.asarray(_hgrn_sum_matrix(), BF16)
    lvl_m = jnp.asarray(_hgrn_level_masks())
    return pl.pallas_call(
        _hgrn_kernel,
        grid=(b // HGRN_BATCH, s // CHUNK),
        in_specs=[
            pl.BlockSpec((HGRN_BATCH, CHUNK, D_HGRN_IN), lambda i, j: (i, j, 0)),
            _const_spec((CONV_W, 3 * D_HGRN)),
            _const_spec(lb_logits.shape),
            _const_spec((1, D_HGRN)),
            _const_spec(sum_m.shape),
            _const_spec(lvl_m.shape),
        ],
        out_specs=pl.BlockSpec((HGRN_BATCH, CHUNK, D_HGRN), lambda i, j: (i, j, 0)),
        out_shape=jax.ShapeDtypeStruct((b, s, D_HGRN), BF16),
        scratch_shapes=[
            pltpu.VMEM((HGRN_BATCH, V7X_SUBLANES, 3 * D_HGRN), F32),
            pltpu.VMEM((HGRN_BATCH, HGRN_HEADS, HGRN_HEAD, HGRN_HEAD), F32),
        ],
        compiler_params=pltpu.CompilerParams(
            dimension_semantics=("parallel", "arbitrary"), vmem_limit_bytes=V7X_VMEM_LIMIT_BYTES),
        name="hgrn2",
    )(pb, conv_w, lb_logits, norm_g, sum_m, lvl_m)


ROUTER_LANES = V7X_LANES


TOKEN_ROWS = V7X_SUBLANES
assert D_MODEL == TOKEN_ROWS * V7X_LANES


def _tiles_shape(tokens):
    return (tokens * TOKEN_ROWS, V7X_LANES)


def _store_token_tiles(ref, x):
    rows = x.shape[0]
    for s in range(TOKEN_ROWS):
        ref[pl.ds(s, rows, stride=TOKEN_ROWS), :] = x[:, s * V7X_LANES:(s + 1) * V7X_LANES]


def _load_token_tiles(ref):
    rows = ref.shape[0] // TOKEN_ROWS
    return jnp.concatenate([ref[pl.ds(s, rows, stride=TOKEN_ROWS), :] for s in range(TOKEN_ROWS)], axis=1)


_R_E1, _R_E2, _R_G1, _R_G2, _R_RANK1, _R_RANK2 = 0, 1, 2, 3, 4, 5


def _outproj_kernel(x_ref, ya_ref, yb_ref, wo_ref, g2_ref, wr_ref, br_ref, tri_ref,
                    h_ref, n_ref, meta_ref, counts_ref, carry_ref):
    @pl.when(pl.program_id(0) == 0)
    def _():
        carry_ref[...] = jnp.zeros_like(carry_ref)

    h = (x_ref[...]
         + _dot(ya_ref[...], wo_ref[:D_RWKV, :])
         + _dot(yb_ref[...], wo_ref[D_RWKV:, :]))
    h_ref[...] = h
    n = _rms(h, g2_ref[...])
    _store_token_tiles(n_ref, n)

    n_hi, n_lo = _split_bf16(n)
    logits = (_dot(n_hi, wr_ref[0]) + _dot(n_lo, wr_ref[0]) + _dot(n_hi, wr_ref[1])) + br_ref[...]
    lane = lax.broadcasted_iota(jnp.int32, logits.shape, 1).astype(F32)
    neg = -jnp.inf
    gl = jnp.where((lane >= N_EXPERTS) & (lane < N_EXPERTS + N_GROUPS), logits, neg)
    gmax = jnp.max(gl, axis=-1, keepdims=True)
    gidx = jnp.min(jnp.where(gl == gmax, lane - N_EXPERTS, ROUTER_LANES), axis=-1, keepdims=True)
    g_w = 1.0 / jnp.sum(jnp.exp(gl - gmax), axis=-1, keepdims=True)
    lo = gidx * EXPERTS_PER_GROUP
    el = jnp.where((lane >= lo) & (lane < lo + EXPERTS_PER_GROUP), logits, neg)
    m1 = jnp.max(el, axis=-1, keepdims=True)
    i1 = jnp.min(jnp.where(el == m1, lane, ROUTER_LANES), axis=-1, keepdims=True)
    el2 = jnp.where(lane == i1, neg, el)
    m2 = jnp.max(el2, axis=-1, keepdims=True)
    i2 = jnp.min(jnp.where(el2 == m2, lane, ROUTER_LANES), axis=-1, keepdims=True)
    t = jnp.exp(m2 - m1)
    w1 = 1.0 / (1.0 + t)

    onehot1 = lane == i1
    onehot2 = lane == i2
    assigned = jnp.where(onehot1 | onehot2, 1.0, 0.0)
    before = _dot(tri_ref[...], assigned.astype(BF16)) + carry_ref[...]
    rank1 = jnp.sum(jnp.where(onehot1, before, 0.0), axis=-1, keepdims=True)
    rank2 = jnp.sum(jnp.where(onehot2, before, 0.0), axis=-1, keepdims=True)
    carry_ref[...] += jnp.sum(assigned, axis=0, keepdims=True)
    counts_ref[...] = jnp.broadcast_to(carry_ref[...], counts_ref.shape)

    meta = jnp.zeros_like(logits)
    for idx, val in ((_R_E1, i1), (_R_E2, i2), (_R_G1, g_w * w1), (_R_G2, g_w * (t * w1)),
                     (_R_RANK1, rank1), (_R_RANK2, rank2)):
        meta = jnp.where(lane == idx, val, meta)
    meta_ref[...] = meta


def _outproj(x2d, ya, yb, wo_bf16, g2, wr, br):
    t = x2d.shape[0]
    row = lambda w: pl.BlockSpec((TM_PROJ, w), lambda i: (i, 0))
    tri = jnp.asarray(np.tril(np.ones((TM_PROJ, TM_PROJ), np.float32), -1), BF16)
    return pl.pallas_call(
        _outproj_kernel,
        grid=(t // TM_PROJ,),
        in_specs=[
            row(D_MODEL), row(D_RWKV), row(D_HGRN),
            _const_spec((D_RWKV + D_HGRN, D_MODEL)),
            _const_spec((1, D_MODEL)),
            _const_spec((2, D_MODEL, ROUTER_LANES)),
            _const_spec((1, ROUTER_LANES)),
            _const_spec((TM_PROJ, TM_PROJ)),
        ],
        out_specs=[row(D_MODEL), pl.BlockSpec(_tiles_shape(TM_PROJ), lambda i: (i, 0)), row(ROUTER_LANES),
                   _const_spec((V7X_SUBLANES, ROUTER_LANES))],
        out_shape=[
            jax.ShapeDtypeStruct((t, D_MODEL), F32),
            jax.ShapeDtypeStruct(_tiles_shape(t), F32),
            jax.ShapeDtypeStruct((t, ROUTER_LANES), F32),
            jax.ShapeDtypeStruct((V7X_SUBLANES, ROUTER_LANES), F32),
        ],
        scratch_shapes=[pltpu.VMEM((1, ROUTER_LANES), F32)],
        compiler_params=pltpu.CompilerParams(
            dimension_semantics=("arbitrary",), vmem_limit_bytes=V7X_VMEM_LIMIT_BYTES),
        name="outproj_router",
    )(x2d, ya, yb, wo_bf16, g2, wr, br, tri)


def _slot_kernel(meta_ref, offs_ref, pos_ref):
    meta = meta_ref[...]
    lane = lax.broadcasted_iota(jnp.int32, meta.shape, 1).astype(F32)
    offs = offs_ref[...]

    def slot(e_lane, rank_lane):
        e = meta[:, e_lane:e_lane + 1]
        return jnp.sum(jnp.where(lane == e, offs, 0.0), axis=-1, keepdims=True) + meta[:, rank_lane:rank_lane + 1]

    pos = jnp.concatenate([slot(_R_E1, _R_RANK1), slot(_R_E2, _R_RANK2)], axis=1)
    pos_ref[...] = pos.astype(jnp.int32)


def _slots(meta, offs):
    t = meta.shape[0]
    return pl.pallas_call(
        _slot_kernel,
        grid=(t // TM_SLOTS,),
        in_specs=[pl.BlockSpec((TM_SLOTS, ROUTER_LANES), lambda i: (i, 0)), _const_spec((1, ROUTER_LANES))],
        out_specs=pl.BlockSpec((TM_SLOTS, 2), lambda i: (i, 0)),
        out_shape=jax.ShapeDtypeStruct((t, 2), jnp.int32),
        compiler_params=pltpu.CompilerParams(dimension_semantics=("parallel",)),
        name="moe_slots",
    )(meta, offs)


def _token_rows(token, count=1):
    return pl.ds(pl.multiple_of(token * TOKEN_ROWS, TOKEN_ROWS), count * TOKEN_ROWS)


def _row_copy(src_ref, src_token, dst_ref, dst_token, sem):
    return pltpu.make_async_copy(src_ref.at[_token_rows(src_token)], dst_ref.at[_token_rows(dst_token)], sem)


def _dispatch_kernel(pos_ref, n_ref, xs_in_ref, xs_ref, sem):
    del xs_in_ref

    def body(r, carry):
        _row_copy(n_ref, r, xs_ref, pos_ref[2 * r], sem).start(priority=0)
        _row_copy(n_ref, r, xs_ref, pos_ref[2 * r + 1], sem).start(priority=1)
        return carry

    lax.fori_loop(0, TM_DISPATCH, body, 0, unroll=8)
    for _ in range(2):
        pltpu.make_async_copy(n_ref, xs_ref.at[_token_rows(0, TM_DISPATCH)], sem).wait()


def _dispatch(pos_flat, n2, xs0):
    t = n2.shape[0] // TOKEN_ROWS
    return pl.pallas_call(
        _dispatch_kernel,
        grid=(t // TM_DISPATCH,),
        in_specs=[
            pl.BlockSpec((2 * TM_DISPATCH,), lambda i: (i,), memory_space=pltpu.SMEM),
            pl.BlockSpec(_tiles_shape(TM_DISPATCH), lambda i: (i, 0)),
            pl.BlockSpec(memory_space=pl.ANY),
        ],
        out_specs=pl.BlockSpec(memory_space=pl.ANY),
        out_shape=jax.ShapeDtypeStruct(xs0.shape, F32),
        scratch_shapes=[pltpu.SemaphoreType.DMA(())],
        input_output_aliases={2: 0},
        compiler_params=pltpu.CompilerParams(
            dimension_semantics=("arbitrary",), has_side_effects=True),
        name="moe_dispatch",
    )(pos_flat, n2, xs0)


def _gmm_kernel(be_ref, nu_ref, first_ref, slot_ref, nxt_ref, xs_ref, wg_hbm, wu_hbm, wd_hbm, y_ref,
                wg_buf, wu_buf, wd_buf, wgu_s, wd_s, sem):
    b = pl.program_id(0)

    def weight_copies(e, s):
        return (pltpu.make_async_copy(wg_hbm.at[e], wg_buf.at[s], sem.at[s, 0]),
                pltpu.make_async_copy(wu_hbm.at[e], wu_buf.at[s], sem.at[s, 1]),
                pltpu.make_async_copy(wd_hbm.at[e], wd_buf.at[s], sem.at[s, 2]))

    @pl.when(b < nu_ref[0])
    def _():
        @pl.when(first_ref[b] == 1)
        def _():
            s = slot_ref[b]

            @pl.when(b == 0)
            def _():
                for c in weight_copies(be_ref[0], 0):
                    c.start()

            for c in weight_copies(be_ref[b], s):
                c.wait()

            @pl.when(nxt_ref[b] >= 0)
            def _():
                for c in weight_copies(nxt_ref[b], 1 - s):
                    c.start()

            wgu_s[:, :D_EXPERT] = wg_buf[s].astype(BF16)
            wgu_s[:, D_EXPERT:] = wu_buf[s].astype(BF16)
            wd_s[...] = wd_buf[s].astype(BF16)

        gu = _dot(_load_token_tiles(xs_ref).astype(BF16), wgu_s[...])
        hg = gu[:, :D_EXPERT]
        hid = (hg * _sigmoid(hg) * gu[:, D_EXPERT:]).astype(BF16)
        _store_token_tiles(y_ref, _dot(hid, wd_s[...]))

    @pl.when(b >= nu_ref[0])
    def _():
        y_ref[...] = jnp.zeros_like(y_ref)


def _gmm(block_expert, n_used, first, slot, nxt, xs, wg, wu, wd):
    n_slots = block_expert.shape[0] * TB_MOE
    blk = lambda b, be, nu, *_: (jnp.maximum(jnp.minimum(b, nu[0] - 1), 0), 0)
    return pl.pallas_call(
        _gmm_kernel,
        grid_spec=pltpu.PrefetchScalarGridSpec(
            num_scalar_prefetch=5,
            grid=(n_slots // TB_MOE,),
            in_specs=[
                pl.BlockSpec(_tiles_shape(TB_MOE), blk),
                pl.BlockSpec(memory_space=pl.ANY),
                pl.BlockSpec(memory_space=pl.ANY),
                pl.BlockSpec(memory_space=pl.ANY),
            ],
            out_specs=pl.BlockSpec(_tiles_shape(TB_MOE), lambda b, *_: (b, 0)),
            scratch_shapes=[
                pltpu.VMEM((2, D_MODEL, D_EXPERT), F32),
                pltpu.VMEM((2, D_MODEL, D_EXPERT), F32),
                pltpu.VMEM((2, D_EXPERT, D_MODEL), F32),
                pltpu.VMEM((D_MODEL, 2 * D_EXPERT), BF16),
                pltpu.VMEM((D_EXPERT, D_MODEL), BF16),
                pltpu.SemaphoreType.DMA((2, 3)),
            ],
        ),
        out_shape=jax.ShapeDtypeStruct(_tiles_shape(n_slots), F32),
        compiler_params=pltpu.CompilerParams(
            dimension_semantics=("arbitrary",), vmem_limit_bytes=V7X_VMEM_LIMIT_BYTES),
        name="moe_gmm",
    )(block_expert, n_used, first, slot, nxt, xs, wg, wu, wd)


def _combine_kernel(pos_ref, pos_next_ref, meta_ref, h_ref, gf_ref, ys_ref, o_ref, buf_ref, sem):
    i = pl.program_id(0)
    slot = lax.rem(i, 2)

    def issue(p_ref, s):
        def body(r, carry):
            _row_copy(ys_ref, p_ref[2 * r], buf_ref.at[s, 0], r, sem.at[s]).start(priority=0)
            _row_copy(ys_ref, p_ref[2 * r + 1], buf_ref.at[s, 1], r, sem.at[s]).start(priority=1)
            return carry
        lax.fori_loop(0, TM_DISPATCH, body, 0, unroll=8)

    @pl.when(i == 0)
    def _():
        issue(pos_ref, 0)

    @pl.when(i + 1 < pl.num_programs(0))
    def _():
        issue(pos_next_ref, 1 - slot)

    for j in range(2):
        pltpu.make_async_copy(ys_ref.at[_token_rows(0, TM_DISPATCH)], buf_ref.at[slot, j], sem.at[slot]).wait()

    meta = meta_ref[...]
    g1 = meta[:, _R_G1:_R_G1 + 1]
    g2 = meta[:, _R_G2:_R_G2 + 1]
    moe = g1 * _load_token_tiles(buf_ref.at[slot, 0]) + g2 * _load_token_tiles(buf_ref.at[slot, 1])
    o_ref[...] = _rms(h_ref[...] + moe, gf_ref[...])


def _combine(pos_flat, meta, h, gf, ys):
    t = h.shape[0]
    n_tiles = t // TM_DISPATCH
    return pl.pallas_call(
        _combine_kernel,
        grid=(n_tiles,),
        in_specs=[
            pl.BlockSpec((2 * TM_DISPATCH,), lambda i: (i,), memory_space=pltpu.SMEM),
            pl.BlockSpec((2 * TM_DISPATCH,), lambda i: (jnp.minimum(i + 1, n_tiles - 1),),
                         memory_space=pltpu.SMEM),
            pl.BlockSpec((TM_DISPATCH, ROUTER_LANES), lambda i: (i, 0)),
            pl.BlockSpec((TM_DISPATCH, D_MODEL), lambda i: (i, 0)),
            _const_spec((1, D_MODEL)),
            pl.BlockSpec(memory_space=pl.ANY),
        ],
        out_specs=pl.BlockSpec((TM_DISPATCH, D_MODEL), lambda i: (i, 0)),
        out_shape=jax.ShapeDtypeStruct((t, D_MODEL), F32),
        scratch_shapes=[
            pltpu.VMEM((2, 2) + _tiles_shape(TM_DISPATCH), F32),
            pltpu.SemaphoreType.DMA((2,)),
        ],
        compiler_params=pltpu.CompilerParams(
            dimension_semantics=("arbitrary",), vmem_limit_bytes=V7X_VMEM_LIMIT_BYTES),
        name="moe_combine",
    )(pos_flat, pos_flat, meta, h, gf, ys)


def _moe_blocks(t):
    return (2 * t + N_EXPERTS * (TB_MOE - 1) + TB_MOE - 1) // TB_MOE


def _moe(n2, meta, counts, h, wg, wu, wd, gf, xs0):
    n_blocks = _moe_blocks(h.shape[0])
    cnt = counts[0, :N_EXPERTS].astype(jnp.int32)
    blocks = (cnt + TB_MOE - 1) // TB_MOE
    ends = jnp.cumsum(blocks)
    offs = ((ends - blocks) * TB_MOE).astype(F32)
    offs_row = jnp.zeros((1, ROUTER_LANES), F32).at[0, :N_EXPERTS].set(offs)
    block_ids = jnp.arange(n_blocks, dtype=jnp.int32)
    block_expert = jnp.minimum(
        jnp.sum((ends[None, :] <= block_ids[:, None]).astype(jnp.int32), axis=1), N_EXPERTS - 1)
    n_used = ends[-1:].astype(jnp.int32)
    first = ((block_ids == 0) | (block_expert != jnp.roll(block_expert, 1))).astype(jnp.int32)
    slot = (jnp.cumsum(first) - 1) % 2
    experts = jnp.arange(N_EXPERTS, dtype=jnp.int32)
    later_used = (experts[None, :] > experts[:, None]) & (blocks[None, :] > 0)
    next_used = jnp.where(jnp.any(later_used, axis=1),
                          jnp.min(jnp.where(later_used, experts[None, :], N_EXPERTS), axis=1), -1)
    nxt = jnp.sum(jnp.where(block_expert[:, None] == experts[None, :], next_used[None, :], 0), axis=1)
    nxt = nxt.astype(jnp.int32)

    pos = _slots(meta, offs_row)
    pos_flat = pos.reshape(-1)
    xs = _dispatch(pos_flat, n2, xs0)
    ys = _gmm(block_expert, n_used, first, slot.astype(jnp.int32), nxt, xs, wg, wu, wd)
    return _combine(pos_flat, meta, h, gf, ys)


def kernel(x, norm1_g, w_in, rwkv_mu, rwkv_w0, rwkv_w_up, rwkv_a0, rwkv_a_up, rwkv_g_up, rwkv_k_k,
           rwkv_k_a, rwkv_r_k, rwkv_ln_w, rwkv_ln_b, hgrn_conv_w, hgrn_lb_logits, hgrn_norm_g, w_out,
           norm2_g, router_g_w, router_g_b, router_e_w, router_e_b, exp_w_gate, exp_w_up, exp_w_down,
           final_norm_g):
    b, s, d = x.shape
    t = b * s
    l = 0
    x2d = x.reshape(t, d)
    row = lambda a: a.reshape(1, -1)

    pa, pb = _inproj(x2d, row(norm1_g[l]), w_in[l].astype(BF16))

    wwa = jnp.zeros((RANK_W + RANK_A, 2 * D_RWKV), F32)
    wwa = wwa.at[:RANK_W, :D_RWKV].set(rwkv_w_up[l]).at[RANK_W:, D_RWKV:].set(rwkv_a_up[l]).astype(BF16)
    ya, xs0 = _rwkv(pa.reshape(b, s, D_RWKV_IN), row(rwkv_mu[l]), row(rwkv_w0[l]), row(rwkv_a0[l]), wwa,
                    rwkv_g_up[l].astype(BF16), row(rwkv_k_k[l]), row(rwkv_k_a[l]), row(rwkv_r_k[l]),
                    row(rwkv_ln_w[l]), row(rwkv_ln_b[l]), _moe_blocks(t) * TB_MOE * TOKEN_ROWS)
    yb = _hgrn(pb.reshape(b, s, D_HGRN_IN), hgrn_conv_w[l], hgrn_lb_logits, row(hgrn_norm_g[l]))

    wr = jnp.zeros((D_MODEL, ROUTER_LANES), F32)
    wr = wr.at[:, :N_EXPERTS].set(router_e_w[l]).at[:, N_EXPERTS:N_EXPERTS + N_GROUPS].set(router_g_w[l])
    br = jnp.zeros((1, ROUTER_LANES), F32)
    br = br.at[0, :N_EXPERTS].set(router_e_b[l]).at[0, N_EXPERTS:N_EXPERTS + N_GROUPS].set(router_g_b[l])
    wr_hi = wr.astype(BF16)
    wr_split = jnp.stack([wr_hi, (wr - wr_hi.astype(F32)).astype(BF16)])
    h, n2, meta, counts = _outproj(x2d, ya.reshape(t, D_RWKV), yb.reshape(t, D_HGRN),
                                   w_out[l].astype(BF16), row(norm2_g[l]), wr_split, br)

    out = _moe(n2, meta, counts, h, exp_w_gate[l], exp_w_up[l], exp_w_down[l], row(final_norm_g), xs0)
    return out.reshape(b, s, d)
```
